```python
import jax, jax.numpy as jnp
from jax import lax
import numpy as np

D_MODEL = 1024
BATCH = 16
SEQ = 2048
DEPTH = 2
DEC_BATCH = 16
DEC_SEQ = 32
PAST_LEN = 2048

CHUNK = 64
H_A = 4
DK_A = 128
DV_A = 128
H_B = 4
DK_B = 128
DV_B = 128
H_C = 8
DK_C = 128
DV_C = 128
CONV_W = 4
N_GROUPS = 4
EXP_PER_GROUP = 8
N_EXPERTS = N_GROUPS * EXP_PER_GROUP
TOP_K = 2
D_EXPERT = 512
MOE_BLOCK = 128
RMS_EPS = 1e-6
ROPE_BASE = 10000.0
N_AB = (DEPTH + 1) // 2
N_C = DEPTH // 2

AB_SPLITS = (H_A * DK_A, H_A * DK_A, H_A * DV_A, H_A * DV_A, H_A, H_A,
             H_B * DK_B, H_B * DK_B, H_B * DV_B, H_B * DV_B)
AB_IN = sum(AB_SPLITS)
AB_OUT = H_A * DV_A + H_B * DV_B
C_QKV = 2 * H_C * DK_C + H_C * DV_C
C_SPLITS = (C_QKV, H_C, H_C, H_C * DV_C)
C_IN = sum(C_SPLITS)
C_OUT = H_C * DV_C

kernel_name = "hybrid_mlstm_retention_gdn_hmoe_stream_step"


def split_cols(z, sizes):
    idx = np.cumsum(sizes)[:-1].tolist()
    return jnp.split(z, idx, axis=-1)


def rmsnorm(x, g):
    xf = x.astype(jnp.float32)
    y = xf * lax.rsqrt(jnp.mean(xf * xf, axis=-1, keepdims=True) + RMS_EPS)
    return (y * g.astype(jnp.float32)).astype(x.dtype)


def l2norm(x):
    return x * lax.rsqrt(jnp.sum(x * x, axis=-1, keepdims=True) + RMS_EPS)


def to_heads(z, h):
    b, t, _ = z.shape
    return z.reshape(b, t, h, -1).transpose(0, 2, 1, 3)


def merge_heads(z):
    b, h, t, d = z.shape
    return z.transpose(0, 2, 1, 3).reshape(b, t, h * d)


def to_chunks(a, L):
    b, h, t = a.shape[:3]
    return jnp.moveaxis(a.reshape(b, h, t // L, L, *a.shape[3:]), 2, 0)


def from_chunks(a):
    a = jnp.moveaxis(a, 0, 2)
    return a.reshape(a.shape[0], a.shape[1], -1, a.shape[-1])


def rotary(x, pos):
    half = x.shape[-1] // 2
    inv = ROPE_BASE ** (-jnp.arange(half, dtype=jnp.float32) / half)
    ang = pos.astype(jnp.float32)[:, None] * inv[None, :]
    cos, sin = jnp.cos(ang), jnp.sin(ang)
    x1, x2 = x[..., :half], x[..., half:]
    return jnp.concatenate([x1 * cos - x2 * sin, x1 * sin + x2 * cos], axis=-1)


def mlstm_scan(q, k, v, ig, lf, c0, n0, m0):
    L = min(CHUNK, q.shape[2])
    causal = jnp.tril(jnp.ones((L, L), dtype=bool))

    def step(carry, inp):
        c, n, m = carry
        qc, kc, vc, ic, fc = inp
        b = jnp.cumsum(fc, axis=-1)
        dmat = jnp.where(causal, b[..., :, None] - b[..., None, :] + ic[..., None, :], -jnp.inf)
        inter = b + m[..., None]
        m_t = jnp.maximum(inter, jnp.max(dmat, axis=-1))
        s = jnp.einsum('bhtd,bhsd->bhts', qc, kc) * jnp.exp(dmat - m_t[..., None])
        w_inter = jnp.exp(inter - m_t)
        num = w_inter[..., None] * jnp.einsum('bhtd,bhde->bhte', qc, c) + jnp.einsum('bhts,bhse->bhte', s, vc)
        den = w_inter * jnp.einsum('bhtd,bhd->bht', qc, n) + jnp.sum(s, axis=-1)
        h = num / jnp.maximum(jnp.abs(den), jnp.exp(-m_t))[..., None]
        m_new = m_t[..., -1]
        wk = jnp.exp(b[..., -1:] - b + ic - m_new[..., None])
        dec = jnp.exp(b[..., -1] + m - m_new)
        c_new = dec[..., None, None] * c + jnp.einsum('bhsd,bhse->bhde', kc * wk[..., None], vc)
        n_new = dec[..., None] * n + jnp.einsum('bhs,bhsd->bhd', wk, kc)
        return (c_new, n_new, m_new), h

    xs = tuple(to_chunks(a, L) for a in (q, k, v, ig, lf))
    (c1, n1, m1), h = lax.scan(step, (c0, n0, m0), xs)
    return from_chunks(h), c1, n1, m1


def retention_scan(q, k, v, s0):
    L = min(CHUNK, q.shape[2])
    lg = jnp.log(1.0 - 2.0 ** (-5.0 - jnp.arange(H_B, dtype=jnp.float32)))
    idx = jnp.arange(L, dtype=jnp.float32)
    rel = idx[:, None] - idx[None, :]
    causal = rel >= 0
    dmat = jnp.where(causal, jnp.exp(jnp.where(causal, rel, 0.0) * lg[:, None, None]), 0.0)
    q_dec = jnp.exp((idx + 1.0) * lg[:, None])
    k_dec = jnp.exp((L - 1.0 - idx) * lg[:, None])
    c_dec = jnp.exp(L * lg)

    def step(s, inp):
        qc, kc, vc = inp
        att = jnp.einsum('bhtd,bhsd->bhts', qc, kc) * dmat
        o = jnp.einsum('bhts,bhse->bhte', att, vc) + q_dec[:, :, None] * jnp.einsum('bhtd,bhde->bhte', qc, s)
        s = c_dec[:, None, None] * s + jnp.einsum('bhsd,bhse->bhde', kc * k_dec[:, :, None], vc)
        return s, o

    s1, o = lax.scan(step, s0, tuple(to_chunks(a, L) for a in (q, k, v)))
    return from_chunks(o), s1


def gdn_scan(q, k, v, beta, g, s0):
    L = min(CHUNK, q.shape[2])
    lower = jnp.tril(jnp.ones((L, L), dtype=bool))
    strict = jnp.tril(jnp.ones((L, L), dtype=bool), -1)
    eye = jnp.eye(L, dtype=jnp.float32)

    def step(s, inp):
        qc, kc, vc, bc, gc = inp
        G = jnp.cumsum(gc, axis=-1)
        diff = G[..., :, None] - G[..., None, :]
        gam = jnp.where(lower, jnp.exp(jnp.where(lower, diff, 0.0)), 0.0)
        a = jnp.where(strict, bc[..., :, None] * jnp.einsum('bhid,bhjd->bhij', kc, kc) * gam, 0.0)
        tri = a + eye
        eg = jnp.exp(G)
        u = lax.linalg.triangular_solve(tri, vc * bc[..., None], left_side=True, lower=True, unit_diagonal=True)
        w = lax.linalg.triangular_solve(tri, kc * (bc * eg)[..., None], left_side=True, lower=True, unit_diagonal=True)
        v_new = u - jnp.einsum('bhld,bhde->bhle', w, s)
        o = (jnp.einsum('bhld,bhde->bhle', qc * eg[..., None], s)
             + jnp.einsum('bhij,bhje->bhie', jnp.einsum('bhid,bhjd->bhij', qc, kc) * gam, v_new))
        g_last = G[..., -1]
        s = (jnp.exp(g_last)[..., None, None] * s
             + jnp.einsum('bhld,bhle->bhde', kc * jnp.exp(g_last[..., None] - G)[..., None], v_new))
        return s, o

    s1, o = lax.scan(step, s0, tuple(to_chunks(a, L) for a in (q, k, v, beta, g)))
    return from_chunks(o), s1


def ab_mixer(h, pos, w_in, gate_bias, w_out, norm_a, norm_b, c0, n0, m0, s0):
    f32 = jnp.float32
    qa, ka, va, oa, ia, fa, qb, kb, vb, gb = split_cols(h @ w_in, AB_SPLITS)
    gate_bias = gate_bias.astype(f32)
    qa = to_heads(qa, H_A).astype(f32)
    ka = to_heads(ka, H_A).astype(f32) * DK_A ** -0.5
    va = to_heads(va, H_A).astype(f32)
    ig = (ia.astype(f32) + gate_bias[:H_A]).transpose(0, 2, 1)
    lf = jax.nn.log_sigmoid(fa.astype(f32) + gate_bias[H_A:]).transpose(0, 2, 1)
    ya, c1, n1, m1 = mlstm_scan(qa, ka, va, ig, lf, c0.astype(f32), n0.astype(f32), m0.astype(f32))
    ya = rmsnorm(ya, norm_a) * jax.nn.sigmoid(to_heads(oa, H_A).astype(f32))
    qb = rotary(to_heads(qb, H_B).astype(f32), pos)
    kb = rotary(to_heads(kb, H_B).astype(f32), pos) * DK_B ** -0.5
    yb, s1 = retention_scan(qb, kb, to_heads(vb, H_B).astype(f32), s0.astype(f32))
    yb = rmsnorm(yb, norm_b) * jax.nn.silu(to_heads(gb, H_B).astype(f32))
    y = jnp.concatenate([merge_heads(ya), merge_heads(yb)], axis=-1).astype(h.dtype)
    return y @ w_out, c1, n1, m1, s1


def gdn_mixer(h, conv_buf, w_in, conv_w, a_log, dt_bias, norm_c, w_out, s0):
    f32 = jnp.float32
    qkv, bt, at, gt = split_cols(h @ w_in, C_SPLITS)
    ext = jnp.concatenate([conv_buf.astype(qkv.dtype), qkv], axis=1)
    conv = lax.conv_general_dilated(ext, conv_w[:, None, :].astype(ext.dtype), (1,), 'VALID',
                                    dimension_numbers=('NWC', 'WIO', 'NWC'), feature_group_count=C_QKV)
    q, k, v = split_cols(jax.nn.silu(conv.astype(f32)), (H_C * DK_C, H_C * DK_C, H_C * DV_C))
    q = l2norm(to_heads(q, H_C)) * DK_C ** -0.5
    k = l2norm(to_heads(k, H_C))
    v = to_heads(v, H_C)
    beta = jax.nn.sigmoid(bt.astype(f32)).transpose(0, 2, 1)
    g = (-jnp.exp(a_log.astype(f32)) * jax.nn.softplus(at.astype(f32) + dt_bias.astype(f32))).transpose(0, 2, 1)
    y, s1 = gdn_scan(q, k, v, beta, g, s0.astype(f32))
    y = rmsnorm(y, norm_c) * jax.nn.silu(to_heads(gt, H_C).astype(f32))
    y = merge_heads(y).astype(h.dtype)
    return y @ w_out, s1, ext[:, -(CONV_W - 1):, :]


def hier_moe(x, w_grp, b_grp, w_exp, b_exp, w_gate, w_up, w_down):
    f32 = jnp.float32
    n, d = x.shape
    xf = x.astype(f32)
    p_grp = jax.nn.softmax(xf @ w_grp.astype(f32) + b_grp.astype(f32), axis=-1)
    grp = jnp.argmax(p_grp, axis=-1)
    p_sel = jnp.take_along_axis(p_grp, grp[:, None], axis=-1)
    fine = (xf @ w_exp.astype(f32) + b_exp.astype(f32)).reshape(n, N_GROUPS, EXP_PER_GROUP)
    fine = jnp.take_along_axis(fine, grp[:, None, None], axis=1)[:, 0]
    w_top, e_top = lax.top_k(jax.nn.softmax(fine, axis=-1), TOP_K)
    gate = p_sel * w_top / jnp.sum(w_top, axis=-1, keepdims=True)
    eid = (grp[:, None] * EXP_PER_GROUP + e_top).reshape(-1)
    tok = jnp.repeat(jnp.arange(n), TOP_K)
    gw = gate.reshape(-1)
    order = jnp.argsort(eid)
    eid_s, tok_s, gw_s = eid[order], tok[order], gw[order]
    counts = jnp.bincount(eid, length=N_EXPERTS)
    padded = ((counts + MOE_BLOCK - 1) // MOE_BLOCK) * MOE_BLOCK
    start = jnp.cumsum(counts) - counts
    pend = jnp.cumsum(padded)
    pstart = pend - padded
    dest = pstart[eid_s] + (jnp.arange(n * TOP_K) - start[eid_s])
    n_blocks = -(-(n * TOP_K) // MOE_BLOCK) + N_EXPERTS
    rows = jnp.zeros((n_blocks * MOE_BLOCK, d), x.dtype).at[dest].set(x[tok_s])
    block_exp = jnp.minimum(jnp.searchsorted(pend, jnp.arange(n_blocks) * MOE_BLOCK, side='right'), N_EXPERTS - 1)

    def expert_block(args):
        xb, e = args
        hb = jax.nn.silu(xb @ w_gate[e]) * (xb @ w_up[e])
        return hb @ w_down[e]

    out_rows = lax.map(expert_block, (rows.reshape(n_blocks, MOE_BLOCK, d), block_exp)).reshape(-1, d)
    y = out_rows[dest] * gw_s[:, None].astype(x.dtype)
    return jax.ops.segment_sum(y, tok_s, num_segments=n)


def run_trunk(x, pos, mc, mn, mm, rs, gs, gconv, norm_mix, norm_ffn, norm_final,
              ab_w_in, ab_gate_bias, ab_w_out, ab_norm_a, ab_norm_b,
              c_w_in, c_conv_w, c_a_log, c_dt_bias, c_norm, c_w_out,
              moe_w_grp, moe_b_grp, moe_w_exp, moe_b_exp, moe_w_gate, moe_w_up, moe_w_down):
    b, t, d = x.shape
    c_l, n_l, m_l, r_l, g_l, cv_l = [], [], [], [], [], []
    for layer in range(DEPTH):
        j = layer // 2
        h = rmsnorm(x, norm_mix[layer])
        if layer % 2 == 0:
            out, c1, n1, m1, s1 = ab_mixer(h, pos, ab_w_in[j], ab_gate_bias[j], ab_w_out[j], ab_norm_a[j],
                                           ab_norm_b[j], mc[j], mn[j], mm[j], rs[j])
            c_l.append(c1); n_l.append(n1); m_l.append(m1); r_l.append(s1)
        else:
            out, s1, cv1 = gdn_mixer(h, gconv[j], c_w_in[j], c_conv_w[j], c_a_log[j], c_dt_bias[j],
                                     c_norm[j], c_w_out[j], gs[j])
            g_l.append(s1); cv_l.append(cv1)
        x = x + out
        h = rmsnorm(x, norm_ffn[layer])
        x = x + hier_moe(h.reshape(b * t, d), moe_w_grp[layer], moe_b_grp[layer], moe_w_exp[layer],
                         moe_b_exp[layer], moe_w_gate[layer], moe_w_up[layer], moe_w_down[layer]).reshape(b, t, d)
    y = rmsnorm(x, norm_final)
    return (y, jnp.stack(c_l), jnp.stack(n_l), jnp.stack(m_l), jnp.stack(r_l), jnp.stack(g_l), jnp.stack(cv_l))


def setup_inputs(seed: int = 0) -> dict:
    key = jax.random.key(seed)
    ks = list(jax.random.split(key, 40))
    f32 = jnp.float32

    def nrm(i, shape, scale):
        return scale * jax.random.normal(ks[i], shape, f32)

    dt = jnp.exp(jax.random.uniform(ks[30], (N_C, H_C), f32, np.log(1e-3), np.log(1e-1)))
    return {
        "x_prompt": nrm(0, (BATCH, SEQ, D_MODEL), 1.0),
        "x_sample": nrm(1, (DEC_BATCH, DEC_SEQ, D_MODEL), 1.0),
        "state_mlstm_c": nrm(2, (N_AB, DEC_BATCH, H_A, DK_A, DV_A), 0.2),
        "state_mlstm_n": nrm(3, (N_AB, DEC_BATCH, H_A, DK_A), 0.2),
        "state_mlstm_m": nrm(4, (N_AB, DEC_BATCH, H_A), 1.0),
        "state_ret": nrm(5, (N_AB, DEC_BATCH, H_B, DK_B, DV_B), 0.2),
        "state_gdn": nrm(6, (N_C, DEC_BATCH, H_C, DK_C, DV_C), 0.1),
        "state_gdn_conv": nrm(7, (N_C, DEC_BATCH, CONV_W - 1, C_QKV), 1.0),
        "norm_mix": 1.0 + nrm(8, (DEPTH, D_MODEL), 0.1),
        "norm_ffn": 1.0 + nrm(9, (DEPTH, D_MODEL), 0.1),
        "norm_final": 1.0 + nrm(10, (D_MODEL,), 0.1),
        "ab_w_in": nrm(11, (N_AB, D_MODEL, AB_IN), D_MODEL ** -0.5),
        "ab_gate_bias": jnp.concatenate([nrm(12, (N_AB, H_A), 0.1),
                                         3.0 + 3.0 * jax.random.uniform(ks[13], (N_AB, H_A), f32)], axis=-1),
        "ab_w_out": nrm(14, (N_AB, AB_OUT, D_MODEL), AB_OUT ** -0.5),
        "ab_norm_a": 1.0 + nrm(15, (N_AB, DV_A), 0.1),
        "ab_norm_b": 1.0 + nrm(16, (N_AB, DV_B), 0.1),
        "c_w_in": nrm(17, (N_C, D_MODEL, C_IN), D_MODEL ** -0.5),
        "c_conv_w": nrm(18, (N_C, CONV_W, C_QKV), CONV_W ** -0.5),
        "c_a_log": jnp.log(jax.random.uniform(ks[19], (N_C, H_C), f32, 1.0, 16.0)),
        "c_dt_bias": dt + jnp.log(-jnp.expm1(-dt)),
        "c_norm": 1.0 + nrm(20, (N_C, DV_C), 0.1),
        "c_w_out": nrm(21, (N_C, C_OUT, D_MODEL), C_OUT ** -0.5),
        "moe_w_grp": nrm(22, (DEPTH, D_MODEL, N_GROUPS), D_MODEL ** -0.5),
        "moe_b_grp": nrm(23, (DEPTH, N_GROUPS), 0.01),
        "moe_w_exp": nrm(24, (DEPTH, D_MODEL, N_EXPERTS), D_MODEL ** -0.5),
        "moe_b_exp": nrm(25, (DEPTH, N_EXPERTS), 0.01),
        "moe_w_gate": nrm(26, (DEPTH, N_EXPERTS, D_MODEL, D_EXPERT), D_MODEL ** -0.5),
        "moe_w_up": nrm(27, (DEPTH, N_EXPERTS, D_MODEL, D_EXPERT), D_MODEL ** -0.5),
        "moe_w_down": nrm(28, (DEPTH, N_EXPERTS, D_EXPERT, D_MODEL), D_EXPERT ** -0.5),
    }


def reference(x_prompt, x_sample, state_mlstm_c, state_mlstm_n, state_mlstm_m, state_ret, state_gdn,
              state_gdn_conv, norm_mix, norm_ffn, norm_final, ab_w_in, ab_gate_bias, ab_w_out, ab_norm_a,
              ab_norm_b, c_w_in, c_conv_w, c_a_log, c_dt_bias, c_norm, c_w_out, moe_w_grp, moe_b_grp,
              moe_w_exp, moe_b_exp, moe_w_gate, moe_w_up, moe_w_down):
    f32 = jnp.float32
    bp, tp, _ = x_prompt.shape
    ts = x_sample.shape[1]
    weights = (norm_mix, norm_ffn, norm_final, ab_w_in, ab_gate_bias, ab_w_out, ab_norm_a, ab_norm_b,
               c_w_in, c_conv_w, c_a_log, c_dt_bias, c_norm, c_w_out, moe_w_grp, moe_b_grp, moe_w_exp,
               moe_b_exp, moe_w_gate, moe_w_up, moe_w_down)
    (y_prompt, mlstm_c_prompt, mlstm_n_prompt, mlstm_m_prompt, ret_prompt, gdn_prompt,
     gdn_conv_prompt) = run_trunk(
        x_prompt, jnp.arange(tp),
        jnp.zeros((N_AB, bp, H_A, DK_A, DV_A), f32), jnp.zeros((N_AB, bp, H_A, DK_A), f32),
        jnp.zeros((N_AB, bp, H_A), f32), jnp.zeros((N_AB, bp, H_B, DK_B, DV_B), f32),
        jnp.zeros((N_C, bp, H_C, DK_C, DV_C), f32), jnp.zeros((N_C, bp, CONV_W - 1, C_QKV), x_prompt.dtype),
        *weights)
    (y_sample, mlstm_c_sample, mlstm_n_sample, mlstm_m_sample, ret_sample, gdn_sample,
     gdn_conv_sample) = run_trunk(
        x_sample, PAST_LEN + jnp.arange(ts), state_mlstm_c, state_mlstm_n, state_mlstm_m, state_ret,
        state_gdn, state_gdn_conv, *weights)
    return (y_prompt, y_sample, mlstm_c_prompt, mlstm_n_prompt, mlstm_m_prompt, ret_prompt, gdn_prompt,
            gdn_conv_prompt, mlstm_c_sample, mlstm_n_sample, mlstm_m_sample, ret_sample, gdn_sample,
            gdn_conv_sample)
```

```python
import functools
import math

import jax
import jax.numpy as jnp
from jax import lax
from jax.experimental import pallas as pl
from jax.experimental.pallas import tpu as pltpu
from jax.experimental.pallas import tpu_sc as plsc

F32 = jnp.float32
BF16 = jnp.bfloat16

D_MODEL = 1024
H_A = 4
H_B = 4
H_C = 8
HEAD_DIM = 128
CONV_W = 4
N_GROUPS = 4
EXP_PER_GROUP = 8
N_EXPERTS = N_GROUPS * EXP_PER_GROUP
D_EXPERT = 512
RMS_EPS = 1e-6
ROPE_BASE = 10000.0
PAST_LEN = 2048

LANES = 128
GATE_LANES = LANES
VMEM_LIMIT = 56 * 1024 * 1024

ROW_TILE = 256
SCAN_BLOCK = 512
SCAN_CHUNK = 64
MOE_ROWS = 256
SC_GATHER_ROWS = 64
GDN_HIST_ROWS = 8


def _cparams(sem):
    return pltpu.CompilerParams(dimension_semantics=sem, vmem_limit_bytes=VMEM_LIMIT)


def _dot(a, b):
    return jnp.dot(a.astype(BF16), b.astype(BF16), preferred_element_type=F32)


def _dot_nt(a, b):
    return lax.dot_general(a.astype(BF16), b.astype(BF16), (((1,), (1,)), ((), ())),
                           preferred_element_type=F32)


def _dot_tn(a, b):
    return lax.dot_general(a.astype(BF16), b.astype(BF16), (((0,), (0,)), ((), ())),
                           preferred_element_type=F32)


def _dot_f32(a, b):
    return jnp.dot(a, b, preferred_element_type=F32, precision=lax.Precision.HIGHEST)


def _round_bf16(x):
    return x.astype(BF16).astype(F32)


def _softplus(x):
    return jnp.maximum(x, 0.0) + jnp.log1p(jnp.exp(-jnp.abs(x)))


def _sigmoid(x):
    return 1.0 / (1.0 + jnp.exp(-x))


def _silu(x):
    return x * _sigmoid(x)


def _rms(x, g):
    return x * lax.rsqrt(jnp.mean(x * x, axis=-1, keepdims=True) + RMS_EPS) * g


def _mean_sq_sublane_order(x):
    rows, d = x.shape
    acc = None
    for c in range(d // LANES):
        xc = x[:, c * LANES:(c + 1) * LANES]
        acc = xc * xc if acc is None else acc + xc * xc
    acc_t = acc.T
    s8 = acc_t[0:8, :]
    for a in range(1, LANES // 8):
        s8 = s8 + acc_t[8 * a:8 * a + 8, :]
    ms_row = jnp.sum(s8, axis=0, keepdims=True) * (1.0 / d)
    r = lax.broadcasted_iota(jnp.int32, (rows, rows), 0)
    c = lax.broadcasted_iota(jnp.int32, (rows, rows), 1)
    return jnp.sum(jnp.where(r == c, ms_row, 0.0), axis=1, keepdims=True)


def _rms_rows(x, g, sublane_order):
    ms = jnp.where(sublane_order, _mean_sq_sublane_order(x), jnp.mean(x * x, axis=-1, keepdims=True))
    return x * lax.rsqrt(ms + RMS_EPS) * g


def _lane_pick(tile, idx):
    lane = lax.broadcasted_iota(jnp.int32, tile.shape, 1)
    return jnp.sum(jnp.where(lane == idx, tile, 0.0), axis=-1, keepdims=True)


def _combine(x_ref, ga_ref, gb_ref, rt_ref):
    rt = rt_ref[...]
    return x_ref[...] + (_lane_pick(rt, 4) * ga_ref[...] + _lane_pick(rt, 5) * gb_ref[...])


def _in_proj_kernel(*refs, combine, gate_mode, n_col_chunks, col_chunk, n_first):
    first = pl.program_id(0) < n_first
    if combine:
        x_ref, ga_ref, gb_ref, rt_ref = refs[:4]
        refs = refs[4:]
        x = _combine(x_ref, ga_ref, gb_ref, rt_ref)
    else:
        x_ref = refs[0]
        refs = refs[1:]
        x = x_ref[...]
    g_ref, w_ref, wg_ref, gp_ref = refs[:4]
    outs = refs[4:]
    if combine:
        xo_ref, z_ref, gc_ref = outs
        xo_ref[...] = x
    else:
        z_ref, gc_ref = outs
    xn = _rms_rows(x, g_ref[...], first)
    xh = xn.astype(BF16)
    for c in range(n_col_chunks):
        cs = slice(c * col_chunk, (c + 1) * col_chunk)
        z_ref[:, cs] = jnp.dot(xh, w_ref[:, cs], preferred_element_type=F32)
    raw = jnp.dot(xh, wg_ref[...], preferred_element_type=F32)
    lane = lax.broadcasted_iota(jnp.int32, raw.shape, 1)
    bias = gp_ref[0:1, :]
    if gate_mode == 0:
        val = raw + bias
        gc_ref[...] = jnp.where(lane < H_A, val, -_softplus(-val))
    else:
        neg_a = -jnp.exp(gp_ref[1:2, :])
        gc_ref[...] = jnp.where(lane < H_C, _sigmoid(raw), neg_a * _softplus(raw + bias))


def _in_proj(x, norm_g, w_main, w_gate, gate_params, gate_mode, n_first_rows, comb=None):
    n, d = x.shape
    nz = w_main.shape[1]
    tm = ROW_TILE
    col_chunk = 512
    row = lambda i: (i, 0)
    const = lambda i: (0, 0)
    in_specs = [pl.BlockSpec((tm, d), row)]
    args = [x]
    out_specs = []
    out_shape = []
    if comb is not None:
        ga, gb, rt = comb
        in_specs += [pl.BlockSpec((tm, d), row), pl.BlockSpec((tm, d), row), pl.BlockSpec((tm, LANES), row)]
        args += [ga, gb, rt]
        out_specs.append(pl.BlockSpec((tm, d), row))
        out_shape.append(jax.ShapeDtypeStruct((n, d), F32))
    in_specs += [pl.BlockSpec((1, d), const), pl.BlockSpec((d, nz), const),
                 pl.BlockSpec((d, GATE_LANES), const), pl.BlockSpec((8, GATE_LANES), const)]
    args += [norm_g.reshape(1, d), w_main.astype(BF16), w_gate.astype(BF16), gate_params]
    out_specs += [pl.BlockSpec((tm, nz), row), pl.BlockSpec((tm, GATE_LANES), row)]
    out_shape += [jax.ShapeDtypeStruct((n, nz), F32), jax.ShapeDtypeStruct((n, GATE_LANES), F32)]
    kern = functools.partial(_in_proj_kernel, combine=comb is not None, gate_mode=gate_mode,
                             n_col_chunks=nz // col_chunk, col_chunk=col_chunk, n_first=n_first_rows // tm)
    return pl.pallas_call(
        kern, grid=(n // tm,), in_specs=in_specs, out_specs=out_specs, out_shape=out_shape,
        compiler_params=_cparams(("parallel",)), name=f"in_proj_{gate_mode}")(*args)


def _final_kernel(x_ref, ga_ref, gb_ref, rt_ref, g_ref, y_ref, *, n_first):
    y_ref[...] = _rms_rows(_combine(x_ref, ga_ref, gb_ref, rt_ref), g_ref[...], pl.program_id(0) < n_first)


def _final_norm(x, ga, gb, rt, norm_g, n_first_rows):
    n, d = x.shape
    tm = ROW_TILE
    row = lambda i: (i, 0)
    return pl.pallas_call(
        functools.partial(_final_kernel, n_first=n_first_rows // tm), grid=(n // tm,),
        in_specs=[pl.BlockSpec((tm, d), row), pl.BlockSpec((tm, d), row), pl.BlockSpec((tm, d), row),
                  pl.BlockSpec((tm, LANES), row), pl.BlockSpec((1, d), lambda i: (0, 0))],
        out_specs=pl.BlockSpec((tm, d), row), out_shape=jax.ShapeDtypeStruct((n, d), F32),
        compiler_params=_cparams(("parallel",)), name="final_norm")(x, ga, gb, rt, norm_g.reshape(1, d))


def _out_proj_router_kernel(y_ref, x_ref, w_ref, g_ref, wr_ref, br_ref,
                            x1_ref, h_ref, rt_ref, cnt_ref, carry, *, n_first):
    i = pl.program_id(0)

    @pl.when(i == 0)
    def _():
        carry[...] = jnp.zeros_like(carry)

    x1 = x_ref[...] + _dot(y_ref[...], w_ref[...])
    x1_ref[...] = x1
    hn = _rms_rows(x1, g_ref[...], i < n_first)
    h_ref[...] = hn
    logits = _dot(hn, wr_ref[...]) + br_ref[0:1, :]
    tm = logits.shape[0]
    lane = lax.broadcasted_iota(jnp.int32, logits.shape, 1).astype(F32)
    neg = jnp.float32(-jnp.inf)
    far = jnp.float32(LANES)
    lg = jnp.where(lane < N_GROUPS, logits, neg)
    gmax = jnp.max(lg, axis=-1, keepdims=True)
    grp = jnp.min(jnp.where(lg == gmax, lane, far), axis=-1, keepdims=True)
    p_sel = 1.0 / jnp.sum(jnp.exp(lg - gmax), axis=-1, keepdims=True)
    lo = N_GROUPS + EXP_PER_GROUP * grp
    fm = jnp.where((lane >= lo) & (lane < lo + EXP_PER_GROUP), logits, neg)
    f1 = jnp.max(fm, axis=-1, keepdims=True)
    i1 = jnp.min(jnp.where(fm == f1, lane, far), axis=-1, keepdims=True)
    fm2 = jnp.where(lane == i1, neg, fm)
    f2 = jnp.max(fm2, axis=-1, keepdims=True)
    i2 = jnp.min(jnp.where(fm2 == f2, lane, far), axis=-1, keepdims=True)
    r = jnp.exp(f2 - f1)
    g1 = p_sel / (1.0 + r)
    g2 = p_sel * r / (1.0 + r)
    onehot = jnp.where((lane == i1) | (lane == i2), 1.0, 0.0)
    rr = lax.broadcasted_iota(jnp.int32, (tm, tm), 0)
    cc = lax.broadcasted_iota(jnp.int32, (tm, tm), 1)
    before = jnp.where(rr > cc, 1.0, 0.0).astype(BF16)
    cnt = jnp.dot(before, onehot.astype(BF16), preferred_element_type=F32) + carry[0:1, :]
    rank1 = jnp.sum(jnp.where(lane == i1, cnt, 0.0), axis=-1, keepdims=True)
    rank2 = jnp.sum(jnp.where(lane == i2, cnt, 0.0), axis=-1, keepdims=True)
    new_carry = carry[0:1, :] + jnp.sum(onehot, axis=0, keepdims=True)
    carry[...] = jnp.broadcast_to(new_carry, carry.shape)
    cnt_ref[...] = jnp.broadcast_to(new_carry, cnt_ref.shape)
    e1 = i1 - N_GROUPS
    e2 = i2 - N_GROUPS
    rt = jnp.where(lane == 0, e1, 0.0)
    rt = jnp.where(lane == 1, e2, rt)
    rt = jnp.where(lane == 2, rank1, rt)
    rt = jnp.where(lane == 3, rank2, rt)
    rt = jnp.where(lane == 4, g1, rt)
    rt = jnp.where(lane == 5, g2, rt)
    rt_ref[...] = rt


def _out_proj_router(y, x, w_out, norm_g, w_router, b_router, n_first_rows):
    n, d = x.shape
    tm = ROW_TILE
    row = lambda i: (i, 0)
    const = lambda i: (0, 0)
    return pl.pallas_call(
        functools.partial(_out_proj_router_kernel, n_first=n_first_rows // tm), grid=(n // tm,),
        in_specs=[pl.BlockSpec((tm, d), row), pl.BlockSpec((tm, d), row), pl.BlockSpec((d, d), const),
                  pl.BlockSpec((1, d), const), pl.BlockSpec((d, LANES), const), pl.BlockSpec((8, LANES), const)],
        out_specs=[pl.BlockSpec((tm, d), row), pl.BlockSpec((tm, d), row), pl.BlockSpec((tm, LANES), row),
                   pl.BlockSpec((8, LANES), const)],
        out_shape=[jax.ShapeDtypeStruct((n, d), F32), jax.ShapeDtypeStruct((n, d), F32),
                   jax.ShapeDtypeStruct((n, LANES), F32), jax.ShapeDtypeStruct((8, LANES), F32)],
        scratch_shapes=[pltpu.VMEM((8, LANES), F32)],
        compiler_params=_cparams(("arbitrary",)), name="out_proj_router",
    )(y, x, w_out.astype(BF16), norm_g.reshape(1, d), w_router.astype(BF16), b_router)


def _expert_kernel(be_ref, nv_ref, x_ref, wg_ref, wu_ref, wd_ref, o_ref):
    i = pl.program_id(0)
    nv = nv_ref[i]

    @pl.when(nv > 0)
    def _():
        rows = lax.broadcasted_iota(jnp.int32, (x_ref.shape[0], 1), 0)
        xb = jnp.where(rows < nv, x_ref[...], 0.0).astype(BF16)
        hg = jnp.dot(xb, wg_ref[0], preferred_element_type=F32)
        hu = jnp.dot(xb, wu_ref[0], preferred_element_type=F32)
        hb = (_silu(hg) * hu).astype(BF16)
        o_ref[...] = jnp.dot(hb, wd_ref[0], preferred_element_type=F32)

    @pl.when(nv == 0)
    def _():
        o_ref[...] = jnp.zeros_like(o_ref)


def _experts(rows, block_exp, block_valid, w_gate, w_up, w_down):
    r, d = rows.shape
    bm = MOE_ROWS
    nblk = r // bm
    grid_spec = pltpu.PrefetchScalarGridSpec(
        num_scalar_prefetch=2, grid=(nblk,),
        in_specs=[pl.BlockSpec((bm, d), lambda i, be, nv: (i, 0)),
                  pl.BlockSpec((1, d, D_EXPERT), lambda i, be, nv: (be[i], 0, 0)),
                  pl.BlockSpec((1, d, D_EXPERT), lambda i, be, nv: (be[i], 0, 0)),
                  pl.BlockSpec((1, D_EXPERT, d), lambda i, be, nv: (be[i], 0, 0))],
        out_specs=pl.BlockSpec((bm, d), lambda i, be, nv: (i, 0)))
    return pl.pallas_call(
        _expert_kernel, grid_spec=grid_spec, out_shape=jax.ShapeDtypeStruct((r, d), F32),
        compiler_params=_cparams(("arbitrary",)), name="experts",
    )(block_exp, block_valid, rows, w_gate.astype(BF16), w_up.astype(BF16), w_down.astype(BF16))


def _gather_rows(table, idx):
    info = plsc.get_sparse_core_info()
    n_workers = info.num_cores * info.num_subcores
    n_rows = idx.shape[0]
    width = table.shape[1]
    ch = SC_GATHER_ROWS
    per_worker = n_rows // n_workers
    n_chunks = per_worker // ch
    assert n_workers * n_chunks * ch == n_rows
    mesh = plsc.VectorSubcoreMesh(core_axis_name="c", subcore_axis_name="s")

    @functools.partial(
        pl.kernel, mesh=mesh, out_type=jax.ShapeDtypeStruct((n_rows, width), table.dtype),
        scratch_types=[pltpu.VMEM((ch,), jnp.int32), pltpu.VMEM((ch, width), table.dtype),
                       pltpu.SemaphoreType.DMA])
    def gather(table_hbm, idx_hbm, out_hbm, idx_v, rows_v, sem):
        wid = lax.axis_index("s") * info.num_cores + lax.axis_index("c")
        base = wid * per_worker

        @pl.loop(0, n_chunks)
        def _(j):
            off = pl.multiple_of(base + j * ch, ch)
            pltpu.sync_copy(idx_hbm.at[pl.ds(off, ch)], idx_v)
            pltpu.async_copy(table_hbm.at[idx_v], rows_v, sem).wait()
            pltpu.sync_copy(rows_v, out_hbm.at[pl.ds(off, ch)])

    return gather(table, idx)


def _pad_to(n, m):
    return -(-n // m) * m


def _moe(h, rt, counts_row, w_gate, w_up, w_down):
    n, d = h.shape
    bm = MOE_ROWS
    sc_quant = 32 * SC_GATHER_ROWS
    counts = counts_row[N_GROUPS:N_GROUPS + N_EXPERTS].astype(jnp.int32)
    e = rt[:, 0:2].astype(jnp.int32)
    rank = rt[:, 2:4].astype(jnp.int32)
    padded = ((counts + bm - 1) // bm) * bm
    pend = jnp.cumsum(padded)
    pstart = pend - padded
    start = jnp.cumsum(counts) - counts
    n_rows = _pad_to(_pad_to(2 * n, bm) + N_EXPERTS * bm, sc_quant)
    n_rows = _pad_to(n_rows, bm)
    nblk = n_rows // bm
    blk_start = jnp.arange(nblk, dtype=jnp.int32) * bm
    block_exp = jnp.minimum(jnp.searchsorted(pend, blk_start, side="right"), N_EXPERTS - 1).astype(jnp.int32)
    block_valid = jnp.clip(counts[block_exp] - (blk_start - pstart[block_exp]), 0, bm).astype(jnp.int32)
    block_valid = jnp.where(blk_start < pend[-1], block_valid, 0)
    order = jnp.argsort(e.reshape(-1), stable=True).astype(jnp.int32)
    tok_sorted = order // 2
    r_idx = jnp.arange(n_rows, dtype=jnp.int32)
    r_exp = jnp.repeat(block_exp, bm)
    j = r_idx - pstart[r_exp]
    src = jnp.where(j < counts[r_exp], tok_sorted[jnp.clip(start[r_exp] + j, 0, 2 * n - 1)], 0)
    rows = _gather_rows(h, src)
    out_rows = _experts(rows, block_exp, block_valid, w_gate, w_up, w_down)
    dest = pstart[e] + rank
    n_pad = _pad_to(n, sc_quant)
    dest = jnp.pad(dest, ((0, n_pad - n), (0, 0)))
    ga = _gather_rows(out_rows, dest[:, 0])
    gb = _gather_rows(out_rows, dest[:, 1])
    return ga, gb


def _chunk_masks(L):
    r = lax.broadcasted_iota(jnp.int32, (L, L), 0)
    c = lax.broadcasted_iota(jnp.int32, (L, L), 1)
    return r, c


def _row_from_col(col, r, c):
    return jnp.sum(jnp.where(r == c, col, 0.0), axis=0, keepdims=True)


def _cumsum_col_row(col, r, c):
    row = _row_from_col(col, r, c)
    cs_col = jnp.sum(jnp.where(r >= c, row, 0.0), axis=1, keepdims=True)
    cs_row = jnp.sum(jnp.where(r <= c, col, 0.0), axis=0, keepdims=True)
    return cs_col, cs_row


def _mlstm_kernel(q_ref, k_ref, v_ref, o_ref, gc_ref, c0_ref, n0_ref, m0_ref, na_ref,
                  y_ref, c1_ref, n1_ref, m1_ref, c_s, n_s, m_s, *, L, n_chunks):
    h = pl.program_id(1)
    t = pl.program_id(2)

    @pl.when(t == 0)
    def _():
        c_s[...] = c0_ref[0, 0]
        n_s[...] = n0_ref[0, 0]
        m_s[...] = m0_ref[0, 0]

    r, c = _chunk_masks(L)
    causal = r >= c
    na = na_ref[...]

    def chunk(ci, carry):
        sl = pl.ds(pl.multiple_of(ci * L, L), L)
        q = q_ref[sl, :]
        k = k_ref[sl, :] * HEAD_DIM ** -0.5
        v = v_ref[sl, :]
        gates = gc_ref[sl, :]
        ig_col = _lane_pick(gates, h)
        lf_col = _lane_pick(gates, H_A + h)
        ig_row = _row_from_col(ig_col, r, c)
        b_col, b_row = _cumsum_col_row(lf_col, r, c)
        cmat = c_s[...]
        nvec = n_s[...]
        m_prev = m_s[0:1, 0:1]
        dmat = jnp.where(causal, b_col - b_row + ig_row, -jnp.inf)
        inter = b_col + m_prev
        m_t = jnp.maximum(inter, jnp.max(dmat, axis=-1, keepdims=True))
        s = _dot_nt(q, k) * jnp.exp(dmat - m_t)
        w_inter = jnp.exp(inter - m_t)
        num = w_inter * _dot(q, cmat) + _dot(s, v)
        qn = jnp.sum(_round_bf16(q) * _round_bf16(nvec), axis=-1, keepdims=True)
        den = w_inter * qn + jnp.sum(s, axis=-1, keepdims=True)
        hout = num / jnp.maximum(jnp.abs(den), jnp.exp(-m_t))
        m_new = m_t[L - 1:L, :]
        b_last = b_col[L - 1:L, :]
        wk = jnp.exp(b_last - b_col + ig_col - m_new)
        dec = jnp.exp(b_last + m_prev - m_new)
        kw = k * wk
        c_s[...] = dec * cmat + _dot_tn(kw, v)
        n_s[...] = dec * nvec + jnp.sum(_round_bf16(wk) * _round_bf16(k), axis=0, keepdims=True)
        m_s[...] = jnp.broadcast_to(m_new, m_s.shape)
        y_ref[sl, :] = _rms(hout, na) * _sigmoid(o_ref[sl, :])
        return carry

    lax.fori_loop(0, n_chunks, chunk, 0)

    @pl.when(t == pl.num_programs(2) - 1)
    def _():
        c1_ref[0, 0] = c_s[...]
        n1_ref[0, 0] = n_s[...]
        m1_ref[0, 0] = m_s[...]


def _retention_kernel(q_ref, k_ref, v_ref, g_ref, cos_ref, sin_ref, s0_ref, nb_ref, lg_ref,
                      y_ref, s1_ref, s_s, *, L, n_chunks):
    t = pl.program_id(2)

    @pl.when(t == 0)
    def _():
        s_s[...] = s0_ref[0, 0]

    r, c = _chunk_masks(L)
    causal = r >= c
    lg = lg_ref[0, 0:1, 0:1]
    rel = (r - c).astype(F32)
    dmat = jnp.where(causal, jnp.exp(jnp.where(causal, rel, 0.0) * lg), 0.0)
    idx = lax.broadcasted_iota(jnp.int32, (L, 1), 0).astype(F32)
    q_dec = jnp.exp((idx + 1.0) * lg)
    k_dec = jnp.exp((L - 1.0 - idx) * lg)
    c_dec = jnp.exp(L * lg)
    nb = nb_ref[...]
    half = HEAD_DIM // 2

    def rot(x, cos, sin):
        return x * cos + pltpu.roll(x, half, 1) * sin

    def chunk(ci, carry):
        sl = pl.ds(pl.multiple_of(ci * L, L), L)
        cos = cos_ref[sl, :]
        sin = sin_ref[sl, :]
        q = rot(q_ref[sl, :], cos, sin)
        k = rot(k_ref[sl, :], cos, sin) * HEAD_DIM ** -0.5
        v = v_ref[sl, :]
        smat = s_s[...]
        att = _dot_nt(q, k) * dmat
        o = _dot(att, v) + q_dec * _dot(q, smat)
        s_s[...] = c_dec * smat + _dot_tn(k * k_dec, v)
        y_ref[sl, :] = _rms(o, nb) * _silu(g_ref[sl, :])
        return carry

    lax.fori_loop(0, n_chunks, chunk, 0)

    @pl.when(t == pl.num_programs(2) - 1)
    def _():
        s1_ref[0, 0] = s_s[...]


def _unit_lower_inverse(a, r, c):
    L = a.shape[0]
    eye = jnp.where(r == c, 1.0, 0.0)
    shift = 4
    same = (r >> shift) == (c >> shift)
    p = -jnp.where(same, a, 0.0)
    inv = eye + p
    for _ in range(3):
        p = _dot_f32(p, p)
        inv = inv + _dot_f32(inv, p)
    while (1 << shift) < L:
        shift += 1
        same2 = (r >> shift) == (c >> shift)
        off = jnp.where(same2 & jnp.logical_not(same), a, 0.0)
        inv = inv - _dot_f32(inv, _dot_f32(off, inv))
        same = same2
    return inv


def _gdn_kernel(q_ref, k_ref, v_ref, gt_ref, gc_ref, cwq_ref, cwk_ref, cwv_ref, hq_ref, hk_ref, hv_ref,
                s0_ref, nc_ref, y_ref, s1_ref, s_s, eq_s, ek_s, ev_s, qn_s, kn_s, vn_s, *, L, n_chunks):
    h = pl.program_id(1)
    t = pl.program_id(2)
    tb = q_ref.shape[0]
    hist = GDN_HIST_ROWS

    @pl.when(t == 0)
    def _():
        s_s[...] = s0_ref[0, 0]
        eq_s[0:hist, :] = _round_bf16(hq_ref[0])
        ek_s[0:hist, :] = _round_bf16(hk_ref[0])
        ev_s[0:hist, :] = _round_bf16(hv_ref[0])

    def conv_silu(x_ref, e_s, cw_ref):
        e_s[hist:hist + tb, :] = _round_bf16(x_ref[...])
        cw = _round_bf16(cw_ref[...])
        acc = e_s[hist - (CONV_W - 1):hist - (CONV_W - 1) + tb, :] * cw[0:1, :]
        for tap in range(1, CONV_W):
            lo = hist - (CONV_W - 1) + tap
            acc = acc + e_s[lo:lo + tb, :] * cw[tap:tap + 1, :]
        e_s[0:hist, :] = e_s[tb:tb + hist, :]
        return _silu(acc)

    def l2n(x):
        return x * lax.rsqrt(jnp.sum(x * x, axis=-1, keepdims=True) + RMS_EPS)

    qn_s[...] = l2n(conv_silu(q_ref, eq_s, cwq_ref)) * HEAD_DIM ** -0.5
    kn_s[...] = l2n(conv_silu(k_ref, ek_s, cwk_ref))
    vn_s[...] = conv_silu(v_ref, ev_s, cwv_ref)

    r, c = _chunk_masks(L)
    lower = r >= c
    strict = r > c
    ncw = nc_ref[...]

    def chunk(ci, carry):
        sl = pl.ds(pl.multiple_of(ci * L, L), L)
        q = qn_s[sl, :]
        k = kn_s[sl, :]
        v = vn_s[sl, :]
        gates = gc_ref[sl, :]
        beta = _lane_pick(gates, h)
        g_col = _lane_pick(gates, H_C + h)
        G_col, G_row = _cumsum_col_row(g_col, r, c)
        gam = jnp.where(lower, jnp.exp(jnp.where(lower, G_col - G_row, 0.0)), 0.0)
        kq = _dot_nt(jnp.concatenate([k, q], axis=0), k)
        a = jnp.where(strict, beta * kq[0:L, :] * gam, 0.0)
        tinv = _unit_lower_inverse(a, r, c)
        eg = jnp.exp(G_col)
        rhs = jnp.concatenate([v * beta, k * (beta * eg)], axis=1)
        uw = _dot_f32(tinv, rhs)
        smat = s_s[...]
        u = uw[:, 0:HEAD_DIM]
        w = uw[:, HEAD_DIM:2 * HEAD_DIM]
        v_new = u - _dot(w, smat)
        o = _dot(q * eg, smat) + _dot(kq[L:2 * L, :] * gam, v_new)
        g_last = G_col[L - 1:L, :]
        s_s[...] = jnp.exp(g_last) * smat + _dot_tn(k * jnp.exp(g_last - G_col), v_new)
        y_ref[sl, :] = _rms(o, ncw) * _silu(gt_ref[sl, :])
        return carry

    lax.fori_loop(0, n_chunks, chunk, 0)

    @pl.when(t == pl.num_programs(2) - 1)
    def _():
        s1_ref[0, 0] = s_s[...]


def _scan_geometry(t_len):
    tb = min(SCAN_BLOCK, t_len)
    L = min(SCAN_CHUNK, t_len)
    return tb, L, t_len // tb, tb // L


def _col_spec(tb, nt, row0_blocks, col_fn):
    return pl.BlockSpec((tb, HEAD_DIM), lambda b, h, t: (row0_blocks + b * nt + t, col_fn(h)))


def _state_spec(shape_tail):
    nd = len(shape_tail)
    return pl.BlockSpec((1, 1) + shape_tail, lambda b, h, t: (b, h) + (0,) * nd)


def _mlstm_scan(z, gc, y_prev, row0, bsz, t_len, c0, n0, m0, norm_a):
    tb, L, nt, n_chunks = _scan_geometry(t_len)
    rb = row0 // tb
    hb = HEAD_DIM
    in_specs = [_col_spec(tb, nt, rb, lambda h: h), _col_spec(tb, nt, rb, lambda h: H_A + h),
                _col_spec(tb, nt, rb, lambda h: 2 * H_A + h), _col_spec(tb, nt, rb, lambda h: 3 * H_A + h),
                pl.BlockSpec((tb, GATE_LANES), lambda b, h, t: (rb + b * nt + t, 0)),
                _state_spec((hb, hb)), _state_spec((1, hb)), _state_spec((1, hb)),
                pl.BlockSpec((1, hb), lambda b, h, t: (0, 0))]
    out_specs = [_col_spec(tb, nt, rb, lambda h: h),
                 _state_spec((hb, hb)), _state_spec((1, hb)), _state_spec((1, hb))]
    out_shape = [jax.ShapeDtypeStruct(y_prev.shape, F32),
                 jax.ShapeDtypeStruct((bsz, H_A, hb, hb), F32), jax.ShapeDtypeStruct((bsz, H_A, 1, hb), F32),
                 jax.ShapeDtypeStruct((bsz, H_A, 1, hb), F32)]
    m0b = jnp.broadcast_to(m0[:, :, None, None], (bsz, H_A, 1, hb))
    kern = functools.partial(_mlstm_kernel_aliased, L=L, n_chunks=n_chunks)
    y, c1, n1, m1 = pl.pallas_call(
        kern, grid=(bsz, H_A, nt), in_specs=in_specs + [pl.BlockSpec(memory_space=pl.ANY)],
        out_specs=out_specs, out_shape=out_shape,
        scratch_shapes=[pltpu.VMEM((hb, hb), F32), pltpu.VMEM((1, hb), F32), pltpu.VMEM((1, hb), F32)],
        input_output_aliases={9: 0},
        compiler_params=_cparams(("parallel", "parallel", "arbitrary")), name="mlstm_scan",
    )(z, z, z, z, gc, c0, n0[:, :, None, :], m0b, norm_a.reshape(1, hb), y_prev)
    return y, c1, n1[:, :, 0, :], m1[:, :, 0, 0]


def _mlstm_kernel_aliased(*refs, **kw):
    return _mlstm_kernel(*refs[:9], *refs[10:], **kw)


def _retention_scan(z, cos, sin, y_prev, row0, bsz, t_len, s0, norm_b):
    tb, L, nt, n_chunks = _scan_geometry(t_len)
    rb = row0 // tb
    hb = HEAD_DIM
    base = 4 * H_A
    in_specs = [_col_spec(tb, nt, rb, lambda h: base + h), _col_spec(tb, nt, rb, lambda h: base + H_B + h),
                _col_spec(tb, nt, rb, lambda h: base + 2 * H_B + h),
                _col_spec(tb, nt, rb, lambda h: base + 3 * H_B + h),
                pl.BlockSpec((tb, hb), lambda b, h, t: (t, 0)), pl.BlockSpec((tb, hb), lambda b, h, t: (t, 0)),
                _state_spec((hb, hb)), pl.BlockSpec((1, hb), lambda b, h, t: (0, 0)),
                pl.BlockSpec((1, 1, hb), lambda b, h, t: (h, 0, 0)),
                pl.BlockSpec(memory_space=pl.ANY)]
    out_specs = [_col_spec(tb, nt, rb, lambda h: H_A + h), _state_spec((hb, hb))]
    out_shape = [jax.ShapeDtypeStruct(y_prev.shape, F32), jax.ShapeDtypeStruct((bsz, H_B, hb, hb), F32)]
    log_gamma = jnp.log(1.0 - 2.0 ** (-5.0 - jnp.arange(H_B, dtype=F32)))
    lg_tab = jnp.broadcast_to(log_gamma[:, None, None], (H_B, 1, hb))
    kern = functools.partial(_retention_kernel_aliased, L=L, n_chunks=n_chunks)
    return pl.pallas_call(
        kern, grid=(bsz, H_B, nt), in_specs=in_specs, out_specs=out_specs, out_shape=out_shape,
        scratch_shapes=[pltpu.VMEM((hb, hb), F32)], input_output_aliases={9: 0},
        compiler_params=_cparams(("parallel", "parallel", "arbitrary")), name="retention_scan",
    )(z, z, z, z, cos, sin, s0, norm_b.reshape(1, hb), lg_tab, y_prev)


def _retention_kernel_aliased(*refs, **kw):
    return _retention_kernel(*refs[:9], *refs[10:], **kw)


def _gdn_scan(z, gc, y_prev, row0, bsz, t_len, conv_w, conv_hist, s0, norm_c):
    tb, L, nt, n_chunks = _scan_geometry(t_len)
    rb = row0 // tb
    hb = HEAD_DIM
    cw_spec = lambda off: pl.BlockSpec((CONV_W, hb), lambda b, h, t: (0, off + h))
    hist_spec = lambda off: pl.BlockSpec((1, GDN_HIST_ROWS, hb), lambda b, h, t: (b, 0, off + h))
    in_specs = [_col_spec(tb, nt, rb, lambda h: h), _col_spec(tb, nt, rb, lambda h: H_C + h),
                _col_spec(tb, nt, rb, lambda h: 2 * H_C + h), _col_spec(tb, nt, rb, lambda h: 3 * H_C + h),
                pl.BlockSpec((tb, GATE_LANES), lambda b, h, t: (rb + b * nt + t, 0)),
                cw_spec(0), cw_spec(H_C), cw_spec(2 * H_C),
                hist_spec(0), hist_spec(H_C), hist_spec(2 * H_C),
                _state_spec((hb, hb)), pl.BlockSpec((1, hb), lambda b, h, t: (0, 0)),
                pl.BlockSpec(memory_space=pl.ANY)]
    out_specs = [_col_spec(tb, nt, rb, lambda h: h), _state_spec((hb, hb))]
    out_shape = [jax.ShapeDtypeStruct(y_prev.shape, F32), jax.ShapeDtypeStruct((bsz, H_C, hb, hb), F32)]
    kern = functools.partial(_gdn_kernel_aliased, L=L, n_chunks=n_chunks)
    ext = pltpu.VMEM((tb + GDN_HIST_ROWS, hb), F32)
    blk = pltpu.VMEM((tb, hb), F32)
    return pl.pallas_call(
        kern, grid=(bsz, H_C, nt), in_specs=in_specs, out_specs=out_specs, out_shape=out_shape,
        scratch_shapes=[pltpu.VMEM((hb, hb), F32), ext, ext, ext, blk, blk, blk],
        input_output_aliases={13: 0},
        compiler_params=_cparams(("parallel", "parallel", "arbitrary")), name="gdn_scan",
    )(z, z, z, z, gc, conv_w, conv_w, conv_w, conv_hist, conv_hist, conv_hist, s0, norm_c.reshape(1, hb), y_prev)


def _gdn_kernel_aliased(*refs, **kw):
    return _gdn_kernel(*refs[:13], *refs[14:], **kw)


def _rope_tables(pos):
    half = HEAD_DIM // 2
    inv = ROPE_BASE ** (-jnp.arange(half, dtype=F32) / half)
    ang = pos.astype(F32)[:, None] * inv[None, :]
    cos, sin = jnp.cos(ang), jnp.sin(ang)
    return jnp.concatenate([cos, cos], axis=-1), jnp.concatenate([-sin, sin], axis=-1)


def _gate_rows(*rows):
    out = jnp.zeros((8, GATE_LANES), F32)
    for i, rvals in enumerate(rows):
        out = out.at[i, :rvals.shape[0]].set(rvals.astype(F32))
    return out


def _pad_cols(w, width):
    return jnp.pad(w, ((0, 0), (0, width - w.shape[1])))


def kernel(x_prompt, x_sample, state_mlstm_c, state_mlstm_n, state_mlstm_m, state_ret, state_gdn, state_gdn_conv, norm_mix, norm_ffn, norm_final, ab_w_in, ab_gate_bias, ab_w_out, ab_norm_a, ab_norm_b, c_w_in, c_conv_w, c_a_log, c_dt_bias, c_norm, c_w_out, moe_w_grp, moe_b_grp, moe_w_exp, moe_b_exp, moe_w_gate, moe_w_up, moe_w_down):
    bp, tp, d = x_prompt.shape
    bs, ts, _ = x_sample.shape
    n_p, n_s = bp * tp, bs * ts
    n = n_p + n_s
    x = jnp.concatenate([x_prompt.reshape(n_p, d), x_sample.reshape(n_s, d)], axis=0)
    hd = HEAD_DIM
    qkv_a = 4 * H_A * hd
    n_gate_a = 2 * H_A
    c_qkv = 3 * H_C * hd
    n_gate_c = 2 * H_C

    def moe_layer(y, xres, w_out, layer):
        w_router = _pad_cols(jnp.concatenate([moe_w_grp[layer], moe_w_exp[layer]], axis=1), LANES)
        b_router = _gate_rows(jnp.concatenate([moe_b_grp[layer], moe_b_exp[layer]]))
        x1, hmoe, rt, cnt = _out_proj_router(y, xres, w_out, norm_ffn[layer], w_router, b_router, n_p)
        ga, gb = _moe(hmoe, rt, cnt[0], moe_w_gate[layer], moe_w_up[layer], moe_w_down[layer])
        return x1, ga, gb, rt

    w_in = ab_w_in[0]
    w_main = jnp.concatenate([w_in[:, :qkv_a], w_in[:, qkv_a + n_gate_a:]], axis=1)
    w_gate = _pad_cols(w_in[:, qkv_a:qkv_a + n_gate_a], GATE_LANES)
    z, gc = _in_proj(x, norm_mix[0], w_main, w_gate, _gate_rows(ab_gate_bias[0]), 0, n_p)
    y = jnp.zeros((n, d), F32)
    zeros = lambda *s: jnp.zeros(s, F32)
    cos_p, sin_p = _rope_tables(jnp.arange(tp))
    cos_s, sin_s = _rope_tables(PAST_LEN + jnp.arange(ts))
    y, c_p, n_pm, m_p = _mlstm_scan(z, gc, y, 0, bp, tp, zeros(bp, H_A, hd, hd), zeros(bp, H_A, hd),
                                    zeros(bp, H_A), ab_norm_a[0])
    y, c_s, n_sm, m_s = _mlstm_scan(z, gc, y, n_p, bs, ts, state_mlstm_c[0], state_mlstm_n[0],
                                    state_mlstm_m[0], ab_norm_a[0])
    y, r_p = _retention_scan(z, cos_p, sin_p, y, 0, bp, tp, zeros(bp, H_B, hd, hd), ab_norm_b[0])
    y, r_s = _retention_scan(z, cos_s, sin_s, y, n_p, bs, ts, state_ret[0], ab_norm_b[0])
    x1, ga, gb, rt = moe_layer(y, x, ab_w_out[0], 0)

    w_in = c_w_in[0]
    w_main = jnp.concatenate([w_in[:, :c_qkv], w_in[:, c_qkv + n_gate_c:]], axis=1)
    w_gate = _pad_cols(w_in[:, c_qkv:c_qkv + n_gate_c], GATE_LANES)
    zero8 = jnp.zeros((H_C,), F32)
    gparams = _gate_rows(jnp.concatenate([zero8, c_dt_bias[0]]), jnp.concatenate([zero8, c_a_log[0]]))
    x2, z, gc = _in_proj(x1, norm_mix[1], w_main, w_gate, gparams, 1, n_p, comb=(ga, gb, rt))
    hist_p = jnp.zeros((bp, GDN_HIST_ROWS, c_qkv), F32)
    hist_s = jnp.pad(state_gdn_conv[0], ((0, 0), (GDN_HIST_ROWS - (CONV_W - 1), 0), (0, 0)))
    y = jnp.zeros((n, d), F32)
    y, g_p = _gdn_scan(z, gc, y, 0, bp, tp, c_conv_w[0], hist_p, zeros(bp, H_C, hd, hd), c_norm[0])
    y, g_s = _gdn_scan(z, gc, y, n_p, bs, ts, c_conv_w[0], hist_s, state_gdn[0], c_norm[0])
    qkv_p = z[:n_p, :c_qkv].reshape(bp, tp, c_qkv)
    qkv_s = z[n_p:, :c_qkv].reshape(bs, ts, c_qkv)
    conv_p = qkv_p[:, tp - (CONV_W - 1):, :]
    conv_s = qkv_s[:, ts - (CONV_W - 1):, :]
    x3, ga, gb, rt = moe_layer(y, x2, c_w_out[0], 1)

    yf = _final_norm(x3, ga, gb, rt, norm_final, n_p)
    y_prompt = yf[:n_p].reshape(bp, tp, d)
    y_sample = yf[n_p:].reshape(bs, ts, d)
    return (y_prompt, y_sample, c_p[None], n_pm[None], m_p[None], r_p[None], g_p[None], conv_p[None],
            c_s[None], n_sm[None], m_s[None], r_s[None], g_s[None], conv_s[None])
```

```python
import functools
import math

import jax
import jax.numpy as jnp
from jax import lax
from jax.experimental import pallas as pl
from jax.experimental.pallas import tpu as pltpu
from jax.experimental.pallas import tpu_sc as plsc

F32 = jnp.float32
BF16 = jnp.bfloat16

D_MODEL = 1024
H_A = 4
H_B = 4
H_C = 8
HEAD_DIM = 128
CONV_W = 4
N_GROUPS = 4
EXP_PER_GROUP = 8
N_EXPERTS = N_GROUPS * EXP_PER_GROUP
D_EXPERT = 512
RMS_EPS = 1e-6
ROPE_BASE = 10000.0
PAST_LEN = 2048

LANES = 128
GATE_LANES = LANES
VMEM_LIMIT = 56 * 1024 * 1024

ROW_TILE = 256
SCAN_BLOCK = 512
SCAN_CHUNK = 64
SCAN_HEADS = 4
MOE_ROWS = 256
SC_GATHER_ROWS = 64
GDN_HIST_ROWS = 8


def _cparams(sem):
    return pltpu.CompilerParams(dimension_semantics=sem, vmem_limit_bytes=VMEM_LIMIT)


def _dot(a, b):
    return jnp.dot(a.astype(BF16), b.astype(BF16), preferred_element_type=F32)


def _dot_nt(a, b):
    return lax.dot_general(a.astype(BF16), b.astype(BF16), (((1,), (1,)), ((), ())),
                           preferred_element_type=F32)


def _dot_tn(a, b):
    return lax.dot_general(a.astype(BF16), b.astype(BF16), (((0,), (0,)), ((), ())),
                           preferred_element_type=F32)


def _split2(x):
    hi = x.astype(BF16)
    return hi, (x - hi.astype(F32)).astype(BF16)


def _dot_x3(a, b):
    return (jnp.dot(a[0], b[0], preferred_element_type=F32)
            + (jnp.dot(a[0], b[1], preferred_element_type=F32) + jnp.dot(a[1], b[0], preferred_element_type=F32)))


def _round_bf16(x):
    return x.astype(BF16).astype(F32)


def _softplus(x):
    return jnp.maximum(x, 0.0) + jnp.log1p(jnp.exp(-jnp.abs(x)))


def _sigmoid(x):
    return 1.0 / (1.0 + jnp.exp(-x))


def _silu(x):
    return x * _sigmoid(x)


def _rms(x, g):
    return x * lax.rsqrt(jnp.mean(x * x, axis=-1, keepdims=True) + RMS_EPS) * g


def _mean_sq_sublane_order(x):
    rows, d = x.shape
    acc = None
    for c in range(d // LANES):
        xc = x[:, c * LANES:(c + 1) * LANES]
        acc = xc * xc if acc is None else acc + xc * xc
    acc_t = acc.T
    s8 = acc_t[0:8, :]
    for a in range(1, LANES // 8):
        s8 = s8 + acc_t[8 * a:8 * a + 8, :]
    ms_row = jnp.sum(s8, axis=0, keepdims=True) * (1.0 / d)
    r = lax.broadcasted_iota(jnp.int32, (rows, rows), 0)
    c = lax.broadcasted_iota(jnp.int32, (rows, rows), 1)
    return jnp.sum(jnp.where(r == c, ms_row, 0.0), axis=1, keepdims=True)


def _rms_rows(x, g, sublane_order):
    ms = jnp.where(sublane_order, _mean_sq_sublane_order(x), jnp.mean(x * x, axis=-1, keepdims=True))
    return x * lax.rsqrt(ms + RMS_EPS) * g


def _lane_pick(tile, idx):
    lane = lax.broadcasted_iota(jnp.int32, tile.shape, 1)
    return jnp.sum(jnp.where(lane == idx, tile, 0.0), axis=-1, keepdims=True)


def _combine(x_ref, ga_ref, gb_ref, rt_ref):
    rt = rt_ref[...]
    return x_ref[...] + (_lane_pick(rt, 4) * ga_ref[...] + _lane_pick(rt, 5) * gb_ref[...])


def _in_proj_kernel(*refs, combine, gate_mode, n_col_chunks, col_chunk, n_first):
    first = pl.program_id(0) < n_first
    if combine:
        x_ref, ga_ref, gb_ref, rt_ref = refs[:4]
        refs = refs[4:]
        x = _combine(x_ref, ga_ref, gb_ref, rt_ref)
    else:
        x_ref = refs[0]
        refs = refs[1:]
        x = x_ref[...]
    g_ref, w_ref, wg_ref, gp_ref = refs[:4]
    outs = refs[4:]
    if combine:
        xo_ref, z_ref, gc_ref = outs
        xo_ref[...] = x
    else:
        z_ref, gc_ref = outs
    xn = _rms_rows(x, g_ref[...], first)
    xh = xn.astype(BF16)
    for c in range(n_col_chunks):
        cs = slice(c * col_chunk, (c + 1) * col_chunk)
        z_ref[:, cs] = jnp.dot(xh, w_ref[:, cs], preferred_element_type=F32)
    raw = jnp.dot(xh, wg_ref[...], preferred_element_type=F32)
    lane = lax.broadcasted_iota(jnp.int32, raw.shape, 1)
    bias = gp_ref[0:1, :]
    if gate_mode == 0:
        val = raw + bias
        gc_ref[...] = jnp.where(lane < H_A, val, -_softplus(-val))
    else:
        neg_a = -jnp.exp(gp_ref[1:2, :])
        gc_ref[...] = jnp.where(lane < H_C, _sigmoid(raw), neg_a * _softplus(raw + bias))


def _in_proj(x, norm_g, w_main, w_gate, gate_params, gate_mode, n_first_rows, comb=None):
    n, d = x.shape
    nz = w_main.shape[1]
    tm = ROW_TILE
    col_chunk = 512
    row = lambda i: (i, 0)
    const = lambda i: (0, 0)
    in_specs = [pl.BlockSpec((tm, d), row)]
    args = [x]
    out_specs = []
    out_shape = []
    if comb is not None:
        ga, gb, rt = comb
        in_specs += [pl.BlockSpec((tm, d), row), pl.BlockSpec((tm, d), row), pl.BlockSpec((tm, LANES), row)]
        args += [ga, gb, rt]
        out_specs.append(pl.BlockSpec((tm, d), row))
        out_shape.append(jax.ShapeDtypeStruct((n, d), F32))
    in_specs += [pl.BlockSpec((1, d), const), pl.BlockSpec((d, nz), const),
                 pl.BlockSpec((d, GATE_LANES), const), pl.BlockSpec((8, GATE_LANES), const)]
    args += [norm_g.reshape(1, d), w_main.astype(BF16), w_gate.astype(BF16), gate_params]
    out_specs += [pl.BlockSpec((tm, nz), row), pl.BlockSpec((tm, GATE_LANES), row)]
    out_shape += [jax.ShapeDtypeStruct((n, nz), F32), jax.ShapeDtypeStruct((n, GATE_LANES), F32)]
    kern = functools.partial(_in_proj_kernel, combine=comb is not None, gate_mode=gate_mode,
                             n_col_chunks=nz // col_chunk, col_chunk=col_chunk, n_first=n_first_rows // tm)
    return pl.pallas_call(
        kern, grid=(n // tm,), in_specs=in_specs, out_specs=out_specs, out_shape=out_shape,
        compiler_params=_cparams(("parallel",)), name=f"in_proj_{gate_mode}")(*args)


def _final_kernel(x_ref, ga_ref, gb_ref, rt_ref, g_ref, y1_ref, y2_ref, *, n_first):
    i = pl.program_id(0)
    y = _rms_rows(_combine(x_ref, ga_ref, gb_ref, rt_ref), g_ref[...], i < n_first)

    @pl.when(i < n_first)
    def _():
        y1_ref[...] = y

    @pl.when(i >= n_first)
    def _():
        y2_ref[...] = y


def _final_norm(x, ga, gb, rt, norm_g, n_first_rows):
    n, d = x.shape
    tm = ROW_TILE
    n_first = n_first_rows // tm
    row = lambda i: (i, 0)
    return pl.pallas_call(
        functools.partial(_final_kernel, n_first=n_first), grid=(n // tm,),
        in_specs=[pl.BlockSpec((tm, d), row), pl.BlockSpec((tm, d), row), pl.BlockSpec((tm, d), row),
                  pl.BlockSpec((tm, LANES), row), pl.BlockSpec((1, d), lambda i: (0, 0))],
        out_specs=[pl.BlockSpec((tm, d), lambda i: (jnp.minimum(i, n_first - 1), 0)),
                   pl.BlockSpec((tm, d), lambda i: (jnp.maximum(i - n_first, 0), 0))],
        out_shape=[jax.ShapeDtypeStruct((n_first_rows, d), F32), jax.ShapeDtypeStruct((n - n_first_rows, d), F32)],
        compiler_params=_cparams(("arbitrary",)), name="final_norm")(x, ga, gb, rt, norm_g.reshape(1, d))


def _out_proj_router_kernel(y_ref, x_ref, w_ref, g_ref, wr_ref, br_ref,
                            x1_ref, h_ref, rt_ref, cnt_ref, carry, *, n_first):
    i = pl.program_id(0)

    @pl.when(i == 0)
    def _():
        carry[...] = jnp.zeros_like(carry)

    x1 = x_ref[...] + _dot(y_ref[...], w_ref[...])
    x1_ref[...] = x1
    hn = _rms_rows(x1, g_ref[...], i < n_first)
    h_ref[...] = hn
    logits = _dot(hn, wr_ref[...]) + br_ref[0:1, :]
    tm = logits.shape[0]
    lane = lax.broadcasted_iota(jnp.int32, logits.shape, 1).astype(F32)
    neg = jnp.float32(-jnp.inf)
    far = jnp.float32(LANES)
    lg = jnp.where(lane < N_GROUPS, logits, neg)
    gmax = jnp.max(lg, axis=-1, keepdims=True)
    grp = jnp.min(jnp.where(lg == gmax, lane, far), axis=-1, keepdims=True)
    p_sel = 1.0 / jnp.sum(jnp.exp(lg - gmax), axis=-1, keepdims=True)
    lo = N_GROUPS + EXP_PER_GROUP * grp
    fm = jnp.where((lane >= lo) & (lane < lo + EXP_PER_GROUP), logits, neg)
    f1 = jnp.max(fm, axis=-1, keepdims=True)
    i1 = jnp.min(jnp.where(fm == f1, lane, far), axis=-1, keepdims=True)
    fm2 = jnp.where(lane == i1, neg, fm)
    f2 = jnp.max(fm2, axis=-1, keepdims=True)
    i2 = jnp.min(jnp.where(fm2 == f2, lane, far), axis=-1, keepdims=True)
    r = jnp.exp(f2 - f1)
    g1 = p_sel / (1.0 + r)
    g2 = p_sel * r / (1.0 + r)
    onehot = jnp.where((lane == i1) | (lane == i2), 1.0, 0.0)
    rr = lax.broadcasted_iota(jnp.int32, (tm, tm), 0)
    cc = lax.broadcasted_iota(jnp.int32, (tm, tm), 1)
    before = jnp.where(rr > cc, 1.0, 0.0).astype(BF16)
    cnt = jnp.dot(before, onehot.astype(BF16), preferred_element_type=F32) + carry[0:1, :]
    rank1 = jnp.sum(jnp.where(lane == i1, cnt, 0.0), axis=-1, keepdims=True)
    rank2 = jnp.sum(jnp.where(lane == i2, cnt, 0.0), axis=-1, keepdims=True)
    new_carry = carry[0:1, :] + jnp.sum(onehot, axis=0, keepdims=True)
    carry[...] = jnp.broadcast_to(new_carry, carry.shape)
    cnt_ref[...] = jnp.broadcast_to(new_carry, cnt_ref.shape)
    e1 = i1 - N_GROUPS
    e2 = i2 - N_GROUPS
    rt = jnp.where(lane == 0, e1, 0.0)
    rt = jnp.where(lane == 1, e2, rt)
    rt = jnp.where(lane == 2, rank1, rt)
    rt = jnp.where(lane == 3, rank2, rt)
    rt = jnp.where(lane == 4, g1, rt)
    rt = jnp.where(lane == 5, g2, rt)
    rt_ref[...] = rt


def _out_proj_router(y, x, w_out, norm_g, w_router, b_router, n_first_rows):
    n, d = x.shape
    tm = ROW_TILE
    row = lambda i: (i, 0)
    const = lambda i: (0, 0)
    return pl.pallas_call(
        functools.partial(_out_proj_router_kernel, n_first=n_first_rows // tm), grid=(n // tm,),
        in_specs=[pl.BlockSpec((tm, d), row), pl.BlockSpec((tm, d), row), pl.BlockSpec((d, d), const),
                  pl.BlockSpec((1, d), const), pl.BlockSpec((d, LANES), const), pl.BlockSpec((8, LANES), const)],
        out_specs=[pl.BlockSpec((tm, d), row), pl.BlockSpec((tm, d), row), pl.BlockSpec((tm, LANES), row),
                   pl.BlockSpec((8, LANES), const)],
        out_shape=[jax.ShapeDtypeStruct((n, d), F32), jax.ShapeDtypeStruct((n, d), F32),
                   jax.ShapeDtypeStruct((n, LANES), F32), jax.ShapeDtypeStruct((8, LANES), F32)],
        scratch_shapes=[pltpu.VMEM((8, LANES), F32)],
        compiler_params=_cparams(("arbitrary",)), name="out_proj_router",
    )(y, x, w_out.astype(BF16), norm_g.reshape(1, d), w_router.astype(BF16), b_router)


def _expert_kernel(be_ref, nv_ref, x_ref, wg_ref, wu_ref, wd_ref, o_ref, wg_s, wu_s, wd_s):
    i = pl.program_id(0)
    nv = nv_ref[i]

    @pl.when((i == 0) | (be_ref[i] != be_ref[jnp.maximum(i - 1, 0)]))
    def _():
        wg_s[...] = wg_ref[0].astype(BF16)
        wu_s[...] = wu_ref[0].astype(BF16)
        wd_s[...] = wd_ref[0].astype(BF16)

    @pl.when(nv > 0)
    def _():
        rows = lax.broadcasted_iota(jnp.int32, (x_ref.shape[0], 1), 0)
        xb = jnp.where(rows < nv, x_ref[...], 0.0).astype(BF16)
        hg = jnp.dot(xb, wg_s[...], preferred_element_type=F32)
        hu = jnp.dot(xb, wu_s[...], preferred_element_type=F32)
        hb = (_silu(hg) * hu).astype(BF16)
        o_ref[...] = jnp.dot(hb, wd_s[...], preferred_element_type=F32)

    @pl.when(nv == 0)
    def _():
        o_ref[...] = jnp.zeros_like(o_ref)


def _experts(rows, block_exp, block_valid, w_gate, w_up, w_down):
    r, d = rows.shape
    bm = MOE_ROWS
    nblk = r // bm
    grid_spec = pltpu.PrefetchScalarGridSpec(
        num_scalar_prefetch=2, grid=(nblk,),
        in_specs=[pl.BlockSpec((bm, d), lambda i, be, nv: (i, 0)),
                  pl.BlockSpec((1, d, D_EXPERT), lambda i, be, nv: (be[i], 0, 0)),
                  pl.BlockSpec((1, d, D_EXPERT), lambda i, be, nv: (be[i], 0, 0)),
                  pl.BlockSpec((1, D_EXPERT, d), lambda i, be, nv: (be[i], 0, 0))],
        out_specs=pl.BlockSpec((bm, d), lambda i, be, nv: (i, 0)),
        scratch_shapes=[pltpu.VMEM((d, D_EXPERT), BF16), pltpu.VMEM((d, D_EXPERT), BF16),
                        pltpu.VMEM((D_EXPERT, d), BF16)])
    return pl.pallas_call(
        _expert_kernel, grid_spec=grid_spec, out_shape=jax.ShapeDtypeStruct((r, d), F32),
        compiler_params=_cparams(("arbitrary",)), name="experts",
    )(block_exp, block_valid, rows, w_gate, w_up, w_down)


def _gather_rows(table, idx):
    info = plsc.get_sparse_core_info()
    n_workers = info.num_cores * info.num_subcores
    n_rows = idx.shape[0]
    width = table.shape[1]
    ch = SC_GATHER_ROWS
    per_worker = n_rows // n_workers
    n_chunks = per_worker // ch
    assert n_workers * n_chunks * ch == n_rows
    mesh = plsc.VectorSubcoreMesh(core_axis_name="c", subcore_axis_name="s")

    @functools.partial(
        pl.kernel, mesh=mesh, out_type=jax.ShapeDtypeStruct((n_rows, width), table.dtype),
        scratch_types=[pltpu.VMEM((ch,), jnp.int32), pltpu.VMEM((ch, width), table.dtype),
                       pltpu.SemaphoreType.DMA])
    def gather(table_hbm, idx_hbm, out_hbm, idx_v, rows_v, sem):
        wid = lax.axis_index("s") * info.num_cores + lax.axis_index("c")
        base = wid * per_worker

        @pl.loop(0, n_chunks)
        def _(j):
            off = pl.multiple_of(base + j * ch, ch)
            pltpu.sync_copy(idx_hbm.at[pl.ds(off, ch)], idx_v)
            pltpu.async_copy(table_hbm.at[idx_v], rows_v, sem).wait()
            pltpu.sync_copy(rows_v, out_hbm.at[pl.ds(off, ch)])

    return gather(table, idx)


def _pad_to(n, m):
    return -(-n // m) * m


def _moe(h, rt, counts_row, w_gate, w_up, w_down):
    n, d = h.shape
    bm = MOE_ROWS
    sc_quant = 32 * SC_GATHER_ROWS
    counts = counts_row[N_GROUPS:N_GROUPS + N_EXPERTS].astype(jnp.int32)
    e = rt[:, 0:2].astype(jnp.int32)
    rank = rt[:, 2:4].astype(jnp.int32)
    padded = ((counts + bm - 1) // bm) * bm
    pend = jnp.cumsum(padded)
    pstart = pend - padded
    start = jnp.cumsum(counts) - counts
    n_rows = _pad_to(_pad_to(2 * n, bm) + N_EXPERTS * bm, sc_quant)
    n_rows = _pad_to(n_rows, bm)
    nblk = n_rows // bm
    blk_start = jnp.arange(nblk, dtype=jnp.int32) * bm
    block_exp = jnp.minimum(jnp.searchsorted(pend, blk_start, side="right"), N_EXPERTS - 1).astype(jnp.int32)
    block_valid = jnp.clip(counts[block_exp] - (blk_start - pstart[block_exp]), 0, bm).astype(jnp.int32)
    block_valid = jnp.where(blk_start < pend[-1], block_valid, 0)
    order = jnp.argsort(e.reshape(-1), stable=True).astype(jnp.int32)
    tok_sorted = order // 2
    r_idx = jnp.arange(n_rows, dtype=jnp.int32)
    r_exp = jnp.repeat(block_exp, bm)
    j = r_idx - pstart[r_exp]
    src = jnp.where(j < counts[r_exp], tok_sorted[jnp.clip(start[r_exp] + j, 0, 2 * n - 1)], 0)
    rows = _gather_rows(h, src)
    out_rows = _experts(rows, block_exp, block_valid, w_gate, w_up, w_down)
    dest = pstart[e] + rank
    n_pad = _pad_to(n, sc_quant)
    dest = jnp.pad(dest, ((0, n_pad - n), (0, 0)))
    ga = _gather_rows(out_rows, dest[:, 0])
    gb = _gather_rows(out_rows, dest[:, 1])
    return ga, gb


def _chunk_masks(L):
    r = lax.broadcasted_iota(jnp.int32, (L, L), 0)
    c = lax.broadcasted_iota(jnp.int32, (L, L), 1)
    return r, c


def _row_from_col(col, r, c):
    return jnp.sum(jnp.where(r == c, col, 0.0), axis=0, keepdims=True)


def _cumsum_col_row(col, r, c):
    row = _row_from_col(col, r, c)
    cs_col = jnp.sum(jnp.where(r >= c, row, 0.0), axis=1, keepdims=True)
    cs_row = jnp.sum(jnp.where(r <= c, col, 0.0), axis=0, keepdims=True)
    return cs_col, cs_row


def _mlstm_kernel(q_ref, k_ref, v_ref, o_ref, gc_ref, c0_ref, n0_ref, m0_ref, na_ref,
                  y_ref, c1_ref, n1_ref, m1_ref, c_s, n_s, m_s, *, L, n_chunks, n_heads):
    h0 = pl.program_id(1) * n_heads
    t = pl.program_id(2)

    @pl.when(t == 0)
    def _():
        c_s[...] = c0_ref[0]
        n_s[...] = n0_ref[0]
        m_s[...] = m0_ref[0]

    r, c = _chunk_masks(L)
    causal = r >= c
    na = na_ref[...]

    def head_chunk(sl, j):
        hs = slice(j * HEAD_DIM, (j + 1) * HEAD_DIM)
        q = q_ref[sl, hs]
        k = k_ref[sl, hs] * HEAD_DIM ** -0.5
        v = v_ref[sl, hs]
        gates = gc_ref[sl, :]
        ig_col = _lane_pick(gates, h0 + j)
        lf_col = _lane_pick(gates, H_A + h0 + j)
        ig_row = _row_from_col(ig_col, r, c)
        b_col, b_row = _cumsum_col_row(lf_col, r, c)
        cmat = c_s[j]
        nvec = n_s[j]
        m_prev = m_s[j, 0:1, 0:1]
        dmat = jnp.where(causal, b_col - b_row + ig_row, -jnp.inf)
        inter = b_col + m_prev
        m_t = jnp.maximum(inter, jnp.max(dmat, axis=-1, keepdims=True))
        s = _dot_nt(q, k) * jnp.exp(dmat - m_t)
        w_inter = jnp.exp(inter - m_t)
        num = w_inter * _dot(q, cmat) + _dot(s, v)
        qn = jnp.sum(_round_bf16(q) * _round_bf16(nvec), axis=-1, keepdims=True)
        den = w_inter * qn + jnp.sum(s, axis=-1, keepdims=True)
        hout = num / jnp.maximum(jnp.abs(den), jnp.exp(-m_t))
        m_new = m_t[L - 1:L, :]
        b_last = b_col[L - 1:L, :]
        wk = jnp.exp(b_last - b_col + ig_col - m_new)
        dec = jnp.exp(b_last + m_prev - m_new)
        kw = k * wk
        c_s[j] = dec * cmat + _dot_tn(kw, v)
        n_s[j] = dec * nvec + jnp.sum(_round_bf16(wk) * _round_bf16(k), axis=0, keepdims=True)
        m_s[j] = jnp.broadcast_to(m_new, (1, HEAD_DIM))
        y_ref[sl, hs] = _rms(hout, na) * _sigmoid(o_ref[sl, hs])

    def chunk(ci, carry):
        sl = pl.ds(pl.multiple_of(ci * L, L), L)
        for j in range(n_heads):
            head_chunk(sl, j)
        return carry

    lax.fori_loop(0, n_chunks, chunk, 0)

    @pl.when(t == pl.num_programs(2) - 1)
    def _():
        c1_ref[0] = c_s[...]
        n1_ref[0] = n_s[...]
        m1_ref[0] = m_s[...]


def _retention_kernel(q_ref, k_ref, v_ref, g_ref, cos_ref, sin_ref, s0_ref, nb_ref, lg_ref,
                      y_ref, s1_ref, s_s, *, L, n_chunks, n_heads):
    t = pl.program_id(2)

    @pl.when(t == 0)
    def _():
        s_s[...] = s0_ref[0]

    r, c = _chunk_masks(L)
    causal = r >= c
    rel = (r - c).astype(F32)
    idx = lax.broadcasted_iota(jnp.int32, (L, 1), 0).astype(F32)
    nb = nb_ref[...]
    half = HEAD_DIM // 2
    decays = []
    for j in range(n_heads):
        lg = lg_ref[j, 0:1, 0:1]
        decays.append((jnp.where(causal, jnp.exp(jnp.where(causal, rel, 0.0) * lg), 0.0),
                       jnp.exp((idx + 1.0) * lg), jnp.exp((L - 1.0 - idx) * lg), jnp.exp(L * lg)))

    def rot(x, cos, sin):
        return x * cos + pltpu.roll(x, half, 1) * sin

    def chunk(ci, carry):
        sl = pl.ds(pl.multiple_of(ci * L, L), L)
        cos = cos_ref[sl, :]
        sin = sin_ref[sl, :]
        for j in range(n_heads):
            hs = slice(j * HEAD_DIM, (j + 1) * HEAD_DIM)
            dmat, q_dec, k_dec, c_dec = decays[j]
            q = rot(q_ref[sl, hs], cos, sin)
            k = rot(k_ref[sl, hs], cos, sin) * HEAD_DIM ** -0.5
            v = v_ref[sl, hs]
            smat = s_s[j]
            att = _dot_nt(q, k) * dmat
            o = _dot(att, v) + q_dec * _dot(q, smat)
            s_s[j] = c_dec * smat + _dot_tn(k * k_dec, v)
            y_ref[sl, hs] = _rms(o, nb) * _silu(g_ref[sl, hs])
        return carry

    lax.fori_loop(0, n_chunks, chunk, 0)

    @pl.when(t == pl.num_programs(2) - 1)
    def _():
        s1_ref[0] = s_s[...]


def _unit_lower_inverse(a, r, c):
    L = a.shape[0]
    eye = jnp.where(r == c, 1.0, 0.0)
    shift = 4
    same = (r >> shift) == (c >> shift)
    p = -jnp.where(same, a, 0.0)
    inv = eye + p
    for _ in range(3):
        ps = _split2(p)
        p = _dot_x3(ps, ps)
        inv = inv + _dot_x3(_split2(inv), _split2(p))
    while (1 << shift) < L:
        shift += 1
        same2 = (r >> shift) == (c >> shift)
        off = jnp.where(same2 & jnp.logical_not(same), a, 0.0)
        invs = _split2(inv)
        inv = inv - _dot_x3(invs, _split2(_dot_x3(_split2(off), invs)))
        same = same2
    return inv


def _gdn_kernel(q_ref, k_ref, v_ref, gt_ref, gc_ref, cwq_ref, cwk_ref, cwv_ref, hq_ref, hk_ref, hv_ref,
                s0_ref, nc_ref, y_ref, s1_ref, s_s, eq_s, ek_s, ev_s, qn_s, kn_s, vn_s, nq_s, b_s, o0_s,
                *, L, n_chunks):
    h = pl.program_id(1)
    t = pl.program_id(2)
    tb = q_ref.shape[0]
    hist = GDN_HIST_ROWS

    @pl.when(t == 0)
    def _():
        s_s[...] = s0_ref[0, 0]
        eq_s[0:hist, :] = _round_bf16(hq_ref[0])
        ek_s[0:hist, :] = _round_bf16(hk_ref[0])
        ev_s[0:hist, :] = _round_bf16(hv_ref[0])

    def conv_silu(x_ref, e_s, cw_ref):
        e_s[hist:hist + tb, :] = _round_bf16(x_ref[...])
        cw = _round_bf16(cw_ref[...])
        acc = e_s[hist - (CONV_W - 1):hist - (CONV_W - 1) + tb, :] * cw[0:1, :]
        for tap in range(1, CONV_W):
            lo = hist - (CONV_W - 1) + tap
            acc = acc + e_s[lo:lo + tb, :] * cw[tap:tap + 1, :]
        e_s[0:hist, :] = e_s[tb:tb + hist, :]
        return _silu(acc)

    def l2n(x):
        return x * lax.rsqrt(jnp.sum(x * x, axis=-1, keepdims=True) + RMS_EPS)

    qn_s[...] = l2n(conv_silu(q_ref, eq_s, cwq_ref)) * HEAD_DIM ** -0.5
    kn_s[...] = l2n(conv_silu(k_ref, ek_s, cwk_ref))
    vn_s[...] = conv_silu(v_ref, ev_s, cwv_ref)

    r, c = _chunk_masks(L)
    lower = r >= c
    strict = r > c
    ncw = nc_ref[...]
    hd = HEAD_DIM

    decays = []
    for ci in range(n_chunks):
        sl = slice(ci * L, (ci + 1) * L)
        q = qn_s[sl, :]
        k = kn_s[sl, :]
        v = vn_s[sl, :]
        gates = gc_ref[sl, :]
        beta = _lane_pick(gates, h)
        g_col = _lane_pick(gates, H_C + h)
        G_col, G_row = _cumsum_col_row(g_col, r, c)
        gam = jnp.where(lower, jnp.exp(jnp.where(lower, G_col - G_row, 0.0)), 0.0)
        kq = _dot_nt(jnp.concatenate([k, q], axis=0), k)
        a = jnp.where(strict, beta * kq[0:L, :] * gam, 0.0)
        tinv = _unit_lower_inverse(a, r, c)
        eg = jnp.exp(G_col)
        rhs = jnp.concatenate([v * beta, k * (beta * eg)], axis=1)
        uw = _dot_x3(_split2(tinv), _split2(rhs))
        g_last = G_col[L - 1:L, :]
        k2 = k * jnp.exp(g_last - G_col)
        bn = _dot_tn(k2, uw)
        ao = _dot(kq[L:2 * L, :] * gam, uw)
        nq_s[ci, 0:hd, :] = bn[:, hd:2 * hd]
        nq_s[ci, hd:hd + L, :] = q * eg - ao[:, hd:2 * hd]
        b_s[ci] = bn[:, 0:hd]
        o0_s[ci] = ao[:, 0:hd]
        decays.append(jnp.exp(g_last))

    smat = s_s[...]
    for ci in range(n_chunks):
        sl = slice(ci * L, (ci + 1) * L)
        ns = _dot(nq_s[ci], smat)
        o = ns[hd:hd + L, :] + o0_s[ci]
        y_ref[sl, :] = _rms(o, ncw) * _silu(gt_ref[sl, :])
        smat = decays[ci] * smat - ns[0:hd, :] + b_s[ci]
    s_s[...] = smat

    @pl.when(t == pl.num_programs(2) - 1)
    def _():
        s1_ref[0, 0] = s_s[...]


def _scan_geometry(t_len):
    tb = min(SCAN_BLOCK, t_len)
    L = min(SCAN_CHUNK, t_len)
    return tb, L, t_len // tb, tb // L


def _col_spec(tb, nt, row0_blocks, col_fn, nh=1):
    return pl.BlockSpec((tb, nh * HEAD_DIM), lambda b, g, t: (row0_blocks + b * nt + t, col_fn(g)))


def _state_spec(shape_tail, nh=1):
    nd = len(shape_tail)
    return pl.BlockSpec((1, nh) + shape_tail, lambda b, g, t: (b, g) + (0,) * nd)


def _mlstm_scan(z, gc, y_prev, row0, bsz, t_len, c0, n0, m0, norm_a):
    tb, L, nt, n_chunks = _scan_geometry(t_len)
    rb = row0 // tb
    hb = HEAD_DIM
    nh = SCAN_HEADS
    ng = H_A // nh
    in_specs = [_col_spec(tb, nt, rb, lambda g: g, nh), _col_spec(tb, nt, rb, lambda g: ng + g, nh),
                _col_spec(tb, nt, rb, lambda g: 2 * ng + g, nh), _col_spec(tb, nt, rb, lambda g: 3 * ng + g, nh),
                pl.BlockSpec((tb, GATE_LANES), lambda b, g, t: (rb + b * nt + t, 0)),
                _state_spec((hb, hb), nh), _state_spec((1, hb), nh), _state_spec((1, hb), nh),
                pl.BlockSpec((1, hb), lambda b, g, t: (0, 0))]
    out_specs = [_col_spec(tb, nt, rb, lambda g: g, nh),
                 _state_spec((hb, hb), nh), _state_spec((1, hb), nh), _state_spec((1, hb), nh)]
    out_shape = [jax.ShapeDtypeStruct(y_prev.shape, F32),
                 jax.ShapeDtypeStruct((bsz, H_A, hb, hb), F32), jax.ShapeDtypeStruct((bsz, H_A, 1, hb), F32),
                 jax.ShapeDtypeStruct((bsz, H_A, 1, hb), F32)]
    m0b = jnp.broadcast_to(m0[:, :, None, None], (bsz, H_A, 1, hb))
    kern = functools.partial(_mlstm_kernel_aliased, L=L, n_chunks=n_chunks, n_heads=nh)
    y, c1, n1, m1 = pl.pallas_call(
        kern, grid=(bsz, ng, nt), in_specs=in_specs + [pl.BlockSpec(memory_space=pl.ANY)],
        out_specs=out_specs, out_shape=out_shape,
        scratch_shapes=[pltpu.VMEM((nh, hb, hb), F32), pltpu.VMEM((nh, 1, hb), F32), pltpu.VMEM((nh, 1, hb), F32)],
        input_output_aliases={9: 0},
        compiler_params=_cparams(("parallel", "parallel", "arbitrary")), name="mlstm_scan",
    )(z, z, z, z, gc, c0, n0[:, :, None, :], m0b, norm_a.reshape(1, hb), y_prev)
    return y, c1, n1[:, :, 0, :], m1[:, :, 0, 0]


def _mlstm_kernel_aliased(*refs, **kw):
    return _mlstm_kernel(*refs[:9], *refs[10:], **kw)


def _retention_scan(z, cos, sin, y_prev, row0, bsz, t_len, s0, norm_b):
    tb, L, nt, n_chunks = _scan_geometry(t_len)
    rb = row0 // tb
    hb = HEAD_DIM
    nh = SCAN_HEADS
    ng = H_B // nh
    base = 4 * (H_A // nh)
    in_specs = [_col_spec(tb, nt, rb, lambda g: base + g, nh), _col_spec(tb, nt, rb, lambda g: base + ng + g, nh),
                _col_spec(tb, nt, rb, lambda g: base + 2 * ng + g, nh),
                _col_spec(tb, nt, rb, lambda g: base + 3 * ng + g, nh),
                pl.BlockSpec((tb, hb), lambda b, g, t: (t, 0)), pl.BlockSpec((tb, hb), lambda b, g, t: (t, 0)),
                _state_spec((hb, hb), nh), pl.BlockSpec((1, hb), lambda b, g, t: (0, 0)),
                pl.BlockSpec((nh, 1, hb), lambda b, g, t: (g, 0, 0)),
                pl.BlockSpec(memory_space=pl.ANY)]
    out_specs = [_col_spec(tb, nt, rb, lambda g: H_A // nh + g, nh), _state_spec((hb, hb), nh)]
    out_shape = [jax.ShapeDtypeStruct(y_prev.shape, F32), jax.ShapeDtypeStruct((bsz, H_B, hb, hb), F32)]
    log_gamma = jnp.log(1.0 - 2.0 ** (-5.0 - jnp.arange(H_B, dtype=F32)))
    lg_tab = jnp.broadcast_to(log_gamma[:, None, None], (H_B, 1, hb))
    kern = functools.partial(_retention_kernel_aliased, L=L, n_chunks=n_chunks, n_heads=nh)
    return pl.pallas_call(
        kern, grid=(bsz, ng, nt), in_specs=in_specs, out_specs=out_specs, out_shape=out_shape,
        scratch_shapes=[pltpu.VMEM((nh, hb, hb), F32)], input_output_aliases={9: 0},
        compiler_params=_cparams(("parallel", "parallel", "arbitrary")), name="retention_scan",
    )(z, z, z, z, cos, sin, s0, norm_b.reshape(1, hb), lg_tab, y_prev)


def _retention_kernel_aliased(*refs, **kw):
    return _retention_kernel(*refs[:9], *refs[10:], **kw)


def _gdn_scan(z, gc, y_prev, row0, bsz, t_len, conv_w, conv_hist, s0, norm_c):
    tb, L, nt, n_chunks = _scan_geometry(t_len)
    rb = row0 // tb
    hb = HEAD_DIM
    cw_spec = lambda off: pl.BlockSpec((CONV_W, hb), lambda b, h, t: (0, off + h))
    hist_spec = lambda off: pl.BlockSpec((1, GDN_HIST_ROWS, hb), lambda b, h, t: (b, 0, off + h))
    in_specs = [_col_spec(tb, nt, rb, lambda h: h), _col_spec(tb, nt, rb, lambda h: H_C + h),
                _col_spec(tb, nt, rb, lambda h: 2 * H_C + h), _col_spec(tb, nt, rb, lambda h: 3 * H_C + h),
                pl.BlockSpec((tb, GATE_LANES), lambda b, h, t: (rb + b * nt + t, 0)),
                cw_spec(0), cw_spec(H_C), cw_spec(2 * H_C),
                hist_spec(0), hist_spec(H_C), hist_spec(2 * H_C),
                _state_spec((hb, hb)), pl.BlockSpec((1, hb), lambda b, h, t: (0, 0)),
                pl.BlockSpec(memory_space=pl.ANY)]
    out_specs = [_col_spec(tb, nt, rb, lambda h: h), _state_spec((hb, hb))]
    out_shape = [jax.ShapeDtypeStruct(y_prev.shape, F32), jax.ShapeDtypeStruct((bsz, H_C, hb, hb), F32)]
    kern = functools.partial(_gdn_kernel_aliased, L=L, n_chunks=n_chunks)
    ext = pltpu.VMEM((tb + GDN_HIST_ROWS, hb), F32)
    blk = pltpu.VMEM((tb, hb), F32)
    return pl.pallas_call(
        kern, grid=(bsz, H_C, nt), in_specs=in_specs, out_specs=out_specs, out_shape=out_shape,
        scratch_shapes=[pltpu.VMEM((hb, hb), F32), ext, ext, ext, blk, blk, blk,
                        pltpu.VMEM((n_chunks, hb + L, hb), F32), pltpu.VMEM((n_chunks, hb, hb), F32),
                        pltpu.VMEM((n_chunks, L, hb), F32)],
        input_output_aliases={13: 0},
        compiler_params=_cparams(("parallel", "parallel", "arbitrary")), name="gdn_scan",
    )(z, z, z, z, gc, conv_w, conv_w, conv_w, conv_hist, conv_hist, conv_hist, s0, norm_c.reshape(1, hb), y_prev)


def _gdn_kernel_aliased(*refs, **kw):
    return _gdn_kernel(*refs[:13], *refs[14:], **kw)


def _rope_tables(pos):
    half = HEAD_DIM // 2
    inv = ROPE_BASE ** (-jnp.arange(half, dtype=F32) / half)
    ang = pos.astype(F32)[:, None] * inv[None, :]
    cos, sin = jnp.cos(ang), jnp.sin(ang)
    return jnp.concatenate([cos, cos], axis=-1), jnp.concatenate([-sin, sin], axis=-1)


def _gate_rows(*rows):
    out = jnp.zeros((8, GATE_LANES), F32)
    for i, rvals in enumerate(rows):
        out = out.at[i, :rvals.shape[0]].set(rvals.astype(F32))
    return out


def _pad_cols(w, width):
    return jnp.pad(w, ((0, 0), (0, width - w.shape[1])))


def kernel(x_prompt, x_sample, state_mlstm_c, state_mlstm_n, state_mlstm_m, state_ret, state_gdn, state_gdn_conv, norm_mix, norm_ffn, norm_final, ab_w_in, ab_gate_bias, ab_w_out, ab_norm_a, ab_norm_b, c_w_in, c_conv_w, c_a_log, c_dt_bias, c_norm, c_w_out, moe_w_grp, moe_b_grp, moe_w_exp, moe_b_exp, moe_w_gate, moe_w_up, moe_w_down):
    bp, tp, d = x_prompt.shape
    bs, ts, _ = x_sample.shape
    n_p, n_s = bp * tp, bs * ts
    n = n_p + n_s
    x = jnp.concatenate([x_prompt.reshape(n_p, d), x_sample.reshape(n_s, d)], axis=0)
    hd = HEAD_DIM
    qkv_a = 4 * H_A * hd
    n_gate_a = 2 * H_A
    c_qkv = 3 * H_C * hd
    n_gate_c = 2 * H_C

    def moe_layer(y, xres, w_out, layer):
        w_router = _pad_cols(jnp.concatenate([moe_w_grp[layer], moe_w_exp[layer]], axis=1), LANES)
        b_router = _gate_rows(jnp.concatenate([moe_b_grp[layer], moe_b_exp[layer]]))
        x1, hmoe, rt, cnt = _out_proj_router(y, xres, w_out, norm_ffn[layer], w_router, b_router, n_p)
        ga, gb = _moe(hmoe, rt, cnt[0], moe_w_gate[layer], moe_w_up[layer], moe_w_down[layer])
        return x1, ga, gb, rt

    w_in = ab_w_in[0]
    w_main = jnp.concatenate([w_in[:, :qkv_a], w_in[:, qkv_a + n_gate_a:]], axis=1)
    w_gate = _pad_cols(w_in[:, qkv_a:qkv_a + n_gate_a], GATE_LANES)
    z, gc = _in_proj(x, norm_mix[0], w_main, w_gate, _gate_rows(ab_gate_bias[0]), 0, n_p)
    y = jnp.zeros((n, d), F32)
    zeros = lambda *s: jnp.zeros(s, F32)
    cos_p, sin_p = _rope_tables(jnp.arange(tp))
    cos_s, sin_s = _rope_tables(PAST_LEN + jnp.arange(ts))
    y, c_p, n_pm, m_p = _mlstm_scan(z, gc, y, 0, bp, tp, zeros(bp, H_A, hd, hd), zeros(bp, H_A, hd),
                                    zeros(bp, H_A), ab_norm_a[0])
    y, c_s, n_sm, m_s = _mlstm_scan(z, gc, y, n_p, bs, ts, state_mlstm_c[0], state_mlstm_n[0],
                                    state_mlstm_m[0], ab_norm_a[0])
    y, r_p = _retention_scan(z, cos_p, sin_p, y, 0, bp, tp, zeros(bp, H_B, hd, hd), ab_norm_b[0])
    y, r_s = _retention_scan(z, cos_s, sin_s, y, n_p, bs, ts, state_ret[0], ab_norm_b[0])
    x1, ga, gb, rt = moe_layer(y, x, ab_w_out[0], 0)

    w_in = c_w_in[0]
    w_main = jnp.concatenate([w_in[:, :c_qkv], w_in[:, c_qkv + n_gate_c:]], axis=1)
    w_gate = _pad_cols(w_in[:, c_qkv:c_qkv + n_gate_c], GATE_LANES)
    zero8 = jnp.zeros((H_C,), F32)
    gparams = _gate_rows(jnp.concatenate([zero8, c_dt_bias[0]]), jnp.concatenate([zero8, c_a_log[0]]))
    x2, z, gc = _in_proj(x1, norm_mix[1], w_main, w_gate, gparams, 1, n_p, comb=(ga, gb, rt))
    hist_p = jnp.zeros((bp, GDN_HIST_ROWS, c_qkv), F32)
    hist_s = jnp.pad(state_gdn_conv[0], ((0, 0), (GDN_HIST_ROWS - (CONV_W - 1), 0), (0, 0)))
    y = jnp.zeros((n, d), F32)
    y, g_p = _gdn_scan(z, gc, y, 0, bp, tp, c_conv_w[0], hist_p, zeros(bp, H_C, hd, hd), c_norm[0])
    y, g_s = _gdn_scan(z, gc, y, n_p, bs, ts, c_conv_w[0], hist_s, state_gdn[0], c_norm[0])
    qkv_p = z[:n_p, :c_qkv].reshape(bp, tp, c_qkv)
    qkv_s = z[n_p:, :c_qkv].reshape(bs, ts, c_qkv)
    conv_p = qkv_p[:, tp - (CONV_W - 1):, :]
    conv_s = qkv_s[:, ts - (CONV_W - 1):, :]
    x3, ga, gb, rt = moe_layer(y, x2, c_w_out[0], 1)

    yf_p, yf_s = _final_norm(x3, ga, gb, rt, norm_final, n_p)
    y_prompt = yf_p.reshape(bp, tp, d)
    y_sample = yf_s.reshape(bs, ts, d)
    return (y_prompt, y_sample, c_p[None], n_pm[None], m_p[None], r_p[None], g_p[None], conv_p[None],
            c_s[None], n_sm[None], m_s[None], r_s[None], g_s[None], conv_s[None])
```

```python
import functools
import math

import jax
import jax.numpy as jnp
import numpy as np
from jax import lax
from jax.experimental import pallas as pl
from jax.experimental.pallas import tpu as pltpu
from jax.experimental.pallas import tpu_sc as plsc

F32 = jnp.float32
BF16 = jnp.bfloat16

D_MODEL = 1024
H_A = 4
H_B = 4
H_C = 8
HEAD_DIM = 128
CONV_W = 4
N_GROUPS = 4
EXP_PER_GROUP = 8
N_EXPERTS = N_GROUPS * EXP_PER_GROUP
D_EXPERT = 512
RMS_EPS = 1e-6
ROPE_BASE = 10000.0
PAST_LEN = 2048

LANES = 128
GATE_LANES = LANES
VMEM_LIMIT = 56 * 1024 * 1024

ROW_TILE = 256
SCAN_BLOCK = 512
SCAN_CHUNK = 64
SCAN_HEADS = 4
MOE_ROWS = 256
SC_GATHER_ROWS = 64
GDN_HIST_ROWS = 8


def _cparams(sem):
    return pltpu.CompilerParams(dimension_semantics=sem, vmem_limit_bytes=VMEM_LIMIT)


def _dot(a, b):
    return jnp.dot(a.astype(BF16), b.astype(BF16), preferred_element_type=F32)


def _dot_nt(a, b):
    return lax.dot_general(a.astype(BF16), b.astype(BF16), (((1,), (1,)), ((), ())),
                           preferred_element_type=F32)


def _dot_tn(a, b):
    return lax.dot_general(a.astype(BF16), b.astype(BF16), (((0,), (0,)), ((), ())),
                           preferred_element_type=F32)


def _split2(x):
    hi = x.astype(BF16)
    return hi, (x - hi.astype(F32)).astype(BF16)


def _dot_x3(a, b):
    return (jnp.dot(a[0], b[0], preferred_element_type=F32)
            + (jnp.dot(a[0], b[1], preferred_element_type=F32) + jnp.dot(a[1], b[0], preferred_element_type=F32)))


def _round_bf16(x):
    return x.astype(BF16).astype(F32)


def _softplus(x):
    return jnp.maximum(x, 0.0) + jnp.log1p(jnp.exp(-jnp.abs(x)))


def _sigmoid(x):
    return 1.0 / (1.0 + jnp.exp(-x))


def _silu(x):
    return x * _sigmoid(x)


def _rms(x, g):
    return x * lax.rsqrt(jnp.mean(x * x, axis=-1, keepdims=True) + RMS_EPS) * g


def _mean_sq_sublane_order(x):
    rows, d = x.shape
    acc = None
    for c in range(d // LANES):
        xc = x[:, c * LANES:(c + 1) * LANES]
        acc = xc * xc if acc is None else acc + xc * xc
    acc_t = acc.T
    s8 = acc_t[0:8, :]
    for a in range(1, LANES // 8):
        s8 = s8 + acc_t[8 * a:8 * a + 8, :]
    ms_row = jnp.sum(s8, axis=0, keepdims=True) * (1.0 / d)
    r = lax.broadcasted_iota(jnp.int32, (rows, rows), 0)
    c = lax.broadcasted_iota(jnp.int32, (rows, rows), 1)
    return jnp.sum(jnp.where(r == c, ms_row, 0.0), axis=1, keepdims=True)


def _rms_rows(x, g, sublane_order):
    ms = jnp.where(sublane_order, _mean_sq_sublane_order(x), jnp.mean(x * x, axis=-1, keepdims=True))
    return x * lax.rsqrt(ms + RMS_EPS) * g


def _lane_pick(tile, idx):
    lane = lax.broadcasted_iota(jnp.int32, tile.shape, 1)
    return jnp.sum(jnp.where(lane == idx, tile, 0.0), axis=-1, keepdims=True)


def _combine(x_ref, ga_ref, gb_ref, rt_ref):
    rt = rt_ref[...]
    return x_ref[...] + (_lane_pick(rt, 4) * ga_ref[...] + _lane_pick(rt, 5) * gb_ref[...])


def _in_proj_kernel(*refs, combine, gate_mode, n_col_chunks, col_chunk, n_first):
    first = pl.program_id(0) < n_first
    if combine:
        x_ref, ga_ref, gb_ref, rt_ref = refs[:4]
        refs = refs[4:]
        x = _combine(x_ref, ga_ref, gb_ref, rt_ref)
    else:
        x_ref = refs[0]
        refs = refs[1:]
        x = x_ref[...]
    g_ref, w_ref, wg_ref, gp_ref = refs[:4]
    outs = refs[4:]
    if combine:
        xo_ref, z_ref, gc_ref = outs
        xo_ref[...] = x
    else:
        z_ref, gc_ref = outs
    xn = _rms_rows(x, g_ref[...], first)
    xh = xn.astype(BF16)
    for c in range(n_col_chunks):
        cs = slice(c * col_chunk, (c + 1) * col_chunk)
        z_ref[:, cs] = jnp.dot(xh, w_ref[:, cs], preferred_element_type=F32)
    raw = jnp.dot(xh, wg_ref[...], preferred_element_type=F32)
    lane = lax.broadcasted_iota(jnp.int32, raw.shape, 1)
    bias = gp_ref[0:1, :]
    if gate_mode == 0:
        val = raw + bias
        gc_ref[...] = jnp.where(lane < H_A, val, -_softplus(-val))
    else:
        neg_a = -jnp.exp(gp_ref[1:2, :])
        gc_ref[...] = jnp.where(lane < H_C, _sigmoid(raw), neg_a * _softplus(raw + bias))


def _in_proj(x, norm_g, w_main, w_gate, gate_params, gate_mode, n_first_rows, comb=None):
    n, d = x.shape
    nz = w_main.shape[1]
    tm = ROW_TILE
    col_chunk = 512
    row = lambda i: (i, 0)
    const = lambda i: (0, 0)
    in_specs = [pl.BlockSpec((tm, d), row)]
    args = [x]
    out_specs = []
    out_shape = []
    if comb is not None:
        ga, gb, rt = comb
        in_specs += [pl.BlockSpec((tm, d), row), pl.BlockSpec((tm, d), row), pl.BlockSpec((tm, LANES), row)]
        args += [ga, gb, rt]
        out_specs.append(pl.BlockSpec((tm, d), row))
        out_shape.append(jax.ShapeDtypeStruct((n, d), F32))
    in_specs += [pl.BlockSpec((1, d), const), pl.BlockSpec((d, nz), const),
                 pl.BlockSpec((d, GATE_LANES), const), pl.BlockSpec((8, GATE_LANES), const)]
    args += [norm_g.reshape(1, d), w_main.astype(BF16), w_gate.astype(BF16), gate_params]
    out_specs += [pl.BlockSpec((tm, nz), row), pl.BlockSpec((tm, GATE_LANES), row)]
    out_shape += [jax.ShapeDtypeStruct((n, nz), F32), jax.ShapeDtypeStruct((n, GATE_LANES), F32)]
    kern = functools.partial(_in_proj_kernel, combine=comb is not None, gate_mode=gate_mode,
                             n_col_chunks=nz // col_chunk, col_chunk=col_chunk, n_first=n_first_rows // tm)
    return pl.pallas_call(
        kern, grid=(n // tm,), in_specs=in_specs, out_specs=out_specs, out_shape=out_shape,
        compiler_params=_cparams(("parallel",)), name=f"in_proj_{gate_mode}")(*args)


def _final_kernel(x_ref, ga_ref, gb_ref, rt_ref, g_ref, y1_ref, y2_ref, *, n_first):
    i = pl.program_id(0)
    y = _rms_rows(_combine(x_ref, ga_ref, gb_ref, rt_ref), g_ref[...], i < n_first)

    @pl.when(i < n_first)
    def _():
        y1_ref[...] = y

    @pl.when(i >= n_first)
    def _():
        y2_ref[...] = y


def _final_norm(x, ga, gb, rt, norm_g, n_first_rows):
    n, d = x.shape
    tm = ROW_TILE
    n_first = n_first_rows // tm
    row = lambda i: (i, 0)
    return pl.pallas_call(
        functools.partial(_final_kernel, n_first=n_first), grid=(n // tm,),
        in_specs=[pl.BlockSpec((tm, d), row), pl.BlockSpec((tm, d), row), pl.BlockSpec((tm, d), row),
                  pl.BlockSpec((tm, LANES), row), pl.BlockSpec((1, d), lambda i: (0, 0))],
        out_specs=[pl.BlockSpec((tm, d), lambda i: (jnp.minimum(i, n_first - 1), 0)),
                   pl.BlockSpec((tm, d), lambda i: (jnp.maximum(i - n_first, 0), 0))],
        out_shape=[jax.ShapeDtypeStruct((n_first_rows, d), F32), jax.ShapeDtypeStruct((n - n_first_rows, d), F32)],
        compiler_params=_cparams(("arbitrary",)), name="final_norm")(x, ga, gb, rt, norm_g.reshape(1, d))


def _out_proj_router_kernel(y_ref, x_ref, w_ref, g_ref, wr_ref, br_ref,
                            x1_ref, h_ref, rt_ref, cnt_ref, carry, *, n_first):
    i = pl.program_id(0)

    @pl.when(i == 0)
    def _():
        carry[...] = jnp.zeros_like(carry)

    x1 = x_ref[...] + _dot(y_ref[...], w_ref[...])
    x1_ref[...] = x1
    hn = _rms_rows(x1, g_ref[...], i < n_first)
    h_ref[...] = hn
    logits = _dot(hn, wr_ref[...]) + br_ref[0:1, :]
    tm = logits.shape[0]
    lane = lax.broadcasted_iota(jnp.int32, logits.shape, 1).astype(F32)
    neg = jnp.float32(-jnp.inf)
    far = jnp.float32(LANES)
    lg = jnp.where(lane < N_GROUPS, logits, neg)
    gmax = jnp.max(lg, axis=-1, keepdims=True)
    grp = jnp.min(jnp.where(lg == gmax, lane, far), axis=-1, keepdims=True)
    p_sel = 1.0 / jnp.sum(jnp.exp(lg - gmax), axis=-1, keepdims=True)
    lo = N_GROUPS + EXP_PER_GROUP * grp
    fm = jnp.where((lane >= lo) & (lane < lo + EXP_PER_GROUP), logits, neg)
    f1 = jnp.max(fm, axis=-1, keepdims=True)
    i1 = jnp.min(jnp.where(fm == f1, lane, far), axis=-1, keepdims=True)
    fm2 = jnp.where(lane == i1, neg, fm)
    f2 = jnp.max(fm2, axis=-1, keepdims=True)
    i2 = jnp.min(jnp.where(fm2 == f2, lane, far), axis=-1, keepdims=True)
    r = jnp.exp(f2 - f1)
    g1 = p_sel / (1.0 + r)
    g2 = p_sel * r / (1.0 + r)
    onehot = jnp.where((lane == i1) | (lane == i2), 1.0, 0.0)
    rr = lax.broadcasted_iota(jnp.int32, (tm, tm), 0)
    cc = lax.broadcasted_iota(jnp.int32, (tm, tm), 1)
    before = jnp.where(rr > cc, 1.0, 0.0).astype(BF16)
    cnt = jnp.dot(before, onehot.astype(BF16), preferred_element_type=F32) + carry[0:1, :]
    rank1 = jnp.sum(jnp.where(lane == i1, cnt, 0.0), axis=-1, keepdims=True)
    rank2 = jnp.sum(jnp.where(lane == i2, cnt, 0.0), axis=-1, keepdims=True)
    new_carry = carry[0:1, :] + jnp.sum(onehot, axis=0, keepdims=True)
    carry[...] = jnp.broadcast_to(new_carry, carry.shape)
    cnt_ref[...] = jnp.broadcast_to(new_carry, cnt_ref.shape)
    e1 = i1 - N_GROUPS
    e2 = i2 - N_GROUPS
    rt = jnp.where(lane == 0, e1, 0.0)
    rt = jnp.where(lane == 1, e2, rt)
    rt = jnp.where(lane == 2, rank1, rt)
    rt = jnp.where(lane == 3, rank2, rt)
    rt = jnp.where(lane == 4, g1, rt)
    rt = jnp.where(lane == 5, g2, rt)
    rt_ref[...] = rt


def _out_proj_router(y, x, w_out, norm_g, w_router, b_router, n_first_rows):
    n, d = x.shape
    tm = ROW_TILE
    row = lambda i: (i, 0)
    const = lambda i: (0, 0)
    return pl.pallas_call(
        functools.partial(_out_proj_router_kernel, n_first=n_first_rows // tm), grid=(n // tm,),
        in_specs=[pl.BlockSpec((tm, d), row), pl.BlockSpec((tm, d), row), pl.BlockSpec((d, d), const),
                  pl.BlockSpec((1, d), const), pl.BlockSpec((d, LANES), const), pl.BlockSpec((8, LANES), const)],
        out_specs=[pl.BlockSpec((tm, d), row), pl.BlockSpec((tm, d), row), pl.BlockSpec((tm, LANES), row),
                   pl.BlockSpec((8, LANES), const)],
        out_shape=[jax.ShapeDtypeStruct((n, d), F32), jax.ShapeDtypeStruct((n, d), F32),
                   jax.ShapeDtypeStruct((n, LANES), F32), jax.ShapeDtypeStruct((8, LANES), F32)],
        scratch_shapes=[pltpu.VMEM((8, LANES), F32)],
        compiler_params=_cparams(("arbitrary",)), name="out_proj_router",
    )(y, x, w_out.astype(BF16), norm_g.reshape(1, d), w_router.astype(BF16), b_router)


def _expert_kernel(be_ref, nv_ref, x_ref, wg_ref, wu_ref, wd_ref, o_ref, wg_s, wu_s, wd_s):
    i = pl.program_id(0)
    nv = nv_ref[i]

    @pl.when((i == 0) | (be_ref[i] != be_ref[jnp.maximum(i - 1, 0)]))
    def _():
        wg_s[...] = wg_ref[0, 0].astype(BF16)
        wu_s[...] = wu_ref[0, 0].astype(BF16)
        wd_s[...] = wd_ref[0, 0].astype(BF16)

    @pl.when(nv > 0)
    def _():
        rows = lax.broadcasted_iota(jnp.int32, (x_ref.shape[0], 1), 0)
        xb = jnp.where(rows < nv, x_ref[...], 0.0).astype(BF16)
        hg = jnp.dot(xb, wg_s[...], preferred_element_type=F32)
        hu = jnp.dot(xb, wu_s[...], preferred_element_type=F32)
        hb = (_silu(hg) * hu).astype(BF16)
        o_ref[...] = jnp.dot(hb, wd_s[...], preferred_element_type=F32)

    @pl.when(nv == 0)
    def _():
        o_ref[...] = jnp.zeros_like(o_ref)


def _experts(rows, block_exp, block_valid, w_gate, w_up, w_down, layer):
    r, d = rows.shape
    bm = MOE_ROWS
    nblk = r // bm
    grid_spec = pltpu.PrefetchScalarGridSpec(
        num_scalar_prefetch=2, grid=(nblk,),
        in_specs=[pl.BlockSpec((bm, d), lambda i, be, nv: (i, 0)),
                  pl.BlockSpec((1, 1, d, D_EXPERT), lambda i, be, nv: (layer, be[i], 0, 0)),
                  pl.BlockSpec((1, 1, d, D_EXPERT), lambda i, be, nv: (layer, be[i], 0, 0)),
                  pl.BlockSpec((1, 1, D_EXPERT, d), lambda i, be, nv: (layer, be[i], 0, 0))],
        out_specs=pl.BlockSpec((bm, d), lambda i, be, nv: (i, 0)),
        scratch_shapes=[pltpu.VMEM((d, D_EXPERT), BF16), pltpu.VMEM((d, D_EXPERT), BF16),
                        pltpu.VMEM((D_EXPERT, d), BF16)])
    return pl.pallas_call(
        _expert_kernel, grid_spec=grid_spec, out_shape=jax.ShapeDtypeStruct((r, d), F32),
        compiler_params=_cparams(("arbitrary",)), name="experts",
    )(block_exp, block_valid, rows, w_gate, w_up, w_down)


def _gather_rows(table, idx):
    info = plsc.get_sparse_core_info()
    n_workers = info.num_cores * info.num_subcores
    n_rows = idx.shape[0]
    width = table.shape[1]
    ch = SC_GATHER_ROWS
    per_worker = n_rows // n_workers
    n_chunks = per_worker // ch
    assert n_workers * n_chunks * ch == n_rows
    mesh = plsc.VectorSubcoreMesh(core_axis_name="c", subcore_axis_name="s")

    @functools.partial(
        pl.kernel, mesh=mesh, out_type=jax.ShapeDtypeStruct((n_rows, width), table.dtype),
        scratch_types=[pltpu.VMEM((ch,), jnp.int32), pltpu.VMEM((ch, width), table.dtype),
                       pltpu.SemaphoreType.DMA])
    def gather(table_hbm, idx_hbm, out_hbm, idx_v, rows_v, sem):
        wid = lax.axis_index("s") * info.num_cores + lax.axis_index("c")
        base = wid * per_worker

        @pl.loop(0, n_chunks)
        def _(j):
            off = pl.multiple_of(base + j * ch, ch)
            pltpu.sync_copy(idx_hbm.at[pl.ds(off, ch)], idx_v)
            pltpu.async_copy(table_hbm.at[idx_v], rows_v, sem).wait()
            pltpu.sync_copy(rows_v, out_hbm.at[pl.ds(off, ch)])

    return gather(table, idx)


def _pad_to(n, m):
    return -(-n // m) * m


def _moe(h, rt, counts_row, w_gate, w_up, w_down, layer):
    n, d = h.shape
    bm = MOE_ROWS
    sc_quant = 32 * SC_GATHER_ROWS
    counts = counts_row[N_GROUPS:N_GROUPS + N_EXPERTS].astype(jnp.int32)
    e = rt[:, 0:2].astype(jnp.int32)
    rank = rt[:, 2:4].astype(jnp.int32)
    padded = ((counts + bm - 1) // bm) * bm
    pend = jnp.cumsum(padded)
    pstart = pend - padded
    start = jnp.cumsum(counts) - counts
    n_rows = _pad_to(_pad_to(2 * n, bm) + N_EXPERTS * bm, sc_quant)
    n_rows = _pad_to(n_rows, bm)
    nblk = n_rows // bm
    blk_start = jnp.arange(nblk, dtype=jnp.int32) * bm
    block_exp = jnp.minimum(jnp.searchsorted(pend, blk_start, side="right"), N_EXPERTS - 1).astype(jnp.int32)
    block_valid = jnp.clip(counts[block_exp] - (blk_start - pstart[block_exp]), 0, bm).astype(jnp.int32)
    block_valid = jnp.where(blk_start < pend[-1], block_valid, 0)
    order = jnp.argsort(e.reshape(-1), stable=True).astype(jnp.int32)
    tok_sorted = order // 2
    r_idx = jnp.arange(n_rows, dtype=jnp.int32)
    r_exp = jnp.repeat(block_exp, bm)
    j = r_idx - pstart[r_exp]
    src = jnp.where(j < counts[r_exp], tok_sorted[jnp.clip(start[r_exp] + j, 0, 2 * n - 1)], 0)
    rows = _gather_rows(h, src)
    out_rows = _experts(rows, block_exp, block_valid, w_gate, w_up, w_down, layer)
    dest = pstart[e] + rank
    n_pad = _pad_to(n, sc_quant)
    dest = jnp.pad(dest, ((0, n_pad - n), (0, 0)))
    ga = _gather_rows(out_rows, dest[:, 0])
    gb = _gather_rows(out_rows, dest[:, 1])
    return ga, gb


def _chunk_masks(L):
    r = lax.broadcasted_iota(jnp.int32, (L, L), 0)
    c = lax.broadcasted_iota(jnp.int32, (L, L), 1)
    return r, c


def _row_from_col(col, r, c):
    return jnp.sum(jnp.where(r == c, col, 0.0), axis=0, keepdims=True)


def _cumsum_col_row(col, r, c):
    row = _row_from_col(col, r, c)
    cs_col = jnp.sum(jnp.where(r >= c, row, 0.0), axis=1, keepdims=True)
    cs_row = jnp.sum(jnp.where(r <= c, col, 0.0), axis=0, keepdims=True)
    return cs_col, cs_row


def _mlstm_kernel(q_ref, k_ref, v_ref, o_ref, gc_ref, c0_ref, n0_ref, m0_ref, na_ref,
                  y_ref, c1_ref, n1_ref, m1_ref, c_s, n_s, m_s, *, L, n_chunks, n_heads):
    h0 = pl.program_id(1) * n_heads
    t = pl.program_id(2)

    @pl.when(t == 0)
    def _():
        c_s[...] = c0_ref[0]
        n_s[...] = n0_ref[0]
        m_s[...] = m0_ref[0]

    r, c = _chunk_masks(L)
    causal = r >= c
    na = na_ref[...]

    heads = range(n_heads)
    cols = [slice(j * HEAD_DIM, (j + 1) * HEAD_DIM) for j in heads]
    rowsum = lambda x: jnp.sum(x, axis=-1, keepdims=True)

    def chunk(ci, carry):
        sl = pl.ds(pl.multiple_of(ci * L, L), L)
        gates = gc_ref[sl, :]
        q = [q_ref[sl, hs] for hs in cols]
        k = [k_ref[sl, hs] * HEAD_DIM ** -0.5 for hs in cols]
        v = [v_ref[sl, hs] for hs in cols]
        ig_col = [_lane_pick(gates, h0 + j) for j in heads]
        ig_row = [_row_from_col(x, r, c) for x in ig_col]
        bcr = [_cumsum_col_row(_lane_pick(gates, H_A + h0 + j), r, c) for j in heads]
        cmat = [c_s[j] for j in heads]
        nvec = [n_s[j] for j in heads]
        m_prev = [m_s[j, 0:1, 0:1] for j in heads]
        dmat = [jnp.where(causal, bc - br + ir, -jnp.inf) for (bc, br), ir in zip(bcr, ig_row)]
        inter = [bc + mp for (bc, _), mp in zip(bcr, m_prev)]
        m_t = [jnp.maximum(it, jnp.max(dm, axis=-1, keepdims=True)) for it, dm in zip(inter, dmat)]
        qk = [_dot_nt(a, b) for a, b in zip(q, k)]
        qc = [_dot(a, b) for a, b in zip(q, cmat)]
        s = [x * jnp.exp(dm - mt) for x, dm, mt in zip(qk, dmat, m_t)]
        w_inter = [jnp.exp(it - mt) for it, mt in zip(inter, m_t)]
        sv = [_dot(a, b) for a, b in zip(s, v)]
        num = [wi * a + b for wi, a, b in zip(w_inter, qc, sv)]
        qn = [rowsum(_round_bf16(a) * _round_bf16(b)) for a, b in zip(q, nvec)]
        den = [wi * a + rowsum(x) for wi, a, x in zip(w_inter, qn, s)]
        hout = [a / jnp.maximum(jnp.abs(b), jnp.exp(-mt)) for a, b, mt in zip(num, den, m_t)]
        m_new = [mt[L - 1:L, :] for mt in m_t]
        b_last = [bc[L - 1:L, :] for bc, _ in bcr]
        wk = [jnp.exp(bl - bc + ic - mn) for bl, (bc, _), ic, mn in zip(b_last, bcr, ig_col, m_new)]
        dec = [jnp.exp(bl + mp - mn) for bl, mp, mn in zip(b_last, m_prev, m_new)]
        kv = [_dot_tn(a * w, b) for a, w, b in zip(k, wk, v)]
        for j in heads:
            c_s[j] = dec[j] * cmat[j] + kv[j]
            n_s[j] = dec[j] * nvec[j] + jnp.sum(_round_bf16(wk[j]) * _round_bf16(k[j]), axis=0,
                                                keepdims=True)
            m_s[j] = jnp.broadcast_to(m_new[j], (1, HEAD_DIM))
            y_ref[sl, cols[j]] = _rms(hout[j], na) * _sigmoid(o_ref[sl, cols[j]])
        return carry

    lax.fori_loop(0, n_chunks, chunk, 0)

    @pl.when(t == pl.num_programs(2) - 1)
    def _():
        c1_ref[0] = c_s[...]
        n1_ref[0] = n_s[...]
        m1_ref[0] = m_s[...]


def _retention_kernel(q_ref, k_ref, v_ref, g_ref, cos_ref, sin_ref, s0_ref, nb_ref, lg_ref,
                      y_ref, s1_ref, s_s, *, L, n_chunks, n_heads):
    t = pl.program_id(2)

    @pl.when(t == 0)
    def _():
        s_s[...] = s0_ref[0]

    r, c = _chunk_masks(L)
    causal = r >= c
    rel = (r - c).astype(F32)
    idx = lax.broadcasted_iota(jnp.int32, (L, 1), 0).astype(F32)
    nb = nb_ref[...]
    half = HEAD_DIM // 2
    decays = []
    for j in range(n_heads):
        lg = lg_ref[j, 0:1, 0:1]
        decays.append((jnp.where(causal, jnp.exp(jnp.where(causal, rel, 0.0) * lg), 0.0),
                       jnp.exp((idx + 1.0) * lg), jnp.exp((L - 1.0 - idx) * lg), jnp.exp(L * lg)))

    def rot(x, cos, sin):
        return x * cos + pltpu.roll(x, half, 1) * sin

    def chunk(ci, carry):
        sl = pl.ds(pl.multiple_of(ci * L, L), L)
        cos = cos_ref[sl, :]
        sin = sin_ref[sl, :]
        heads = range(n_heads)
        cols = [slice(j * HEAD_DIM, (j + 1) * HEAD_DIM) for j in heads]
        q = [rot(q_ref[sl, hs], cos, sin) for hs in cols]
        k = [rot(k_ref[sl, hs], cos, sin) * HEAD_DIM ** -0.5 for hs in cols]
        v = [v_ref[sl, hs] for hs in cols]
        smat = [s_s[j] for j in heads]
        qk = [_dot_nt(a, b) for a, b in zip(q, k)]
        qs = [_dot(a, b) for a, b in zip(q, smat)]
        kv = [_dot_tn(a * decays[j][2], b) for j, (a, b) in enumerate(zip(k, v))]
        av = [_dot(x * decays[j][0], b) for j, (x, b) in enumerate(zip(qk, v))]
        for j in heads:
            o = av[j] + decays[j][1] * qs[j]
            s_s[j] = decays[j][3] * smat[j] + kv[j]
            y_ref[sl, cols[j]] = _rms(o, nb) * _silu(g_ref[sl, cols[j]])
        return carry

    lax.fori_loop(0, n_chunks, chunk, 0)

    @pl.when(t == pl.num_programs(2) - 1)
    def _():
        s1_ref[0] = s_s[...]


def _unit_lower_inverses(mats, r, c):
    L = mats[0].shape[0]
    eye = jnp.where(r == c, 1.0, 0.0)
    shift = 4
    same = (r >> shift) == (c >> shift)
    ps = [-jnp.where(same, a, 0.0) for a in mats]
    invs = [eye + p for p in ps]
    for _ in range(3):
        sp = [_split2(p) for p in ps]
        ps = [_dot_x3(s, s) for s in sp]
        sp = [_split2(p) for p in ps]
        si = [_split2(v) for v in invs]
        invs = [v + _dot_x3(a, b) for v, a, b in zip(invs, si, sp)]
    while (1 << shift) < L:
        shift += 1
        same2 = (r >> shift) == (c >> shift)
        offs = [_split2(jnp.where(same2 & jnp.logical_not(same), a, 0.0)) for a in mats]
        si = [_split2(v) for v in invs]
        mid = [_split2(_dot_x3(o, s)) for o, s in zip(offs, si)]
        invs = [v - _dot_x3(s, m) for v, s, m in zip(invs, si, mid)]
        same = same2
    return invs


def _gdn_kernel(q_ref, k_ref, v_ref, gt_ref, gc_ref, cwq_ref, cwk_ref, cwv_ref, hq_ref, hk_ref, hv_ref,
                s0_ref, nc_ref, y_ref, s1_ref, s_s, eq_s, ek_s, ev_s, qn_s, kn_s, vn_s, nq_s, b_s, o0_s,
                *, L, n_chunks):
    h = pl.program_id(1)
    t = pl.program_id(2)
    tb = q_ref.shape[0]
    hist = GDN_HIST_ROWS

    @pl.when(t == 0)
    def _():
        s_s[...] = s0_ref[0, 0]
        eq_s[0:hist, :] = _round_bf16(hq_ref[0])
        ek_s[0:hist, :] = _round_bf16(hk_ref[0])
        ev_s[0:hist, :] = _round_bf16(hv_ref[0])

    def conv_silu(x_ref, e_s, cw_ref):
        e_s[hist:hist + tb, :] = _round_bf16(x_ref[...])
        cw = _round_bf16(cw_ref[...])
        acc = e_s[hist - (CONV_W - 1):hist - (CONV_W - 1) + tb, :] * cw[0:1, :]
        for tap in range(1, CONV_W):
            lo = hist - (CONV_W - 1) + tap
            acc = acc + e_s[lo:lo + tb, :] * cw[tap:tap + 1, :]
        e_s[0:hist, :] = e_s[tb:tb + hist, :]
        return _silu(acc)

    def l2n(x):
        return x * lax.rsqrt(jnp.sum(x * x, axis=-1, keepdims=True) + RMS_EPS)

    qn_s[...] = l2n(conv_silu(q_ref, eq_s, cwq_ref)) * HEAD_DIM ** -0.5
    kn_s[...] = l2n(conv_silu(k_ref, ek_s, cwk_ref))
    vn_s[...] = conv_silu(v_ref, ev_s, cwv_ref)

    r, c = _chunk_masks(L)
    lower = r >= c
    strict = r > c
    ncw = nc_ref[...]
    hd = HEAD_DIM

    chunks = range(n_chunks)
    rows = [slice(ci * L, (ci + 1) * L) for ci in chunks]
    qs = [qn_s[sl, :] for sl in rows]
    ks = [kn_s[sl, :] for sl in rows]
    vs = [vn_s[sl, :] for sl in rows]
    gates = [gc_ref[sl, :] for sl in rows]
    betas = [_lane_pick(g, h) for g in gates]
    cums = [_cumsum_col_row(_lane_pick(g, H_C + h), r, c) for g in gates]
    gams = [jnp.where(lower, jnp.exp(jnp.where(lower, gc - gr, 0.0)), 0.0) for gc, gr in cums]
    kqs = [_dot_nt(jnp.concatenate([k, q], axis=0), k) for k, q in zip(ks, qs)]
    tinvs = _unit_lower_inverses(
        [jnp.where(strict, b * kq[0:L, :] * gam, 0.0) for b, kq, gam in zip(betas, kqs, gams)], r, c)
    egs = [jnp.exp(gc) for gc, _ in cums]
    rhss = [_split2(jnp.concatenate([v * b, k * (b * eg)], axis=1)) for v, k, b, eg in zip(vs, ks, betas, egs)]
    uws = [_dot_x3(_split2(ti), rh) for ti, rh in zip(tinvs, rhss)]
    g_lasts = [gc[L - 1:L, :] for gc, _ in cums]
    bns = [_dot_tn(k * jnp.exp(gl - gc), uw) for k, gl, (gc, _), uw in zip(ks, g_lasts, cums, uws)]
    aos = [_dot(kq[L:2 * L, :] * gam, uw) for kq, gam, uw in zip(kqs, gams, uws)]
    for ci in chunks:
        nq_s[ci, 0:hd, :] = bns[ci][:, hd:2 * hd]
        nq_s[ci, hd:hd + L, :] = qs[ci] * egs[ci] - aos[ci][:, hd:2 * hd]
        b_s[ci] = bns[ci][:, 0:hd]
        o0_s[ci] = aos[ci][:, 0:hd]
    decays = [jnp.exp(gl) for gl in g_lasts]

    smat = s_s[...]
    for ci in range(n_chunks):
        sl = slice(ci * L, (ci + 1) * L)
        ns = _dot(nq_s[ci], smat)
        o = ns[hd:hd + L, :] + o0_s[ci]
        y_ref[sl, :] = _rms(o, ncw) * _silu(gt_ref[sl, :])
        smat = decays[ci] * smat - ns[0:hd, :] + b_s[ci]
    s_s[...] = smat

    @pl.when(t == pl.num_programs(2) - 1)
    def _():
        s1_ref[0, 0] = s_s[...]


def _scan_geometry(t_len):
    tb = min(SCAN_BLOCK, t_len)
    L = min(SCAN_CHUNK, t_len)
    return tb, L, t_len // tb, tb // L


def _col_spec(tb, nt, row0_blocks, col_fn, nh=1):
    return pl.BlockSpec((tb, nh * HEAD_DIM), lambda b, g, t: (row0_blocks + b * nt + t, col_fn(g)))


def _state_spec(shape_tail, nh=1):
    nd = len(shape_tail)
    return pl.BlockSpec((1, nh) + shape_tail, lambda b, g, t: (b, g) + (0,) * nd)


def _mlstm_scan(z, gc, y_prev, row0, bsz, t_len, c0, n0, m0, norm_a):
    tb, L, nt, n_chunks = _scan_geometry(t_len)
    rb = row0 // tb
    hb = HEAD_DIM
    nh = SCAN_HEADS
    ng = H_A // nh
    in_specs = [_col_spec(tb, nt, rb, lambda g: g, nh), _col_spec(tb, nt, rb, lambda g: ng + g, nh),
                _col_spec(tb, nt, rb, lambda g: 2 * ng + g, nh), _col_spec(tb, nt, rb, lambda g: 3 * ng + g, nh),
                pl.BlockSpec((tb, GATE_LANES), lambda b, g, t: (rb + b * nt + t, 0)),
                _state_spec((hb, hb), nh), _state_spec((1, hb), nh), _state_spec((1, hb), nh),
                pl.BlockSpec((1, hb), lambda b, g, t: (0, 0))]
    out_specs = [_col_spec(tb, nt, rb, lambda g: g, nh),
                 _state_spec((hb, hb), nh), _state_spec((1, hb), nh), _state_spec((1, hb), nh)]
    out_shape = [jax.ShapeDtypeStruct(y_prev.shape, F32),
                 jax.ShapeDtypeStruct((bsz, H_A, hb, hb), F32), jax.ShapeDtypeStruct((bsz, H_A, 1, hb), F32),
                 jax.ShapeDtypeStruct((bsz, H_A, 1, hb), F32)]
    m0b = jnp.broadcast_to(m0[:, :, None, None], (bsz, H_A, 1, hb))
    kern = functools.partial(_mlstm_kernel_aliased, L=L, n_chunks=n_chunks, n_heads=nh)
    y, c1, n1, m1 = pl.pallas_call(
        kern, grid=(bsz, ng, nt), in_specs=in_specs + [pl.BlockSpec(memory_space=pl.ANY)],
        out_specs=out_specs, out_shape=out_shape,
        scratch_shapes=[pltpu.VMEM((nh, hb, hb), F32), pltpu.VMEM((nh, 1, hb), F32), pltpu.VMEM((nh, 1, hb), F32)],
        input_output_aliases={9: 0},
        compiler_params=_cparams(("parallel", "parallel", "arbitrary")), name="mlstm_scan",
    )(z, z, z, z, gc, c0, n0[:, :, None, :], m0b, norm_a.reshape(1, hb), y_prev)
    return y, c1, n1[:, :, 0, :], m1[:, :, 0, 0]


def _mlstm_kernel_aliased(*refs, **kw):
    return _mlstm_kernel(*refs[:9], *refs[10:], **kw)


def _retention_scan(z, cos, sin, y_prev, row0, bsz, t_len, s0, norm_b):
    tb, L, nt, n_chunks = _scan_geometry(t_len)
    rb = row0 // tb
    hb = HEAD_DIM
    nh = SCAN_HEADS
    ng = H_B // nh
    base = 4 * (H_A // nh)
    in_specs = [_col_spec(tb, nt, rb, lambda g: base + g, nh), _col_spec(tb, nt, rb, lambda g: base + ng + g, nh),
                _col_spec(tb, nt, rb, lambda g: base + 2 * ng + g, nh),
                _col_spec(tb, nt, rb, lambda g: base + 3 * ng + g, nh),
                pl.BlockSpec((tb, hb), lambda b, g, t: (t, 0)), pl.BlockSpec((tb, hb), lambda b, g, t: (t, 0)),
                _state_spec((hb, hb), nh), pl.BlockSpec((1, hb), lambda b, g, t: (0, 0)),
                pl.BlockSpec((nh, 1, hb), lambda b, g, t: (g, 0, 0)),
                pl.BlockSpec(memory_space=pl.ANY)]
    out_specs = [_col_spec(tb, nt, rb, lambda g: H_A // nh + g, nh), _state_spec((hb, hb), nh)]
    out_shape = [jax.ShapeDtypeStruct(y_prev.shape, F32), jax.ShapeDtypeStruct((bsz, H_B, hb, hb), F32)]
    log_gamma = jnp.log(1.0 - 2.0 ** (-5.0 - jnp.arange(H_B, dtype=F32)))
    lg_tab = jnp.broadcast_to(log_gamma[:, None, None], (H_B, 1, hb))
    kern = functools.partial(_retention_kernel_aliased, L=L, n_chunks=n_chunks, n_heads=nh)
    return pl.pallas_call(
        kern, grid=(bsz, ng, nt), in_specs=in_specs, out_specs=out_specs, out_shape=out_shape,
        scratch_shapes=[pltpu.VMEM((nh, hb, hb), F32)], input_output_aliases={9: 0},
        compiler_params=_cparams(("parallel", "parallel", "arbitrary")), name="retention_scan",
    )(z, z, z, z, cos, sin, s0, norm_b.reshape(1, hb), lg_tab, y_prev)


def _retention_kernel_aliased(*refs, **kw):
    return _retention_kernel(*refs[:9], *refs[10:], **kw)


def _gdn_scan(z, gc, y_prev, row0, bsz, t_len, conv_w, conv_hist, s0, norm_c):
    tb, L, nt, n_chunks = _scan_geometry(t_len)
    rb = row0 // tb
    hb = HEAD_DIM
    cw_spec = lambda off: pl.BlockSpec((CONV_W, hb), lambda b, h, t: (0, off + h))
    hist_spec = lambda off: pl.BlockSpec((1, GDN_HIST_ROWS, hb), lambda b, h, t: (b, 0, off + h))
    in_specs = [_col_spec(tb, nt, rb, lambda h: h), _col_spec(tb, nt, rb, lambda h: H_C + h),
                _col_spec(tb, nt, rb, lambda h: 2 * H_C + h), _col_spec(tb, nt, rb, lambda h: 3 * H_C + h),
                pl.BlockSpec((tb, GATE_LANES), lambda b, h, t: (rb + b * nt + t, 0)),
                cw_spec(0), cw_spec(H_C), cw_spec(2 * H_C),
                hist_spec(0), hist_spec(H_C), hist_spec(2 * H_C),
                _state_spec((hb, hb)), pl.BlockSpec((1, hb), lambda b, h, t: (0, 0)),
                pl.BlockSpec(memory_space=pl.ANY)]
    out_specs = [_col_spec(tb, nt, rb, lambda h: h), _state_spec((hb, hb))]
    out_shape = [jax.ShapeDtypeStruct(y_prev.shape, F32), jax.ShapeDtypeStruct((bsz, H_C, hb, hb), F32)]
    kern = functools.partial(_gdn_kernel_aliased, L=L, n_chunks=n_chunks)
    ext = pltpu.VMEM((tb + GDN_HIST_ROWS, hb), F32)
    blk = pltpu.VMEM((tb, hb), F32)
    return pl.pallas_call(
        kern, grid=(bsz, H_C, nt), in_specs=in_specs, out_specs=out_specs, out_shape=out_shape,
        scratch_shapes=[pltpu.VMEM((hb, hb), F32), ext, ext, ext, blk, blk, blk,
                        pltpu.VMEM((n_chunks, hb + L, hb), F32), pltpu.VMEM((n_chunks, hb, hb), F32),
                        pltpu.VMEM((n_chunks, L, hb), F32)],
        input_output_aliases={13: 0},
        compiler_params=_cparams(("parallel", "parallel", "arbitrary")), name="gdn_scan",
    )(z, z, z, z, gc, conv_w, conv_w, conv_w, conv_hist, conv_hist, conv_hist, s0, norm_c.reshape(1, hb), y_prev)


def _gdn_kernel_aliased(*refs, **kw):
    return _gdn_kernel(*refs[:13], *refs[14:], **kw)


def _rope_tables(pos):
    half = HEAD_DIM // 2
    inv = ROPE_BASE ** (-jnp.arange(half, dtype=F32) / half)
    ang = pos.astype(F32)[:, None] * inv[None, :]
    cos, sin = jnp.cos(ang), jnp.sin(ang)
    return jnp.concatenate([cos, cos], axis=-1), jnp.concatenate([-sin, sin], axis=-1)


def _gate_rows(*rows):
    out = jnp.zeros((8, GATE_LANES), F32)
    for i, rvals in enumerate(rows):
        out = out.at[i, :rvals.shape[0]].set(rvals.astype(F32))
    return out


def _pad_cols(w, width):
    return jnp.pad(w, ((0, 0), (0, width - w.shape[1])))


def kernel(x_prompt, x_sample, state_mlstm_c, state_mlstm_n, state_mlstm_m, state_ret, state_gdn, state_gdn_conv, norm_mix, norm_ffn, norm_final, ab_w_in, ab_gate_bias, ab_w_out, ab_norm_a, ab_norm_b, c_w_in, c_conv_w, c_a_log, c_dt_bias, c_norm, c_w_out, moe_w_grp, moe_b_grp, moe_w_exp, moe_b_exp, moe_w_gate, moe_w_up, moe_w_down):
    bp, tp, d = x_prompt.shape
    bs, ts, _ = x_sample.shape
    n_p, n_s = bp * tp, bs * ts
    n = n_p + n_s
    x = jnp.concatenate([x_prompt.reshape(n_p, d), x_sample.reshape(n_s, d)], axis=0)
    hd = HEAD_DIM
    qkv_a = 4 * H_A * hd
    n_gate_a = 2 * H_A
    c_qkv = 3 * H_C * hd
    n_gate_c = 2 * H_C

    def moe_layer(y, xres, w_out, layer):
        w_router = _pad_cols(jnp.concatenate([moe_w_grp[layer], moe_w_exp[layer]], axis=1), LANES)
        b_router = _gate_rows(jnp.concatenate([moe_b_grp[layer], moe_b_exp[layer]]))
        x1, hmoe, rt, cnt = _out_proj_router(y, xres, w_out, norm_ffn[layer], w_router, b_router, n_p)
        ga, gb = _moe(hmoe, rt, cnt[0], moe_w_gate, moe_w_up, moe_w_down, layer)
        return x1, ga, gb, rt

    w_in = ab_w_in[0]
    w_main = jnp.concatenate([w_in[:, :qkv_a], w_in[:, qkv_a + n_gate_a:]], axis=1)
    w_gate = _pad_cols(w_in[:, qkv_a:qkv_a + n_gate_a], GATE_LANES)
    z, gc = _in_proj(x, norm_mix[0], w_main, w_gate, _gate_rows(ab_gate_bias[0]), 0, n_p)
    y = jnp.zeros((n, d), F32)
    zeros = lambda *s: jnp.zeros(s, F32)
    cos_p, sin_p = _rope_tables(jnp.arange(tp))
    cos_s, sin_s = _rope_tables(PAST_LEN + jnp.arange(ts))
    y, c_p, n_pm, m_p = _mlstm_scan(z, gc, y, 0, bp, tp, zeros(bp, H_A, hd, hd), zeros(bp, H_A, hd),
                                    zeros(bp, H_A), ab_norm_a[0])
    y, c_s, n_sm, m_s = _mlstm_scan(z, gc, y, n_p, bs, ts, state_mlstm_c[0], state_mlstm_n[0],
                                    state_mlstm_m[0], ab_norm_a[0])
    y, r_p = _retention_scan(z, cos_p, sin_p, y, 0, bp, tp, zeros(bp, H_B, hd, hd), ab_norm_b[0])
    y, r_s = _retention_scan(z, cos_s, sin_s, y, n_p, bs, ts, state_ret[0], ab_norm_b[0])
    x1, ga, gb, rt = moe_layer(y, x, ab_w_out[0], 0)

    w_in = c_w_in[0]
    w_main = jnp.concatenate([w_in[:, :c_qkv], w_in[:, c_qkv + n_gate_c:]], axis=1)
    w_gate = _pad_cols(w_in[:, c_qkv:c_qkv + n_gate_c], GATE_LANES)
    zero8 = jnp.zeros((H_C,), F32)
    gparams = _gate_rows(jnp.concatenate([zero8, c_dt_bias[0]]), jnp.concatenate([zero8, c_a_log[0]]))
    x2, z, gc = _in_proj(x1, norm_mix[1], w_main, w_gate, gparams, 1, n_p, comb=(ga, gb, rt))
    hist_p = jnp.zeros((bp, GDN_HIST_ROWS, c_qkv), F32)
    hist_s = jnp.pad(state_gdn_conv[0], ((0, 0), (GDN_HIST_ROWS - (CONV_W - 1), 0), (0, 0)))
    y = jnp.zeros((n, d), F32)
    y, g_p = _gdn_scan(z, gc, y, 0, bp, tp, c_conv_w[0], hist_p, zeros(bp, H_C, hd, hd), c_norm[0])
    y, g_s = _gdn_scan(z, gc, y, n_p, bs, ts, c_conv_w[0], hist_s, state_gdn[0], c_norm[0])
    keep = CONV_W - 1
    last_rows = lambda row0, b, t: (row0 + np.arange(b)[:, None] * t + np.arange(t - keep, t)[None, :]).reshape(-1)
    conv_p = jnp.take(z, last_rows(0, bp, tp), axis=0)[:, :c_qkv].reshape(bp, keep, c_qkv)
    conv_s = jnp.take(z, last_rows(n_p, bs, ts), axis=0)[:, :c_qkv].reshape(bs, keep, c_qkv)
    x3, ga, gb, rt = moe_layer(y, x2, c_w_out[0], 1)

    yf_p, yf_s = _final_norm(x3, ga, gb, rt, norm_final, n_p)
    y_prompt = yf_p.reshape(bp, tp, d)
    y_sample = yf_s.reshape(bs, ts, d)
    return (y_prompt, y_sample, c_p[None], n_pm[None], m_p[None], r_p[None], g_p[None], conv_p[None],
            c_s[None], n_sm[None], m_s[None], r_s[None], g_s[None], conv_s[None])
```

```python
import functools
import math

import jax
import jax.numpy as jnp
import numpy as np
from jax import lax
from jax.experimental import pallas as pl
from jax.experimental.pallas import tpu as pltpu
from jax.experimental.pallas import tpu_sc as plsc

F32 = jnp.float32
BF16 = jnp.bfloat16

D_MODEL = 1024
H_A = 4
H_B = 4
H_C = 8
HEAD_DIM = 128
CONV_W = 4
N_GROUPS = 4
EXP_PER_GROUP = 8
N_EXPERTS = N_GROUPS * EXP_PER_GROUP
D_EXPERT = 512
RMS_EPS = 1e-6
ROPE_BASE = 10000.0
PAST_LEN = 2048

LANES = 128
GATE_LANES = LANES
VMEM_LIMIT = 56 * 1024 * 1024

ROW_TILE = 512
SCAN_BLOCK = 512
SCAN_CHUNK = 64
SCAN_HEADS = 4
MOE_ROWS = 512
SC_GATHER_ROWS = 64
GDN_HIST_ROWS = 8


def _cparams(sem):
    return pltpu.CompilerParams(dimension_semantics=sem, vmem_limit_bytes=VMEM_LIMIT)


def _dot(a, b):
    return jnp.dot(a.astype(BF16), b.astype(BF16), preferred_element_type=F32)


def _dot_nt(a, b):
    return lax.dot_general(a.astype(BF16), b.astype(BF16), (((1,), (1,)), ((), ())),
                           preferred_element_type=F32)


def _dot_tn(a, b):
    return lax.dot_general(a.astype(BF16), b.astype(BF16), (((0,), (0,)), ((), ())),
                           preferred_element_type=F32)


def _pack_bf16_pairs(x):
    w = x.shape[1] // 2
    bits = lax.bitcast_convert_type(_round_bf16(x), jnp.uint32)
    return lax.bitcast_convert_type((bits[:, :w] >> 16) | bits[:, w:], jnp.int32)


def _unpack_bf16_pairs(p):
    bits = lax.bitcast_convert_type(p, jnp.uint32)
    lo = lax.bitcast_convert_type(bits << 16, F32)
    hi = lax.bitcast_convert_type(bits & jnp.uint32(0xFFFF0000), F32)
    return jnp.concatenate([lo, hi], axis=1)


def _round_bf16(x):
    return x.astype(BF16).astype(F32)


def _softplus(x):
    return jnp.maximum(x, 0.0) + jnp.log1p(jnp.exp(-jnp.abs(x)))


def _sigmoid(x):
    return 1.0 / (1.0 + jnp.exp(-x))


def _silu(x):
    return x * _sigmoid(x)


def _rms(x, g):
    return x * lax.rsqrt(jnp.mean(x * x, axis=-1, keepdims=True) + RMS_EPS) * g


def _mean_sq_sublane_order(x):
    rows, d = x.shape
    acc = None
    for c in range(d // LANES):
        xc = x[:, c * LANES:(c + 1) * LANES]
        acc = xc * xc if acc is None else acc + xc * xc
    acc_t = acc.T
    s8 = acc_t[0:8, :]
    for a in range(1, LANES // 8):
        s8 = s8 + acc_t[8 * a:8 * a + 8, :]
    ms_row = jnp.sum(s8, axis=0, keepdims=True) * (1.0 / d)
    r = lax.broadcasted_iota(jnp.int32, (rows, rows), 0)
    c = lax.broadcasted_iota(jnp.int32, (rows, rows), 1)
    return jnp.sum(jnp.where(r == c, ms_row, 0.0), axis=1, keepdims=True)


def _rms_rows(x, g, sublane_order):
    ms = jnp.where(sublane_order, _mean_sq_sublane_order(x), jnp.mean(x * x, axis=-1, keepdims=True))
    return x * lax.rsqrt(ms + RMS_EPS) * g


def _lane_pick(tile, idx):
    lane = lax.broadcasted_iota(jnp.int32, tile.shape, 1)
    return jnp.sum(jnp.where(lane == idx, tile, 0.0), axis=-1, keepdims=True)


def _combine(x_ref, ga_ref, gb_ref, rt_ref):
    rt = rt_ref[...]
    return x_ref[...] + (_lane_pick(rt, 4) * ga_ref[...] + _lane_pick(rt, 5) * gb_ref[...])


def _in_proj_kernel(*refs, combine, gate_mode, n_col_chunks, col_chunk, n_first):
    first = pl.program_id(0) < n_first
    if combine:
        x_ref, ga_ref, gb_ref, rt_ref = refs[:4]
        refs = refs[4:]
        x = _combine(x_ref, ga_ref, gb_ref, rt_ref)
    else:
        x_ref = refs[0]
        refs = refs[1:]
        x = x_ref[...]
    g_ref, w_ref, wg_ref, gp_ref = refs[:4]
    outs = refs[4:]
    if combine:
        xo_ref, z_ref, gc_ref = outs
        xo_ref[...] = x
    else:
        z_ref, gc_ref = outs
    xn = _rms_rows(x, g_ref[...], first)
    xh = xn.astype(BF16)
    for c in range(n_col_chunks):
        cs = slice(c * col_chunk, (c + 1) * col_chunk)
        z_ref[:, cs] = jnp.dot(xh, w_ref[:, cs], preferred_element_type=F32)
    raw = jnp.dot(xh, wg_ref[...], preferred_element_type=F32)
    lane = lax.broadcasted_iota(jnp.int32, raw.shape, 1)
    bias = gp_ref[0:1, :]
    if gate_mode == 0:
        val = raw + bias
        gc_ref[...] = jnp.where(lane < H_A, val, -_softplus(-val))
    else:
        neg_a = -jnp.exp(gp_ref[1:2, :])
        gc_ref[...] = jnp.where(lane < H_C, _sigmoid(raw), neg_a * _softplus(raw + bias))


def _in_proj(x, norm_g, w_main, w_gate, gate_params, gate_mode, n_first_rows, comb=None):
    n, d = x.shape
    nz = w_main.shape[1]
    tm = ROW_TILE
    col_chunk = 512
    row = lambda i: (i, 0)
    const = lambda i: (0, 0)
    in_specs = [pl.BlockSpec((tm, d), row)]
    args = [x]
    out_specs = []
    out_shape = []
    once = dict(pipeline_mode=pl.Buffered(1))
    if comb is not None:
        gab, second_row0, rt = comb
        second = second_row0 // tm
        in_specs += [pl.BlockSpec((tm, d), row), pl.BlockSpec((tm, d), lambda i: (second + i, 0)),
                     pl.BlockSpec((tm, LANES), row)]
        args += [gab, gab, rt]
        out_specs.append(pl.BlockSpec((tm, d), row))
        out_shape.append(jax.ShapeDtypeStruct((n, d), F32))
    in_specs += [pl.BlockSpec((1, d), const), pl.BlockSpec((d, nz), const, **once),
                 pl.BlockSpec((d, GATE_LANES), const, **once), pl.BlockSpec((8, GATE_LANES), const)]
    args += [norm_g.reshape(1, d), w_main.astype(BF16), w_gate.astype(BF16), gate_params]
    out_specs += [pl.BlockSpec((tm, nz), row), pl.BlockSpec((tm, GATE_LANES), row)]
    out_shape += [jax.ShapeDtypeStruct((n, nz), F32), jax.ShapeDtypeStruct((n, GATE_LANES), F32)]
    kern = functools.partial(_in_proj_kernel, combine=comb is not None, gate_mode=gate_mode,
                             n_col_chunks=nz // col_chunk, col_chunk=col_chunk, n_first=n_first_rows // tm)
    return pl.pallas_call(
        kern, grid=(n // tm,), in_specs=in_specs, out_specs=out_specs, out_shape=out_shape,
        compiler_params=_cparams(("parallel",)), name=f"in_proj_{gate_mode}")(*args)


def _final_kernel(x_ref, ga_ref, gb_ref, rt_ref, g_ref, y1_ref, y2_ref, *, n_first):
    i = pl.program_id(0)
    y = _rms_rows(_combine(x_ref, ga_ref, gb_ref, rt_ref), g_ref[...], i < n_first)

    @pl.when(i < n_first)
    def _():
        y1_ref[...] = y

    @pl.when(i >= n_first)
    def _():
        y2_ref[...] = y


def _final_norm(x, gab, second_row0, rt, norm_g, n_first_rows):
    n, d = x.shape
    tm = ROW_TILE
    n_first = n_first_rows // tm
    second = second_row0 // tm
    row = lambda i: (i, 0)
    return pl.pallas_call(
        functools.partial(_final_kernel, n_first=n_first), grid=(n // tm,),
        in_specs=[pl.BlockSpec((tm, d), row), pl.BlockSpec((tm, d), row),
                  pl.BlockSpec((tm, d), lambda i: (second + i, 0)),
                  pl.BlockSpec((tm, LANES), row), pl.BlockSpec((1, d), lambda i: (0, 0))],
        out_specs=[pl.BlockSpec((tm, d), lambda i: (jnp.minimum(i, n_first - 1), 0)),
                   pl.BlockSpec((tm, d), lambda i: (jnp.maximum(i - n_first, 0), 0))],
        out_shape=[jax.ShapeDtypeStruct((n_first_rows, d), F32), jax.ShapeDtypeStruct((n - n_first_rows, d), F32)],
        compiler_params=_cparams(("arbitrary",)), name="final_norm")(x, gab, gab, rt, norm_g.reshape(1, d))


def _out_proj_router_kernel(y_ref, x_ref, w_ref, g_ref, wr_ref, br_ref,
                            x1_ref, h_ref, rt_ref, cnt_ref, carry, *, n_first):
    i = pl.program_id(0)

    @pl.when(i == 0)
    def _():
        carry[...] = jnp.zeros_like(carry)

    x1 = x_ref[...] + _dot(y_ref[...], w_ref[...])
    x1_ref[...] = x1
    hn = _rms_rows(x1, g_ref[...], i < n_first)
    h_ref[...] = _pack_bf16_pairs(hn)
    logits = _dot(hn, wr_ref[...]) + br_ref[0:1, :]
    tm = logits.shape[0]
    lane = lax.broadcasted_iota(jnp.int32, logits.shape, 1).astype(F32)
    neg = jnp.float32(-jnp.inf)
    far = jnp.float32(LANES)
    lg = jnp.where(lane < N_GROUPS, logits, neg)
    gmax = jnp.max(lg, axis=-1, keepdims=True)
    grp = jnp.min(jnp.where(lg == gmax, lane, far), axis=-1, keepdims=True)
    p_sel = 1.0 / jnp.sum(jnp.exp(lg - gmax), axis=-1, keepdims=True)
    lo = N_GROUPS + EXP_PER_GROUP * grp
    fm = jnp.where((lane >= lo) & (lane < lo + EXP_PER_GROUP), logits, neg)
    f1 = jnp.max(fm, axis=-1, keepdims=True)
    i1 = jnp.min(jnp.where(fm == f1, lane, far), axis=-1, keepdims=True)
    fm2 = jnp.where(lane == i1, neg, fm)
    f2 = jnp.max(fm2, axis=-1, keepdims=True)
    i2 = jnp.min(jnp.where(fm2 == f2, lane, far), axis=-1, keepdims=True)
    r = jnp.exp(f2 - f1)
    g1 = p_sel / (1.0 + r)
    g2 = p_sel * r / (1.0 + r)
    onehot = jnp.where((lane == i1) | (lane == i2), 1.0, 0.0)
    rr = lax.broadcasted_iota(jnp.int32, (tm, tm), 0)
    cc = lax.broadcasted_iota(jnp.int32, (tm, tm), 1)
    before = jnp.where(rr > cc, 1.0, 0.0).astype(BF16)
    cnt = jnp.dot(before, onehot.astype(BF16), preferred_element_type=F32) + carry[0:1, :]
    rank1 = jnp.sum(jnp.where(lane == i1, cnt, 0.0), axis=-1, keepdims=True)
    rank2 = jnp.sum(jnp.where(lane == i2, cnt, 0.0), axis=-1, keepdims=True)
    new_carry = carry[0:1, :] + jnp.sum(onehot, axis=0, keepdims=True)
    carry[...] = jnp.broadcast_to(new_carry, carry.shape)
    cnt_ref[...] = jnp.broadcast_to(new_carry, cnt_ref.shape)
    e1 = i1 - N_GROUPS
    e2 = i2 - N_GROUPS
    rt = jnp.where(lane == 0, e1, 0.0)
    rt = jnp.where(lane == 1, e2, rt)
    rt = jnp.where(lane == 2, rank1, rt)
    rt = jnp.where(lane == 3, rank2, rt)
    rt = jnp.where(lane == 4, g1, rt)
    rt = jnp.where(lane == 5, g2, rt)
    rt_ref[...] = rt


def _out_proj_router(y, x, w_out, norm_g, w_router, b_router, n_first_rows):
    n, d = x.shape
    tm = ROW_TILE
    row = lambda i: (i, 0)
    const = lambda i: (0, 0)
    once = dict(pipeline_mode=pl.Buffered(1))
    return pl.pallas_call(
        functools.partial(_out_proj_router_kernel, n_first=n_first_rows // tm), grid=(n // tm,),
        in_specs=[pl.BlockSpec((tm, d), row), pl.BlockSpec((tm, d), row), pl.BlockSpec((d, d), const, **once),
                  pl.BlockSpec((1, d), const), pl.BlockSpec((d, LANES), const, **once),
                  pl.BlockSpec((8, LANES), const)],
        out_specs=[pl.BlockSpec((tm, d), row), pl.BlockSpec((tm, d // 2), row), pl.BlockSpec((tm, LANES), row),
                   pl.BlockSpec((8, LANES), const)],
        out_shape=[jax.ShapeDtypeStruct((n, d), F32), jax.ShapeDtypeStruct((n, d // 2), jnp.int32),
                   jax.ShapeDtypeStruct((n, LANES), F32), jax.ShapeDtypeStruct((8, LANES), F32)],
        scratch_shapes=[pltpu.VMEM((8, LANES), F32)],
        compiler_params=_cparams(("arbitrary",)), name="out_proj_router",
    )(y, x, w_out.astype(BF16), norm_g.reshape(1, d), w_router.astype(BF16), b_router)


def _expert_kernel(be_ref, nv_ref, x_ref, wg_ref, wu_ref, wd_ref, o_ref, wg_s, wu_s, wd_s):
    i = pl.program_id(0)
    nv = nv_ref[i]

    @pl.when((i == 0) | (be_ref[i] != be_ref[jnp.maximum(i - 1, 0)]))
    def _():
        wg_s[...] = wg_ref[0, 0].astype(BF16)
        wu_s[...] = wu_ref[0, 0].astype(BF16)
        wd_s[...] = wd_ref[0, 0].astype(BF16)

    @pl.when(nv > 0)
    def _():
        rows = lax.broadcasted_iota(jnp.int32, (x_ref.shape[0], 1), 0)
        packed = jnp.where(rows < nv, x_ref[...], 0)
        xb = _unpack_bf16_pairs(packed).astype(BF16)
        hg = jnp.dot(xb, wg_s[...], preferred_element_type=F32)
        hu = jnp.dot(xb, wu_s[...], preferred_element_type=F32)
        hb = (_silu(hg) * hu).astype(BF16)
        o_ref[...] = jnp.dot(hb, wd_s[...], preferred_element_type=F32)

    @pl.when(nv == 0)
    def _():
        o_ref[...] = jnp.zeros_like(o_ref)


def _experts(rows, block_exp, block_valid, w_gate, w_up, w_down, layer):
    r, half_d = rows.shape
    d = 2 * half_d
    bm = MOE_ROWS
    nblk = r // bm
    grid_spec = pltpu.PrefetchScalarGridSpec(
        num_scalar_prefetch=2, grid=(nblk,),
        in_specs=[pl.BlockSpec((bm, half_d), lambda i, be, nv: (i, 0)),
                  pl.BlockSpec((1, 1, d, D_EXPERT), lambda i, be, nv: (layer, be[i], 0, 0)),
                  pl.BlockSpec((1, 1, d, D_EXPERT), lambda i, be, nv: (layer, be[i], 0, 0)),
                  pl.BlockSpec((1, 1, D_EXPERT, d), lambda i, be, nv: (layer, be[i], 0, 0))],
        out_specs=pl.BlockSpec((bm, d), lambda i, be, nv: (i, 0)),
        scratch_shapes=[pltpu.VMEM((d, D_EXPERT), BF16), pltpu.VMEM((d, D_EXPERT), BF16),
                        pltpu.VMEM((D_EXPERT, d), BF16)])
    return pl.pallas_call(
        _expert_kernel, grid_spec=grid_spec, out_shape=jax.ShapeDtypeStruct((r, d), F32),
        compiler_params=_cparams(("arbitrary",)), name="experts",
    )(block_exp, block_valid, rows, w_gate, w_up, w_down)


def _gather_rows(table, idx):
    info = plsc.get_sparse_core_info()
    n_workers = info.num_cores * info.num_subcores
    n_rows = idx.shape[0]
    width = table.shape[1]
    ch = SC_GATHER_ROWS
    per_worker = n_rows // n_workers
    n_chunks = per_worker // ch
    assert n_workers * n_chunks * ch == n_rows
    mesh = plsc.VectorSubcoreMesh(core_axis_name="c", subcore_axis_name="s")

    @functools.partial(
        pl.kernel, mesh=mesh, out_type=jax.ShapeDtypeStruct((n_rows, width), table.dtype),
        scratch_types=[pltpu.VMEM((ch,), jnp.int32), pltpu.VMEM((ch, width), table.dtype),
                       pltpu.SemaphoreType.DMA])
    def gather(table_hbm, idx_hbm, out_hbm, idx_v, rows_v, sem):
        wid = lax.axis_index("s") * info.num_cores + lax.axis_index("c")
        base = wid * per_worker

        @pl.loop(0, n_chunks)
        def _(j):
            off = pl.multiple_of(base + j * ch, ch)
            pltpu.sync_copy(idx_hbm.at[pl.ds(off, ch)], idx_v)
            pltpu.async_copy(table_hbm.at[idx_v], rows_v, sem).wait()
            pltpu.sync_copy(rows_v, out_hbm.at[pl.ds(off, ch)])

    return gather(table, idx)


def _pad_to(n, m):
    return -(-n // m) * m


def _moe(h, rt, counts_row, w_gate, w_up, w_down, layer):
    n, d = h.shape
    bm = MOE_ROWS
    sc_quant = 32 * SC_GATHER_ROWS
    counts = counts_row[N_GROUPS:N_GROUPS + N_EXPERTS].astype(jnp.int32)
    e = rt[:, 0:2].astype(jnp.int32)
    rank = rt[:, 2:4].astype(jnp.int32)
    padded = ((counts + bm - 1) // bm) * bm
    pend = jnp.cumsum(padded)
    pstart = pend - padded
    start = jnp.cumsum(counts) - counts
    n_rows = _pad_to(_pad_to(2 * n, bm) + N_EXPERTS * bm, sc_quant)
    n_rows = _pad_to(n_rows, bm)
    nblk = n_rows // bm
    blk_start = jnp.arange(nblk, dtype=jnp.int32) * bm
    block_exp = jnp.minimum(jnp.sum(blk_start[:, None] >= pend[None, :], axis=1), N_EXPERTS - 1).astype(jnp.int32)
    blk_off = blk_start - pstart[block_exp]
    block_valid = jnp.clip(counts[block_exp] - blk_off, 0, bm).astype(jnp.int32)
    order = jnp.argsort(e.reshape(-1), stable=True).astype(jnp.int32)
    tok_sorted = order // 2
    within = jnp.arange(bm, dtype=jnp.int32)[None, :]
    pos = jnp.clip((start[block_exp] + blk_off)[:, None] + within, 0, 2 * n - 1)
    src = jnp.where(within < block_valid[:, None], tok_sorted[pos], 0).reshape(-1)
    rows = _gather_rows(h, src)
    out_rows = _experts(rows, block_exp, block_valid, w_gate, w_up, w_down, layer)
    experts_iota = jnp.arange(N_EXPERTS, dtype=jnp.int32)
    dest = jnp.sum(jnp.where(e[:, :, None] == experts_iota, pstart, 0), axis=-1) + rank
    n_pad = _pad_to(n, sc_quant)
    dest = jnp.pad(dest, ((0, n_pad - n), (0, 0)))
    return _gather_rows(out_rows, dest.T.reshape(-1)), n_pad


def _chunk_masks(L):
    r = lax.broadcasted_iota(jnp.int32, (L, L), 0)
    c = lax.broadcasted_iota(jnp.int32, (L, L), 1)
    return r, c


def _row_from_col(col, r, c):
    return jnp.sum(jnp.where(r == c, col, 0.0), axis=0, keepdims=True)


def _cumsum_col_row(col, r, c):
    row = _row_from_col(col, r, c)
    cs_col = jnp.sum(jnp.where(r >= c, row, 0.0), axis=1, keepdims=True)
    cs_row = jnp.sum(jnp.where(r <= c, col, 0.0), axis=0, keepdims=True)
    return cs_col, cs_row


def _mlstm_kernel(q_ref, k_ref, v_ref, o_ref, gc_ref, c0_ref, n0_ref, m0_ref, na_ref,
                  y_ref, c1_ref, n1_ref, m1_ref, c_s, n_s, m_s, *, L, n_chunks, n_heads):
    h0 = pl.program_id(1) * n_heads
    t = pl.program_id(2)

    @pl.when(t == 0)
    def _():
        c_s[...] = c0_ref[0]
        n_s[...] = n0_ref[0]
        m_s[...] = m0_ref[0]

    r, c = _chunk_masks(L)
    causal = r >= c
    na = na_ref[...]

    heads = range(n_heads)
    cols = [slice(j * HEAD_DIM, (j + 1) * HEAD_DIM) for j in heads]
    rowsum = lambda x: jnp.sum(x, axis=-1, keepdims=True)

    def chunk(ci, carry):
        sl = pl.ds(pl.multiple_of(ci * L, L), L)
        gates = gc_ref[sl, :]
        q = [q_ref[sl, hs] for hs in cols]
        k = [k_ref[sl, hs] * HEAD_DIM ** -0.5 for hs in cols]
        v = [v_ref[sl, hs] for hs in cols]
        ig_col = [_lane_pick(gates, h0 + j) for j in heads]
        ig_row = [_row_from_col(x, r, c) for x in ig_col]
        bcr = [_cumsum_col_row(_lane_pick(gates, H_A + h0 + j), r, c) for j in heads]
        cmat = [c_s[j] for j in heads]
        nvec = [n_s[j] for j in heads]
        m_prev = [m_s[j, 0:1, 0:1] for j in heads]
        dmat = [jnp.where(causal, bc - br + ir, -jnp.inf) for (bc, br), ir in zip(bcr, ig_row)]
        inter = [bc + mp for (bc, _), mp in zip(bcr, m_prev)]
        m_t = [jnp.maximum(it, jnp.max(dm, axis=-1, keepdims=True)) for it, dm in zip(inter, dmat)]
        qk = [_dot_nt(a, b) for a, b in zip(q, k)]
        qc = [_dot(a, b) for a, b in zip(q, cmat)]
        s = [x * jnp.exp(dm - mt) for x, dm, mt in zip(qk, dmat, m_t)]
        w_inter = [jnp.exp(it - mt) for it, mt in zip(inter, m_t)]
        sv = [_dot(a, b) for a, b in zip(s, v)]
        num = [wi * a + b for wi, a, b in zip(w_inter, qc, sv)]
        qn = [_dot_nt(a, jnp.broadcast_to(b, (8, HEAD_DIM)))[:, 0:1] for a, b in zip(q, nvec)]
        den = [wi * a + rowsum(x) for wi, a, x in zip(w_inter, qn, s)]
        hout = [a / jnp.maximum(jnp.abs(b), jnp.exp(-mt)) for a, b, mt in zip(num, den, m_t)]
        m_new = [mt[L - 1:L, :] for mt in m_t]
        b_last = [bc[L - 1:L, :] for bc, _ in bcr]
        wk = [jnp.exp(bl - bc + ic - mn) for bl, (bc, _), ic, mn in zip(b_last, bcr, ig_col, m_new)]
        dec = [jnp.exp(bl + mp - mn) for bl, mp, mn in zip(b_last, m_prev, m_new)]
        kv = [_dot_tn(a * w, b) for a, w, b in zip(k, wk, v)]
        wkk = [_dot(jnp.broadcast_to(_row_from_col(w, r, c), (8, L)), a)[0:1, :] for w, a in zip(wk, k)]
        for j in heads:
            c_s[j] = dec[j] * cmat[j] + kv[j]
            n_s[j] = dec[j] * nvec[j] + wkk[j]
            m_s[j] = jnp.broadcast_to(m_new[j], (1, HEAD_DIM))
            y_ref[sl, cols[j]] = _rms(hout[j], na) * _sigmoid(o_ref[sl, cols[j]])
        return carry

    lax.fori_loop(0, n_chunks, chunk, 0)

    @pl.when(t == pl.num_programs(2) - 1)
    def _():
        c1_ref[0] = c_s[...]
        n1_ref[0] = n_s[...]
        m1_ref[0] = m_s[...]


def _retention_kernel(q_ref, k_ref, v_ref, g_ref, cos_ref, sin_ref, s0_ref, nb_ref, dm_ref, dv_ref,
                      y_ref, s1_ref, s_s, *, L, n_chunks, n_heads):
    t = pl.program_id(2)

    @pl.when(t == 0)
    def _():
        s_s[...] = s0_ref[0]

    nb = nb_ref[...]
    half = HEAD_DIM // 2
    decays = [(dm_ref[j], dv_ref[j, :, 0:1], dv_ref[j, :, 1:2], dv_ref[j, 0:1, 2:3]) for j in range(n_heads)]

    def rot(x, cos, sin):
        return x * cos + pltpu.roll(x, half, 1) * sin

    def chunk(ci, carry):
        sl = pl.ds(pl.multiple_of(ci * L, L), L)
        cos = cos_ref[sl, :]
        sin = sin_ref[sl, :]
        heads = range(n_heads)
        cols = [slice(j * HEAD_DIM, (j + 1) * HEAD_DIM) for j in heads]
        q = [rot(q_ref[sl, hs], cos, sin) for hs in cols]
        k = [rot(k_ref[sl, hs], cos, sin) * HEAD_DIM ** -0.5 for hs in cols]
        v = [v_ref[sl, hs] for hs in cols]
        smat = [s_s[j] for j in heads]
        qk = [_dot_nt(a, b) for a, b in zip(q, k)]
        qs = [_dot(a, b) for a, b in zip(q, smat)]
        kv = [_dot_tn(a * decays[j][2], b) for j, (a, b) in enumerate(zip(k, v))]
        av = [_dot(x * decays[j][0], b) for j, (x, b) in enumerate(zip(qk, v))]
        for j in heads:
            o = av[j] + decays[j][1] * qs[j]
            s_s[j] = decays[j][3] * smat[j] + kv[j]
            y_ref[sl, cols[j]] = _rms(o, nb) * _silu(g_ref[sl, cols[j]])
        return carry

    lax.fori_loop(0, n_chunks, chunk, 0)

    @pl.when(t == pl.num_programs(2) - 1)
    def _():
        s1_ref[0] = s_s[...]


def _unit_lower_inverses(mats, r, c):
    L = mats[0].shape[0]
    eye = jnp.where(r == c, 1.0, 0.0)
    shift = 4
    same = (r >> shift) == (c >> shift)
    ps = [-jnp.where(same, a, 0.0) for a in mats]
    invs = [eye + p for p in ps]
    mm = lambda a, b: jnp.dot(a, b, preferred_element_type=F32)
    for _ in range(3):
        pb = [p.astype(BF16) for p in ps]
        ps = [mm(p, p) for p in pb]
        pb = [p.astype(BF16) for p in ps]
        invs = [v + mm(v.astype(BF16), p) for v, p in zip(invs, pb)]
    while (1 << shift) < L:
        shift += 1
        same2 = (r >> shift) == (c >> shift)
        offs = [jnp.where(same2 & jnp.logical_not(same), a, 0.0).astype(BF16) for a in mats]
        ib = [v.astype(BF16) for v in invs]
        mid = [mm(o, v).astype(BF16) for o, v in zip(offs, ib)]
        invs = [v - mm(vb, m) for v, vb, m in zip(invs, ib, mid)]
        same = same2
    return invs


def _gdn_kernel(q_ref, k_ref, v_ref, gt_ref, gc_ref, cwq_ref, cwk_ref, cwv_ref, hq_ref, hk_ref, hv_ref,
                s0_ref, nc_ref, y_ref, s1_ref, s_s, eq_s, ek_s, ev_s, qn_s, kn_s, vn_s, nq_s, b_s, o0_s,
                *, L, n_chunks):
    h = pl.program_id(1)
    t = pl.program_id(2)
    tb = q_ref.shape[0]
    hist = GDN_HIST_ROWS

    @pl.when(t == 0)
    def _():
        s_s[...] = s0_ref[0, 0]
        eq_s[0:hist, :] = _round_bf16(hq_ref[0])
        ek_s[0:hist, :] = _round_bf16(hk_ref[0])
        ev_s[0:hist, :] = _round_bf16(hv_ref[0])

    def conv_silu(x_ref, e_s, cw_ref):
        e_s[hist:hist + tb, :] = _round_bf16(x_ref[...])
        cw = _round_bf16(cw_ref[...])
        acc = e_s[hist - (CONV_W - 1):hist - (CONV_W - 1) + tb, :] * cw[0:1, :]
        for tap in range(1, CONV_W):
            lo = hist - (CONV_W - 1) + tap
            acc = acc + e_s[lo:lo + tb, :] * cw[tap:tap + 1, :]
        e_s[0:hist, :] = e_s[tb:tb + hist, :]
        return _silu(acc)

    def l2n(x):
        return x * lax.rsqrt(jnp.sum(x * x, axis=-1, keepdims=True) + RMS_EPS)

    qn_s[...] = l2n(conv_silu(q_ref, eq_s, cwq_ref)) * HEAD_DIM ** -0.5
    kn_s[...] = l2n(conv_silu(k_ref, ek_s, cwk_ref))
    vn_s[...] = conv_silu(v_ref, ev_s, cwv_ref)

    r, c = _chunk_masks(L)
    lower = r >= c
    strict = r > c
    ncw = nc_ref[...]
    hd = HEAD_DIM

    chunks = range(n_chunks)
    rows = [slice(ci * L, (ci + 1) * L) for ci in chunks]
    qs = [qn_s[sl, :] for sl in rows]
    ks = [kn_s[sl, :] for sl in rows]
    vs = [vn_s[sl, :] for sl in rows]
    gates = [gc_ref[sl, :] for sl in rows]
    betas = [_lane_pick(g, h) for g in gates]
    cums = [_cumsum_col_row(_lane_pick(g, H_C + h), r, c) for g in gates]
    gams = [jnp.where(lower, jnp.exp(jnp.where(lower, gc - gr, 0.0)), 0.0) for gc, gr in cums]
    kqs = [_dot_nt(jnp.concatenate([k, q], axis=0), k) for k, q in zip(ks, qs)]
    tinvs = _unit_lower_inverses(
        [jnp.where(strict, b * kq[0:L, :] * gam, 0.0) for b, kq, gam in zip(betas, kqs, gams)], r, c)
    egs = [jnp.exp(gc) for gc, _ in cums]
    rhss = [jnp.concatenate([v * b, k * (b * eg)], axis=1) for v, k, b, eg in zip(vs, ks, betas, egs)]
    uws = [_dot(ti, rh) for ti, rh in zip(tinvs, rhss)]
    g_lasts = [gc[L - 1:L, :] for gc, _ in cums]
    bns = [_dot_tn(k * jnp.exp(gl - gc), uw) for k, gl, (gc, _), uw in zip(ks, g_lasts, cums, uws)]
    aos = [_dot(kq[L:2 * L, :] * gam, uw) for kq, gam, uw in zip(kqs, gams, uws)]
    for ci in chunks:
        nq_s[ci, 0:hd, :] = bns[ci][:, hd:2 * hd]
        nq_s[ci, hd:hd + L, :] = qs[ci] * egs[ci] - aos[ci][:, hd:2 * hd]
        b_s[ci] = bns[ci][:, 0:hd]
        o0_s[ci] = aos[ci][:, 0:hd]
    decays = [jnp.exp(gl) for gl in g_lasts]

    smat = s_s[...]
    for ci in range(n_chunks):
        sl = slice(ci * L, (ci + 1) * L)
        ns = _dot(nq_s[ci], smat)
        o = ns[hd:hd + L, :] + o0_s[ci]
        y_ref[sl, :] = _rms(o, ncw) * _silu(gt_ref[sl, :])
        smat = decays[ci] * smat - ns[0:hd, :] + b_s[ci]
    s_s[...] = smat

    @pl.when(t == pl.num_programs(2) - 1)
    def _():
        s1_ref[0, 0] = s_s[...]


def _scan_geometry(t_len):
    tb = min(SCAN_BLOCK, t_len)
    L = min(SCAN_CHUNK, t_len)
    return tb, L, t_len // tb, tb // L


def _col_spec(tb, nt, row0_blocks, col_fn, nh=1):
    return pl.BlockSpec((tb, nh * HEAD_DIM), lambda b, g, t: (row0_blocks + b * nt + t, col_fn(g)))


def _state_spec(shape_tail, nh=1):
    nd = len(shape_tail)
    return pl.BlockSpec((1, nh) + shape_tail, lambda b, g, t: (b, g) + (0,) * nd)


def _mlstm_scan(z, gc, y_prev, row0, bsz, t_len, c0, n0, m0, norm_a):
    tb, L, nt, n_chunks = _scan_geometry(t_len)
    rb = row0 // tb
    hb = HEAD_DIM
    nh = SCAN_HEADS
    ng = H_A // nh
    in_specs = [_col_spec(tb, nt, rb, lambda g: g, nh), _col_spec(tb, nt, rb, lambda g: ng + g, nh),
                _col_spec(tb, nt, rb, lambda g: 2 * ng + g, nh), _col_spec(tb, nt, rb, lambda g: 3 * ng + g, nh),
                pl.BlockSpec((tb, GATE_LANES), lambda b, g, t: (rb + b * nt + t, 0)),
                _state_spec((hb, hb), nh), _state_spec((1, hb), nh), _state_spec((1, hb), nh),
                pl.BlockSpec((1, hb), lambda b, g, t: (0, 0))]
    out_specs = [_col_spec(tb, nt, rb, lambda g: g, nh),
                 _state_spec((hb, hb), nh), _state_spec((1, hb), nh), _state_spec((1, hb), nh)]
    out_shape = [jax.ShapeDtypeStruct(y_prev.shape, F32),
                 jax.ShapeDtypeStruct((bsz, H_A, hb, hb), F32), jax.ShapeDtypeStruct((bsz, H_A, 1, hb), F32),
                 jax.ShapeDtypeStruct((bsz, H_A, 1, hb), F32)]
    m0b = jnp.broadcast_to(m0[:, :, None, None], (bsz, H_A, 1, hb))
    kern = functools.partial(_mlstm_kernel_aliased, L=L, n_chunks=n_chunks, n_heads=nh)
    y, c1, n1, m1 = pl.pallas_call(
        kern, grid=(bsz, ng, nt), in_specs=in_specs + [pl.BlockSpec(memory_space=pl.ANY)],
        out_specs=out_specs, out_shape=out_shape,
        scratch_shapes=[pltpu.VMEM((nh, hb, hb), F32), pltpu.VMEM((nh, 1, hb), F32), pltpu.VMEM((nh, 1, hb), F32)],
        input_output_aliases={9: 0},
        compiler_params=_cparams(("parallel", "parallel", "arbitrary")), name="mlstm_scan",
    )(z, z, z, z, gc, c0, n0[:, :, None, :], m0b, norm_a.reshape(1, hb), y_prev)
    return y, c1, n1[:, :, 0, :], m1[:, :, 0, 0]


def _mlstm_kernel_aliased(*refs, **kw):
    return _mlstm_kernel(*refs[:9], *refs[10:], **kw)


def _retention_scan(z, cos, sin, y_prev, row0, bsz, t_len, s0, norm_b):
    tb, L, nt, n_chunks = _scan_geometry(t_len)
    rb = row0 // tb
    hb = HEAD_DIM
    nh = SCAN_HEADS
    ng = H_B // nh
    base = 4 * (H_A // nh)
    in_specs = [_col_spec(tb, nt, rb, lambda g: base + g, nh), _col_spec(tb, nt, rb, lambda g: base + ng + g, nh),
                _col_spec(tb, nt, rb, lambda g: base + 2 * ng + g, nh),
                _col_spec(tb, nt, rb, lambda g: base + 3 * ng + g, nh),
                pl.BlockSpec((tb, hb), lambda b, g, t: (t, 0)), pl.BlockSpec((tb, hb), lambda b, g, t: (t, 0)),
                _state_spec((hb, hb), nh), pl.BlockSpec((1, hb), lambda b, g, t: (0, 0)),
                pl.BlockSpec((nh, L, L), lambda b, g, t: (g, 0, 0)),
                pl.BlockSpec((nh, L, hb), lambda b, g, t: (g, 0, 0)),
                pl.BlockSpec(memory_space=pl.ANY)]
    out_specs = [_col_spec(tb, nt, rb, lambda g: H_A // nh + g, nh), _state_spec((hb, hb), nh)]
    out_shape = [jax.ShapeDtypeStruct(y_prev.shape, F32), jax.ShapeDtypeStruct((bsz, H_B, hb, hb), F32)]
    lg = jnp.log(1.0 - 2.0 ** (-5.0 - jnp.arange(H_B, dtype=F32)))
    idx = jnp.arange(L, dtype=F32)
    rel = idx[:, None] - idx[None, :]
    causal = rel >= 0
    dmat = jnp.where(causal, jnp.exp(jnp.where(causal, rel, 0.0) * lg[:, None, None]), 0.0)
    q_dec = jnp.exp((idx + 1.0) * lg[:, None])
    k_dec = jnp.exp((L - 1.0 - idx) * lg[:, None])
    c_dec = jnp.broadcast_to(jnp.exp(L * lg)[:, None], (H_B, L))
    dvec = jnp.pad(jnp.stack([q_dec, k_dec, c_dec], axis=-1), ((0, 0), (0, 0), (0, hb - 3)))
    kern = functools.partial(_retention_kernel_aliased, L=L, n_chunks=n_chunks, n_heads=nh)
    return pl.pallas_call(
        kern, grid=(bsz, ng, nt), in_specs=in_specs, out_specs=out_specs, out_shape=out_shape,
        scratch_shapes=[pltpu.VMEM((nh, hb, hb), F32)], input_output_aliases={10: 0},
        compiler_params=_cparams(("parallel", "parallel", "arbitrary")), name="retention_scan",
    )(z, z, z, z, cos, sin, s0, norm_b.reshape(1, hb), dmat, dvec, y_prev)


def _retention_kernel_aliased(*refs, **kw):
    return _retention_kernel(*refs[:10], *refs[11:], **kw)


def _gdn_scan(z, gc, y_prev, row0, bsz, t_len, conv_w, conv_hist, s0, norm_c):
    tb, L, nt, n_chunks = _scan_geometry(t_len)
    rb = row0 // tb
    hb = HEAD_DIM
    cw_spec = lambda off: pl.BlockSpec((CONV_W, hb), lambda b, h, t: (0, off + h))
    hist_spec = lambda off: pl.BlockSpec((1, GDN_HIST_ROWS, hb), lambda b, h, t: (b, 0, off + h))
    in_specs = [_col_spec(tb, nt, rb, lambda h: h), _col_spec(tb, nt, rb, lambda h: H_C + h),
                _col_spec(tb, nt, rb, lambda h: 2 * H_C + h), _col_spec(tb, nt, rb, lambda h: 3 * H_C + h),
                pl.BlockSpec((tb, GATE_LANES), lambda b, h, t: (rb + b * nt + t, 0)),
                cw_spec(0), cw_spec(H_C), cw_spec(2 * H_C),
                hist_spec(0), hist_spec(H_C), hist_spec(2 * H_C),
                _state_spec((hb, hb)), pl.BlockSpec((1, hb), lambda b, h, t: (0, 0)),
                pl.BlockSpec(memory_space=pl.ANY)]
    out_specs = [_col_spec(tb, nt, rb, lambda h: h), _state_spec((hb, hb))]
    out_shape = [jax.ShapeDtypeStruct(y_prev.shape, F32), jax.ShapeDtypeStruct((bsz, H_C, hb, hb), F32)]
    kern = functools.partial(_gdn_kernel_aliased, L=L, n_chunks=n_chunks)
    ext = pltpu.VMEM((tb + GDN_HIST_ROWS, hb), F32)
    blk = pltpu.VMEM((tb, hb), F32)
    return pl.pallas_call(
        kern, grid=(bsz, H_C, nt), in_specs=in_specs, out_specs=out_specs, out_shape=out_shape,
        scratch_shapes=[pltpu.VMEM((hb, hb), F32), ext, ext, ext, blk, blk, blk,
                        pltpu.VMEM((n_chunks, hb + L, hb), F32), pltpu.VMEM((n_chunks, hb, hb), F32),
                        pltpu.VMEM((n_chunks, L, hb), F32)],
        input_output_aliases={13: 0},
        compiler_params=_cparams(("parallel", "parallel", "arbitrary")), name="gdn_scan",
    )(z, z, z, z, gc, conv_w, conv_w, conv_w, conv_hist, conv_hist, conv_hist, s0, norm_c.reshape(1, hb), y_prev)


def _gdn_kernel_aliased(*refs, **kw):
    return _gdn_kernel(*refs[:13], *refs[14:], **kw)


def _rope_tables(pos):
    half = HEAD_DIM // 2
    inv = ROPE_BASE ** (-jnp.arange(half, dtype=F32) / half)
    ang = pos.astype(F32)[:, None] * inv[None, :]
    cos, sin = jnp.cos(ang), jnp.sin(ang)
    return jnp.concatenate([cos, cos], axis=-1), jnp.concatenate([-sin, sin], axis=-1)


def _gate_rows(*rows):
    out = jnp.zeros((8, GATE_LANES), F32)
    for i, rvals in enumerate(rows):
        out = out.at[i, :rvals.shape[0]].set(rvals.astype(F32))
    return out


def _pad_cols(w, width):
    return jnp.pad(w, ((0, 0), (0, width - w.shape[1])))


def kernel(x_prompt, x_sample, state_mlstm_c, state_mlstm_n, state_mlstm_m, state_ret, state_gdn, state_gdn_conv, norm_mix, norm_ffn, norm_final, ab_w_in, ab_gate_bias, ab_w_out, ab_norm_a, ab_norm_b, c_w_in, c_conv_w, c_a_log, c_dt_bias, c_norm, c_w_out, moe_w_grp, moe_b_grp, moe_w_exp, moe_b_exp, moe_w_gate, moe_w_up, moe_w_down):
    bp, tp, d = x_prompt.shape
    bs, ts, _ = x_sample.shape
    n_p, n_s = bp * tp, bs * ts
    n = n_p + n_s
    x = jnp.concatenate([x_prompt.reshape(n_p, d), x_sample.reshape(n_s, d)], axis=0)
    hd = HEAD_DIM
    qkv_a = 4 * H_A * hd
    n_gate_a = 2 * H_A
    c_qkv = 3 * H_C * hd
    n_gate_c = 2 * H_C

    def moe_layer(y, xres, w_out, layer):
        w_router = _pad_cols(jnp.concatenate([moe_w_grp[layer], moe_w_exp[layer]], axis=1), LANES)
        b_router = _gate_rows(jnp.concatenate([moe_b_grp[layer], moe_b_exp[layer]]))
        x1, hmoe, rt, cnt = _out_proj_router(y, xres, w_out, norm_ffn[layer], w_router, b_router, n_p)
        gab, second_row0 = _moe(hmoe, rt, cnt[0], moe_w_gate, moe_w_up, moe_w_down, layer)
        return x1, gab, second_row0, rt

    w_in = ab_w_in[0]
    w_main = jnp.concatenate([w_in[:, :qkv_a], w_in[:, qkv_a + n_gate_a:]], axis=1)
    w_gate = _pad_cols(w_in[:, qkv_a:qkv_a + n_gate_a], GATE_LANES)
    z, gc = _in_proj(x, norm_mix[0], w_main, w_gate, _gate_rows(ab_gate_bias[0]), 0, n_p)
    y = jnp.zeros((n, d), F32)
    zeros = lambda *s: jnp.zeros(s, F32)
    cos_p, sin_p = _rope_tables(jnp.arange(tp))
    cos_s, sin_s = _rope_tables(PAST_LEN + jnp.arange(ts))
    y, c_p, n_pm, m_p = _mlstm_scan(z, gc, y, 0, bp, tp, zeros(bp, H_A, hd, hd), zeros(bp, H_A, hd),
                                    zeros(bp, H_A), ab_norm_a[0])
    y, c_s, n_sm, m_s = _mlstm_scan(z, gc, y, n_p, bs, ts, state_mlstm_c[0], state_mlstm_n[0],
                                    state_mlstm_m[0], ab_norm_a[0])
    y, r_p = _retention_scan(z, cos_p, sin_p, y, 0, bp, tp, zeros(bp, H_B, hd, hd), ab_norm_b[0])
    y, r_s = _retention_scan(z, cos_s, sin_s, y, n_p, bs, ts, state_ret[0], ab_norm_b[0])
    x1, gab, second_row0, rt = moe_layer(y, x, ab_w_out[0], 0)

    w_in = c_w_in[0]
    w_main = jnp.concatenate([w_in[:, :c_qkv], w_in[:, c_qkv + n_gate_c:]], axis=1)
    w_gate = _pad_cols(w_in[:, c_qkv:c_qkv + n_gate_c], GATE_LANES)
    zero8 = jnp.zeros((H_C,), F32)
    gparams = _gate_rows(jnp.concatenate([zero8, c_dt_bias[0]]), jnp.concatenate([zero8, c_a_log[0]]))
    x2, z, gc = _in_proj(x1, norm_mix[1], w_main, w_gate, gparams, 1, n_p, comb=(gab, second_row0, rt))
    hist_p = jnp.zeros((bp, GDN_HIST_ROWS, c_qkv), F32)
    hist_s = jnp.pad(state_gdn_conv[0], ((0, 0), (GDN_HIST_ROWS - (CONV_W - 1), 0), (0, 0)))
    y = jnp.zeros((n, d), F32)
    y, g_p = _gdn_scan(z, gc, y, 0, bp, tp, c_conv_w[0], hist_p, zeros(bp, H_C, hd, hd), c_norm[0])
    y, g_s = _gdn_scan(z, gc, y, n_p, bs, ts, c_conv_w[0], hist_s, state_gdn[0], c_norm[0])
    keep = CONV_W - 1
    last_rows = lambda row0, b, t: (row0 + np.arange(b)[:, None] * t + np.arange(t - keep, t)[None, :]).reshape(-1)
    conv_p = jnp.take(z, last_rows(0, bp, tp), axis=0)[:, :c_qkv].reshape(bp, keep, c_qkv)
    conv_s = jnp.take(z, last_rows(n_p, bs, ts), axis=0)[:, :c_qkv].reshape(bs, keep, c_qkv)
    x3, gab, second_row0, rt = moe_layer(y, x2, c_w_out[0], 1)

    yf_p, yf_s = _final_norm(x3, gab, second_row0, rt, norm_final, n_p)
    y_prompt = yf_p.reshape(bp, tp, d)
    y_sample = yf_s.reshape(bs, ts, d)
    return (y_prompt, y_sample, c_p[None], n_pm[None], m_p[None], r_p[None], g_p[None], conv_p[None],
            c_s[None], n_sm[None], m_s[None], r_s[None], g_s[None], conv_s[None])
```

```python
import functools
import math

import jax
import jax.numpy as jnp
import numpy as np
from jax import lax
from jax.experimental import pallas as pl
from jax.experimental.pallas import tpu as pltpu
from jax.experimental.pallas import tpu_sc as plsc

F32 = jnp.float32
BF16 = jnp.bfloat16

D_MODEL = 1024
H_A = 4
H_B = 4
H_C = 8
HEAD_DIM = 128
CONV_W = 4
N_GROUPS = 4
EXP_PER_GROUP = 8
N_EXPERTS = N_GROUPS * EXP_PER_GROUP
D_EXPERT = 512
RMS_EPS = 1e-6
ROPE_BASE = 10000.0
PAST_LEN = 2048

LANES = 128
GATE_LANES = LANES
VMEM_LIMIT = 56 * 1024 * 1024

ROW_TILE = 512
SCAN_BLOCK = 512
SCAN_CHUNK = 64
SCAN_HEADS = 4
GDN_HEADS = 2
MOE_ROWS = 512
SC_GATHER_BYTES = 256 * 1024
SC_MAX_INDEX_VECTOR = 128
SC_WORKERS = 32
GDN_HIST_ROWS = 8


def _cparams(sem):
    return pltpu.CompilerParams(dimension_semantics=sem, vmem_limit_bytes=VMEM_LIMIT)


def _dot(a, b):
    return jnp.dot(a.astype(BF16), b.astype(BF16), preferred_element_type=F32)


def _dot_nt(a, b):
    return lax.dot_general(a.astype(BF16), b.astype(BF16), (((1,), (1,)), ((), ())),
                           preferred_element_type=F32)


def _dot_tn(a, b):
    return lax.dot_general(a.astype(BF16), b.astype(BF16), (((0,), (0,)), ((), ())),
                           preferred_element_type=F32)


def _pack_bf16_pairs(x):
    w = x.shape[1] // 2
    bits = lax.bitcast_convert_type(_round_bf16(x), jnp.uint32)
    return lax.bitcast_convert_type((bits[:, :w] >> 16) | bits[:, w:], jnp.int32)


def _unpack_bf16_pairs(p):
    bits = lax.bitcast_convert_type(p, jnp.uint32)
    lo = lax.bitcast_convert_type(bits << 16, F32)
    hi = lax.bitcast_convert_type(bits & jnp.uint32(0xFFFF0000), F32)
    return jnp.concatenate([lo, hi], axis=1)


def _round_bf16(x):
    return x.astype(BF16).astype(F32)


def _softplus(x):
    return jnp.maximum(x, 0.0) + jnp.log1p(jnp.exp(-jnp.abs(x)))


def _sigmoid(x):
    return 1.0 / (1.0 + jnp.exp(-x))


def _silu(x):
    return x * _sigmoid(x)


def _rms(x, g):
    return x * lax.rsqrt(jnp.mean(x * x, axis=-1, keepdims=True) + RMS_EPS) * g


def _mean_sq_sublane_order(x):
    rows, d = x.shape
    acc = None
    for c in range(d // LANES):
        xc = x[:, c * LANES:(c + 1) * LANES]
        acc = xc * xc if acc is None else acc + xc * xc
    acc_t = acc.T
    s8 = acc_t[0:8, :]
    for a in range(1, LANES // 8):
        s8 = s8 + acc_t[8 * a:8 * a + 8, :]
    ms_row = jnp.sum(s8, axis=0, keepdims=True) * (1.0 / d)
    r = lax.broadcasted_iota(jnp.int32, (rows, rows), 0)
    c = lax.broadcasted_iota(jnp.int32, (rows, rows), 1)
    return jnp.sum(jnp.where(r == c, ms_row, 0.0), axis=1, keepdims=True)


def _rms_rows(x, g, sublane_order):
    ms = jnp.where(sublane_order, _mean_sq_sublane_order(x), jnp.mean(x * x, axis=-1, keepdims=True))
    return x * lax.rsqrt(ms + RMS_EPS) * g


def _lane_pick(tile, idx):
    lane = lax.broadcasted_iota(jnp.int32, tile.shape, 1)
    return jnp.sum(jnp.where(lane == idx, tile, 0.0), axis=-1, keepdims=True)


def _combine(x_ref, ga_ref, gb_ref, rt_ref):
    rt = rt_ref[...]
    return x_ref[...] + (_lane_pick(rt, 4) * ga_ref[...] + _lane_pick(rt, 5) * gb_ref[...])


def _in_proj_kernel(*refs, combine, gate_mode, n_col_chunks, col_chunk, n_first):
    first = pl.program_id(0) < n_first
    if combine:
        x_ref, ga_ref, gb_ref, rt_ref = refs[:4]
        refs = refs[4:]
        x = _combine(x_ref, ga_ref, gb_ref, rt_ref)
    else:
        x_ref = refs[0]
        refs = refs[1:]
        x = x_ref[...]
    g_ref, w_ref, wg_ref, gp_ref = refs[:4]
    outs = refs[4:]
    if combine:
        xo_ref, z_ref, gc_ref = outs
        xo_ref[...] = x
    else:
        z_ref, gc_ref = outs
    xn = _rms_rows(x, g_ref[...], first)
    xh = xn.astype(BF16)
    for c in range(n_col_chunks):
        cs = slice(c * col_chunk, (c + 1) * col_chunk)
        z_ref[:, cs] = jnp.dot(xh, w_ref[:, cs], preferred_element_type=F32)
    raw = jnp.dot(xh, wg_ref[...], preferred_element_type=F32)
    lane = lax.broadcasted_iota(jnp.int32, raw.shape, 1)
    bias = gp_ref[0:1, :]
    if gate_mode == 0:
        val = raw + bias
        gc_ref[...] = jnp.where(lane < H_A, val, -_softplus(-val))
    else:
        neg_a = -jnp.exp(gp_ref[1:2, :])
        gc_ref[...] = jnp.where(lane < H_C, _sigmoid(raw), neg_a * _softplus(raw + bias))


def _in_proj(x, norm_g, w_main, w_gate, gate_params, gate_mode, n_first_rows, comb=None):
    n, d = x.shape
    nz = w_main.shape[1]
    tm = ROW_TILE
    col_chunk = 512
    row = lambda i: (i, 0)
    const = lambda i: (0, 0)
    in_specs = [pl.BlockSpec((tm, d), row)]
    args = [x]
    out_specs = []
    out_shape = []
    once = dict(pipeline_mode=pl.Buffered(1))
    if comb is not None:
        gab, second_row0, rt = comb
        second = second_row0 // tm
        in_specs += [pl.BlockSpec((tm, d), row), pl.BlockSpec((tm, d), lambda i: (second + i, 0)),
                     pl.BlockSpec((tm, LANES), row)]
        args += [gab, gab, rt]
        out_specs.append(pl.BlockSpec((tm, d), row))
        out_shape.append(jax.ShapeDtypeStruct((n, d), F32))
    in_specs += [pl.BlockSpec((1, d), const), pl.BlockSpec((d, nz), const, **once),
                 pl.BlockSpec((d, GATE_LANES), const, **once), pl.BlockSpec((8, GATE_LANES), const)]
    args += [norm_g.reshape(1, d), w_main.astype(BF16), w_gate.astype(BF16), gate_params]
    out_specs += [pl.BlockSpec((tm, nz), row), pl.BlockSpec((tm, GATE_LANES), row)]
    out_shape += [jax.ShapeDtypeStruct((n, nz), F32), jax.ShapeDtypeStruct((n, GATE_LANES), F32)]
    kern = functools.partial(_in_proj_kernel, combine=comb is not None, gate_mode=gate_mode,
                             n_col_chunks=nz // col_chunk, col_chunk=col_chunk, n_first=n_first_rows // tm)
    return pl.pallas_call(
        kern, grid=(n // tm,), in_specs=in_specs, out_specs=out_specs, out_shape=out_shape,
        compiler_params=_cparams(("parallel",)), name=f"in_proj_{gate_mode}")(*args)


def _final_kernel(x_ref, ga_ref, gb_ref, rt_ref, g_ref, y1_ref, y2_ref, *, n_first):
    i = pl.program_id(0)
    y = _rms_rows(_combine(x_ref, ga_ref, gb_ref, rt_ref), g_ref[...], i < n_first)

    @pl.when(i < n_first)
    def _():
        y1_ref[...] = y

    @pl.when(i >= n_first)
    def _():
        y2_ref[...] = y


def _final_norm(x, gab, second_row0, rt, norm_g, n_first_rows):
    n, d = x.shape
    tm = ROW_TILE
    n_first = n_first_rows // tm
    second = second_row0 // tm
    row = lambda i: (i, 0)
    return pl.pallas_call(
        functools.partial(_final_kernel, n_first=n_first), grid=(n // tm,),
        in_specs=[pl.BlockSpec((tm, d), row), pl.BlockSpec((tm, d), row),
                  pl.BlockSpec((tm, d), lambda i: (second + i, 0)),
                  pl.BlockSpec((tm, LANES), row), pl.BlockSpec((1, d), lambda i: (0, 0))],
        out_specs=[pl.BlockSpec((tm, d), lambda i: (jnp.minimum(i, n_first - 1), 0)),
                   pl.BlockSpec((tm, d), lambda i: (jnp.maximum(i - n_first, 0), 0))],
        out_shape=[jax.ShapeDtypeStruct((n_first_rows, d), F32), jax.ShapeDtypeStruct((n - n_first_rows, d), F32)],
        compiler_params=_cparams(("arbitrary",)), name="final_norm")(x, gab, gab, rt, norm_g.reshape(1, d))


def _out_proj_router_kernel(y_ref, x_ref, w_ref, g_ref, wr_ref, br_ref,
                            x1_ref, h_ref, rt_ref, cnt_ref, carry, *, n_first):
    i = pl.program_id(0)

    @pl.when(i == 0)
    def _():
        carry[...] = jnp.zeros_like(carry)

    x1 = x_ref[...] + _dot(y_ref[...], w_ref[...])
    x1_ref[...] = x1
    hn = _rms_rows(x1, g_ref[...], i < n_first)
    h_ref[...] = _pack_bf16_pairs(hn)
    logits = _dot(hn, wr_ref[...]) + br_ref[0:1, :]
    tm = logits.shape[0]
    lane = lax.broadcasted_iota(jnp.int32, logits.shape, 1).astype(F32)
    neg = jnp.float32(-jnp.inf)
    far = jnp.float32(LANES)
    lg = jnp.where(lane < N_GROUPS, logits, neg)
    gmax = jnp.max(lg, axis=-1, keepdims=True)
    grp = jnp.min(jnp.where(lg == gmax, lane, far), axis=-1, keepdims=True)
    p_sel = 1.0 / jnp.sum(jnp.exp(lg - gmax), axis=-1, keepdims=True)
    lo = N_GROUPS + EXP_PER_GROUP * grp
    fm = jnp.where((lane >= lo) & (lane < lo + EXP_PER_GROUP), logits, neg)
    f1 = jnp.max(fm, axis=-1, keepdims=True)
    i1 = jnp.min(jnp.where(fm == f1, lane, far), axis=-1, keepdims=True)
    fm2 = jnp.where(lane == i1, neg, fm)
    f2 = jnp.max(fm2, axis=-1, keepdims=True)
    i2 = jnp.min(jnp.where(fm2 == f2, lane, far), axis=-1, keepdims=True)
    r = jnp.exp(f2 - f1)
    g1 = p_sel / (1.0 + r)
    g2 = p_sel * r / (1.0 + r)
    onehot = jnp.where((lane == i1) | (lane == i2), 1.0, 0.0)
    rr = lax.broadcasted_iota(jnp.int32, (tm, tm), 0)
    cc = lax.broadcasted_iota(jnp.int32, (tm, tm), 1)
    before = jnp.where(rr > cc, 1.0, 0.0).astype(BF16)
    cnt = jnp.dot(before, onehot.astype(BF16), preferred_element_type=F32) + carry[0:1, :]
    rank1 = jnp.sum(jnp.where(lane == i1, cnt, 0.0), axis=-1, keepdims=True)
    rank2 = jnp.sum(jnp.where(lane == i2, cnt, 0.0), axis=-1, keepdims=True)
    new_carry = carry[0:1, :] + jnp.sum(onehot, axis=0, keepdims=True)
    carry[...] = jnp.broadcast_to(new_carry, carry.shape)
    cnt_ref[...] = jnp.broadcast_to(new_carry, cnt_ref.shape)
    e1 = i1 - N_GROUPS
    e2 = i2 - N_GROUPS
    rt = jnp.where(lane == 0, e1, 0.0)
    rt = jnp.where(lane == 1, e2, rt)
    rt = jnp.where(lane == 2, rank1, rt)
    rt = jnp.where(lane == 3, rank2, rt)
    rt = jnp.where(lane == 4, g1, rt)
    rt = jnp.where(lane == 5, g2, rt)
    rt_ref[...] = rt


def _out_proj_router(y, x, w_out, norm_g, w_router, b_router, n_first_rows):
    n, d = x.shape
    tm = ROW_TILE
    row = lambda i: (i, 0)
    const = lambda i: (0, 0)
    once = dict(pipeline_mode=pl.Buffered(1))
    return pl.pallas_call(
        functools.partial(_out_proj_router_kernel, n_first=n_first_rows // tm), grid=(n // tm,),
        in_specs=[pl.BlockSpec((tm, d), row), pl.BlockSpec((tm, d), row), pl.BlockSpec((d, d), const, **once),
                  pl.BlockSpec((1, d), const), pl.BlockSpec((d, LANES), const, **once),
                  pl.BlockSpec((8, LANES), const)],
        out_specs=[pl.BlockSpec((tm, d), row), pl.BlockSpec((tm, d // 2), row), pl.BlockSpec((tm, LANES), row),
                   pl.BlockSpec((8, LANES), const)],
        out_shape=[jax.ShapeDtypeStruct((n, d), F32), jax.ShapeDtypeStruct((n, d // 2), jnp.int32),
                   jax.ShapeDtypeStruct((n, LANES), F32), jax.ShapeDtypeStruct((8, LANES), F32)],
        scratch_shapes=[pltpu.VMEM((8, LANES), F32)],
        compiler_params=_cparams(("arbitrary",)), name="out_proj_router",
    )(y, x, w_out.astype(BF16), norm_g.reshape(1, d), w_router.astype(BF16), b_router)


def _expert_kernel(be_ref, nv_ref, x_ref, wg_ref, wu_ref, wd_ref, o_ref, wg_s, wu_s, wd_s):
    i = pl.program_id(0)
    nv = nv_ref[i]

    @pl.when((i == 0) | (be_ref[i] != be_ref[jnp.maximum(i - 1, 0)]))
    def _():
        wg_s[...] = wg_ref[0, 0].astype(BF16)
        wu_s[...] = wu_ref[0, 0].astype(BF16)
        wd_s[...] = wd_ref[0, 0].astype(BF16)

    @pl.when(nv > 0)
    def _():
        rows = lax.broadcasted_iota(jnp.int32, (x_ref.shape[0], 1), 0)
        packed = jnp.where(rows < nv, x_ref[...], 0)
        xb = _unpack_bf16_pairs(packed).astype(BF16)
        hg = jnp.dot(xb, wg_s[...], preferred_element_type=F32)
        hu = jnp.dot(xb, wu_s[...], preferred_element_type=F32)
        hb = (_silu(hg) * hu).astype(BF16)
        o_ref[...] = jnp.dot(hb, wd_s[...], preferred_element_type=F32)

    @pl.when(nv == 0)
    def _():
        o_ref[...] = jnp.zeros_like(o_ref)


def _experts(rows, block_exp, block_valid, w_gate, w_up, w_down, layer):
    r, half_d = rows.shape
    d = 2 * half_d
    bm = MOE_ROWS
    nblk = r // bm
    grid_spec = pltpu.PrefetchScalarGridSpec(
        num_scalar_prefetch=2, grid=(nblk,),
        in_specs=[pl.BlockSpec((bm, half_d), lambda i, be, nv: (i, 0)),
                  pl.BlockSpec((1, 1, d, D_EXPERT), lambda i, be, nv: (layer, be[i], 0, 0)),
                  pl.BlockSpec((1, 1, d, D_EXPERT), lambda i, be, nv: (layer, be[i], 0, 0)),
                  pl.BlockSpec((1, 1, D_EXPERT, d), lambda i, be, nv: (layer, be[i], 0, 0))],
        out_specs=pl.BlockSpec((bm, d), lambda i, be, nv: (i, 0)),
        scratch_shapes=[pltpu.VMEM((d, D_EXPERT), BF16), pltpu.VMEM((d, D_EXPERT), BF16),
                        pltpu.VMEM((D_EXPERT, d), BF16)])
    return pl.pallas_call(
        _expert_kernel, grid_spec=grid_spec, out_shape=jax.ShapeDtypeStruct((r, d), F32),
        compiler_params=_cparams(("arbitrary",)), name="experts",
    )(block_exp, block_valid, rows, w_gate, w_up, w_down)


def _sc_chunk_rows(width, dtype):
    return min(SC_MAX_INDEX_VECTOR, SC_GATHER_BYTES // (width * jnp.dtype(dtype).itemsize))


def _gather_rows(table, idx):
    info = plsc.get_sparse_core_info()
    n_workers = info.num_cores * info.num_subcores
    n_rows = idx.shape[0]
    width = table.shape[1]
    ch = _sc_chunk_rows(width, table.dtype)
    per_worker = n_rows // n_workers
    n_chunks = per_worker // ch
    assert n_workers * n_chunks * ch == n_rows
    mesh = plsc.VectorSubcoreMesh(core_axis_name="c", subcore_axis_name="s")

    @functools.partial(
        pl.kernel, mesh=mesh, out_type=jax.ShapeDtypeStruct((n_rows, width), table.dtype),
        scratch_types=[pltpu.VMEM((ch,), jnp.int32), pltpu.VMEM((ch, width), table.dtype),
                       pltpu.SemaphoreType.DMA])
    def gather(table_hbm, idx_hbm, out_hbm, idx_v, rows_v, sem):
        wid = lax.axis_index("s") * info.num_cores + lax.axis_index("c")
        base = wid * per_worker

        @pl.loop(0, n_chunks)
        def _(j):
            off = pl.multiple_of(base + j * ch, ch)
            pltpu.sync_copy(idx_hbm.at[pl.ds(off, ch)], idx_v)
            pltpu.async_copy(table_hbm.at[idx_v], rows_v, sem).wait()
            pltpu.sync_copy(rows_v, out_hbm.at[pl.ds(off, ch)])

    return gather(table, idx)


def _pad_to(n, m):
    return -(-n // m) * m


def _moe(h, rt, counts_row, w_gate, w_up, w_down, layer):
    n, half_d = h.shape
    bm = MOE_ROWS
    sc_quant = SC_WORKERS * _sc_chunk_rows(half_d, h.dtype)
    counts = counts_row[N_GROUPS:N_GROUPS + N_EXPERTS].astype(jnp.int32)
    e = rt[:, 0:2].astype(jnp.int32)
    rank = rt[:, 2:4].astype(jnp.int32)
    padded = ((counts + bm - 1) // bm) * bm
    pend = jnp.cumsum(padded)
    pstart = pend - padded
    start = jnp.cumsum(counts) - counts
    n_rows = _pad_to(_pad_to(2 * n, bm) + N_EXPERTS * bm, sc_quant)
    n_rows = _pad_to(n_rows, bm)
    nblk = n_rows // bm
    blk_start = jnp.arange(nblk, dtype=jnp.int32) * bm
    block_exp = jnp.minimum(jnp.sum(blk_start[:, None] >= pend[None, :], axis=1), N_EXPERTS - 1).astype(jnp.int32)
    blk_off = blk_start - pstart[block_exp]
    block_valid = jnp.clip(counts[block_exp] - blk_off, 0, bm).astype(jnp.int32)
    order = jnp.argsort(e.reshape(-1), stable=True).astype(jnp.int32)
    tok_sorted = order // 2
    within = jnp.arange(bm, dtype=jnp.int32)[None, :]
    pos = jnp.clip((start[block_exp] + blk_off)[:, None] + within, 0, 2 * n - 1)
    filler = (blk_start[:, None] + within) % n
    src = jnp.where(within < block_valid[:, None], tok_sorted[pos], filler).reshape(-1)
    rows = _gather_rows(h, src)
    out_rows = _experts(rows, block_exp, block_valid, w_gate, w_up, w_down, layer)
    experts_iota = jnp.arange(N_EXPERTS, dtype=jnp.int32)
    dest = jnp.sum(jnp.where(e[:, :, None] == experts_iota, pstart, 0), axis=-1) + rank
    n_pad = _pad_to(n, SC_WORKERS * _sc_chunk_rows(out_rows.shape[1], out_rows.dtype))
    dest = jnp.concatenate([dest, jnp.arange(2 * (n_pad - n), dtype=jnp.int32).reshape(-1, 2)], axis=0)
    return _gather_rows(out_rows, dest.T.reshape(-1)), n_pad


def _chunk_masks(L):
    r = lax.broadcasted_iota(jnp.int32, (L, L), 0)
    c = lax.broadcasted_iota(jnp.int32, (L, L), 1)
    return r, c


def _row_from_col(col, r, c):
    return jnp.sum(jnp.where(r == c, col, 0.0), axis=0, keepdims=True)


def _cumsum_col_row(col, r, c):
    row = _row_from_col(col, r, c)
    cs_col = jnp.sum(jnp.where(r >= c, row, 0.0), axis=1, keepdims=True)
    cs_row = jnp.sum(jnp.where(r <= c, col, 0.0), axis=0, keepdims=True)
    return cs_col, cs_row


def _mlstm_kernel(q_ref, k_ref, v_ref, o_ref, gc_ref, c0_ref, n0_ref, m0_ref, na_ref,
                  y_ref, c1_ref, n1_ref, m1_ref, c_s, n_s, m_s, *, L, n_chunks, n_heads):
    h0 = pl.program_id(1) * n_heads
    t = pl.program_id(2)

    @pl.when(t == 0)
    def _():
        c_s[...] = c0_ref[0]
        n_s[...] = n0_ref[0]
        m_s[...] = m0_ref[0]

    r, c = _chunk_masks(L)
    causal = r >= c
    na = na_ref[...]

    heads = range(n_heads)
    cols = [slice(j * HEAD_DIM, (j + 1) * HEAD_DIM) for j in heads]
    rowsum = lambda x: jnp.sum(x, axis=-1, keepdims=True)

    def chunk(ci, carry):
        sl = pl.ds(pl.multiple_of(ci * L, L), L)
        gates = gc_ref[sl, :]
        q = [q_ref[sl, hs] for hs in cols]
        k = [k_ref[sl, hs] * HEAD_DIM ** -0.5 for hs in cols]
        v = [v_ref[sl, hs] for hs in cols]
        ig_col = [_lane_pick(gates, h0 + j) for j in heads]
        ig_row = [_row_from_col(x, r, c) for x in ig_col]
        bcr = [_cumsum_col_row(_lane_pick(gates, H_A + h0 + j), r, c) for j in heads]
        cmat = [c_s[j] for j in heads]
        nvec = [n_s[j] for j in heads]
        m_prev = [m_s[j, 0:1, 0:1] for j in heads]
        dmat = [jnp.where(causal, bc - br + ir, -jnp.inf) for (bc, br), ir in zip(bcr, ig_row)]
        inter = [bc + mp for (bc, _), mp in zip(bcr, m_prev)]
        m_t = [jnp.maximum(it, jnp.max(dm, axis=-1, keepdims=True)) for it, dm in zip(inter, dmat)]
        qk = [_dot_nt(a, b) for a, b in zip(q, k)]
        qc = [_dot(a, b) for a, b in zip(q, cmat)]
        s = [x * jnp.exp(dm - mt) for x, dm, mt in zip(qk, dmat, m_t)]
        w_inter = [jnp.exp(it - mt) for it, mt in zip(inter, m_t)]
        sv = [_dot(a, b) for a, b in zip(s, v)]
        num = [wi * a + b for wi, a, b in zip(w_inter, qc, sv)]
        qn = [_dot_nt(a, jnp.broadcast_to(b, (8, HEAD_DIM)))[:, 0:1] for a, b in zip(q, nvec)]
        den = [wi * a + rowsum(x) for wi, a, x in zip(w_inter, qn, s)]
        hout = [a / jnp.maximum(jnp.abs(b), jnp.exp(-mt)) for a, b, mt in zip(num, den, m_t)]
        m_new = [mt[L - 1:L, :] for mt in m_t]
        b_last = [bc[L - 1:L, :] for bc, _ in bcr]
        wk = [jnp.exp(bl - bc + ic - mn) for bl, (bc, _), ic, mn in zip(b_last, bcr, ig_col, m_new)]
        dec = [jnp.exp(bl + mp - mn) for bl, mp, mn in zip(b_last, m_prev, m_new)]
        kv = [_dot_tn(a * w, b) for a, w, b in zip(k, wk, v)]
        wkk = [_dot(jnp.broadcast_to(_row_from_col(w, r, c), (8, L)), a)[0:1, :] for w, a in zip(wk, k)]
        for j in heads:
            c_s[j] = dec[j] * cmat[j] + kv[j]
            n_s[j] = dec[j] * nvec[j] + wkk[j]
            m_s[j] = jnp.broadcast_to(m_new[j], (1, HEAD_DIM))
            y_ref[sl, cols[j]] = _rms(hout[j], na) * _sigmoid(o_ref[sl, cols[j]])
        return carry

    lax.fori_loop(0, n_chunks, chunk, 0)

    @pl.when(t == pl.num_programs(2) - 1)
    def _():
        c1_ref[0] = c_s[...]
        n1_ref[0] = n_s[...]
        m1_ref[0] = m_s[...]


def _retention_kernel(q_ref, k_ref, v_ref, g_ref, cos_ref, sin_ref, s0_ref, nb_ref, dm_ref, dv_ref,
                      y_ref, s1_ref, s_s, *, L, n_chunks, n_heads):
    t = pl.program_id(2)

    @pl.when(t == 0)
    def _():
        s_s[...] = s0_ref[0]

    nb = nb_ref[...]
    half = HEAD_DIM // 2
    decays = [(dm_ref[j], dv_ref[j, :, 0:1], dv_ref[j, :, 1:2], dv_ref[j, 0:1, 2:3]) for j in range(n_heads)]

    def rot(x, cos, sin):
        return x * cos + pltpu.roll(x, half, 1) * sin

    def chunk(ci, carry):
        sl = pl.ds(pl.multiple_of(ci * L, L), L)
        cos = cos_ref[sl, :]
        sin = sin_ref[sl, :]
        heads = range(n_heads)
        cols = [slice(j * HEAD_DIM, (j + 1) * HEAD_DIM) for j in heads]
        q = [rot(q_ref[sl, hs], cos, sin) for hs in cols]
        k = [rot(k_ref[sl, hs], cos, sin) * HEAD_DIM ** -0.5 for hs in cols]
        v = [v_ref[sl, hs] for hs in cols]
        smat = [s_s[j] for j in heads]
        qk = [_dot_nt(a, b) for a, b in zip(q, k)]
        qs = [_dot(a, b) for a, b in zip(q, smat)]
        kv = [_dot_tn(a * decays[j][2], b) for j, (a, b) in enumerate(zip(k, v))]
        av = [_dot(x * decays[j][0], b) for j, (x, b) in enumerate(zip(qk, v))]
        for j in heads:
            o = av[j] + decays[j][1] * qs[j]
            s_s[j] = decays[j][3] * smat[j] + kv[j]
            y_ref[sl, cols[j]] = _rms(o, nb) * _silu(g_ref[sl, cols[j]])
        return carry

    lax.fori_loop(0, n_chunks, chunk, 0)

    @pl.when(t == pl.num_programs(2) - 1)
    def _():
        s1_ref[0] = s_s[...]


def _unit_lower_inverses(mats, r, c):
    L = mats[0].shape[0]
    eye = jnp.where(r == c, 1.0, 0.0)
    shift = 4
    same = (r >> shift) == (c >> shift)
    ps = [-jnp.where(same, a, 0.0) for a in mats]
    invs = [eye + p for p in ps]
    mm = lambda a, b: jnp.dot(a, b, preferred_element_type=F32)
    for _ in range(3):
        pb = [p.astype(BF16) for p in ps]
        ps = [mm(p, p) for p in pb]
        pb = [p.astype(BF16) for p in ps]
        invs = [v + mm(v.astype(BF16), p) for v, p in zip(invs, pb)]
    while (1 << shift) < L:
        shift += 1
        same2 = (r >> shift) == (c >> shift)
        offs = [jnp.where(same2 & jnp.logical_not(same), a, 0.0).astype(BF16) for a in mats]
        ib = [v.astype(BF16) for v in invs]
        mid = [mm(o, v).astype(BF16) for o, v in zip(offs, ib)]
        invs = [v - mm(vb, m) for v, vb, m in zip(invs, ib, mid)]
        same = same2
    return invs


def _gdn_kernel(q_ref, k_ref, v_ref, gt_ref, gc_ref, cwq_ref, cwk_ref, cwv_ref, hq_ref, hk_ref, hv_ref,
                s0_ref, nc_ref, y_ref, s1_ref, s_s, eq_s, ek_s, ev_s, qn_s, kn_s, vn_s, nq_s, b_s, o0_s,
                *, L, n_chunks, n_heads):
    h0 = pl.program_id(1) * n_heads
    t = pl.program_id(2)
    tb = q_ref.shape[0]
    hist = GDN_HIST_ROWS

    @pl.when(t == 0)
    def _():
        s_s[...] = s0_ref[0]
        eq_s[0:hist, :] = _round_bf16(hq_ref[0])
        ek_s[0:hist, :] = _round_bf16(hk_ref[0])
        ev_s[0:hist, :] = _round_bf16(hv_ref[0])

    def conv_silu(x_ref, e_s, cw_ref):
        e_s[hist:hist + tb, :] = _round_bf16(x_ref[...])
        cw = _round_bf16(cw_ref[...])
        acc = e_s[hist - (CONV_W - 1):hist - (CONV_W - 1) + tb, :] * cw[0:1, :]
        for tap in range(1, CONV_W):
            lo = hist - (CONV_W - 1) + tap
            acc = acc + e_s[lo:lo + tb, :] * cw[tap:tap + 1, :]
        e_s[0:hist, :] = e_s[tb:tb + hist, :]
        return _silu(acc)

    hd = HEAD_DIM
    heads = range(n_heads)
    cols = [slice(j * hd, (j + 1) * hd) for j in heads]

    def l2n(x):
        parts = [x[:, hs] for hs in cols]
        return jnp.concatenate(
            [p * lax.rsqrt(jnp.sum(p * p, axis=-1, keepdims=True) + RMS_EPS) for p in parts], axis=1)

    qn_s[...] = l2n(conv_silu(q_ref, eq_s, cwq_ref)) * HEAD_DIM ** -0.5
    kn_s[...] = l2n(conv_silu(k_ref, ek_s, cwk_ref))
    vn_s[...] = conv_silu(v_ref, ev_s, cwv_ref)

    r, c = _chunk_masks(L)
    lower = r >= c
    strict = r > c
    ncw = nc_ref[...]

    items = [(j, ci) for j in heads for ci in range(n_chunks)]
    rows = [slice(ci * L, (ci + 1) * L) for _, ci in items]
    qs = [qn_s[sl, cols[j]] for sl, (j, _) in zip(rows, items)]
    ks = [kn_s[sl, cols[j]] for sl, (j, _) in zip(rows, items)]
    vs = [vn_s[sl, cols[j]] for sl, (j, _) in zip(rows, items)]
    gates = [gc_ref[sl, :] for sl in rows]
    betas = [_lane_pick(g, h0 + j) for g, (j, _) in zip(gates, items)]
    cums = [_cumsum_col_row(_lane_pick(g, H_C + h0 + j), r, c) for g, (j, _) in zip(gates, items)]
    gams = [jnp.where(lower, jnp.exp(jnp.where(lower, gc - gr, 0.0)), 0.0) for gc, gr in cums]
    kqs = [_dot_nt(jnp.concatenate([k, q], axis=0), k) for k, q in zip(ks, qs)]
    tinvs = _unit_lower_inverses(
        [jnp.where(strict, b * kq[0:L, :] * gam, 0.0) for b, kq, gam in zip(betas, kqs, gams)], r, c)
    egs = [jnp.exp(gc) for gc, _ in cums]
    rhss = [jnp.concatenate([v * b, k * (b * eg)], axis=1) for v, k, b, eg in zip(vs, ks, betas, egs)]
    uws = [_dot(ti, rh) for ti, rh in zip(tinvs, rhss)]
    g_lasts = [gc[L - 1:L, :] for gc, _ in cums]
    bns = [_dot_tn(k * jnp.exp(gl - gc), uw) for k, gl, (gc, _), uw in zip(ks, g_lasts, cums, uws)]
    aos = [_dot(kq[L:2 * L, :] * gam, uw) for kq, gam, uw in zip(kqs, gams, uws)]
    for i in range(len(items)):
        nq_s[i, 0:hd, :] = bns[i][:, hd:2 * hd]
        nq_s[i, hd:hd + L, :] = qs[i] * egs[i] - aos[i][:, hd:2 * hd]
        b_s[i] = bns[i][:, 0:hd]
        o0_s[i] = aos[i][:, 0:hd]
    decays = [jnp.exp(gl) for gl in g_lasts]

    smat = [s_s[j] for j in heads]
    for ci in range(n_chunks):
        sl = slice(ci * L, (ci + 1) * L)
        ids = [j * n_chunks + ci for j in heads]
        ns = [_dot(nq_s[i], s) for i, s in zip(ids, smat)]
        for j, i in enumerate(ids):
            o = ns[j][hd:hd + L, :] + o0_s[i]
            y_ref[sl, cols[j]] = _rms(o, ncw) * _silu(gt_ref[sl, cols[j]])
        smat = [decays[i] * s - n[0:hd, :] + b_s[i] for i, s, n in zip(ids, smat, ns)]
    for j in heads:
        s_s[j] = smat[j]

    @pl.when(t == pl.num_programs(2) - 1)
    def _():
        s1_ref[0] = s_s[...]


def _scan_geometry(t_len):
    tb = min(SCAN_BLOCK, t_len)
    L = min(SCAN_CHUNK, t_len)
    return tb, L, t_len // tb, tb // L


def _col_spec(tb, nt, row0_blocks, col_fn, nh=1):
    return pl.BlockSpec((tb, nh * HEAD_DIM), lambda b, g, t: (row0_blocks + b * nt + t, col_fn(g)))


def _state_spec(shape_tail, nh=1):
    nd = len(shape_tail)
    return pl.BlockSpec((1, nh) + shape_tail, lambda b, g, t: (b, g) + (0,) * nd)


def _mlstm_scan(z, gc, y_prev, row0, bsz, t_len, c0, n0, m0, norm_a):
    tb, L, nt, n_chunks = _scan_geometry(t_len)
    rb = row0 // tb
    hb = HEAD_DIM
    nh = SCAN_HEADS
    ng = H_A // nh
    in_specs = [_col_spec(tb, nt, rb, lambda g: g, nh), _col_spec(tb, nt, rb, lambda g: ng + g, nh),
                _col_spec(tb, nt, rb, lambda g: 2 * ng + g, nh), _col_spec(tb, nt, rb, lambda g: 3 * ng + g, nh),
                pl.BlockSpec((tb, GATE_LANES), lambda b, g, t: (rb + b * nt + t, 0)),
                _state_spec((hb, hb), nh), _state_spec((1, hb), nh), _state_spec((1, hb), nh),
                pl.BlockSpec((1, hb), lambda b, g, t: (0, 0))]
    out_specs = [_col_spec(tb, nt, rb, lambda g: g, nh),
                 _state_spec((hb, hb), nh), _state_spec((1, hb), nh), _state_spec((1, hb), nh)]
    out_shape = [jax.ShapeDtypeStruct(y_prev.shape, F32),
                 jax.ShapeDtypeStruct((bsz, H_A, hb, hb), F32), jax.ShapeDtypeStruct((bsz, H_A, 1, hb), F32),
                 jax.ShapeDtypeStruct((bsz, H_A, 1, hb), F32)]
    m0b = jnp.broadcast_to(m0[:, :, None, None], (bsz, H_A, 1, hb))
    kern = functools.partial(_mlstm_kernel_aliased, L=L, n_chunks=n_chunks, n_heads=nh)
    y, c1, n1, m1 = pl.pallas_call(
        kern, grid=(bsz, ng, nt), in_specs=in_specs + [pl.BlockSpec(memory_space=pl.ANY)],
        out_specs=out_specs, out_shape=out_shape,
        scratch_shapes=[pltpu.VMEM((nh, hb, hb), F32), pltpu.VMEM((nh, 1, hb), F32), pltpu.VMEM((nh, 1, hb), F32)],
        input_output_aliases={9: 0},
        compiler_params=_cparams(("parallel", "parallel", "arbitrary")), name="mlstm_scan",
    )(z, z, z, z, gc, c0, n0[:, :, None, :], m0b, norm_a.reshape(1, hb), y_prev)
    return y, c1, n1[:, :, 0, :], m1[:, :, 0, 0]


def _mlstm_kernel_aliased(*refs, **kw):
    return _mlstm_kernel(*refs[:9], *refs[10:], **kw)


def _retention_scan(z, cos, sin, y_prev, row0, bsz, t_len, s0, norm_b):
    tb, L, nt, n_chunks = _scan_geometry(t_len)
    rb = row0 // tb
    hb = HEAD_DIM
    nh = SCAN_HEADS
    ng = H_B // nh
    base = 4 * (H_A // nh)
    in_specs = [_col_spec(tb, nt, rb, lambda g: base + g, nh), _col_spec(tb, nt, rb, lambda g: base + ng + g, nh),
                _col_spec(tb, nt, rb, lambda g: base + 2 * ng + g, nh),
                _col_spec(tb, nt, rb, lambda g: base + 3 * ng + g, nh),
                pl.BlockSpec((tb, hb), lambda b, g, t: (t, 0)), pl.BlockSpec((tb, hb), lambda b, g, t: (t, 0)),
                _state_spec((hb, hb), nh), pl.BlockSpec((1, hb), lambda b, g, t: (0, 0)),
                pl.BlockSpec((nh, L, L), lambda b, g, t: (g, 0, 0)),
                pl.BlockSpec((nh, L, hb), lambda b, g, t: (g, 0, 0)),
                pl.BlockSpec(memory_space=pl.ANY)]
    out_specs = [_col_spec(tb, nt, rb, lambda g: H_A // nh + g, nh), _state_spec((hb, hb), nh)]
    out_shape = [jax.ShapeDtypeStruct(y_prev.shape, F32), jax.ShapeDtypeStruct((bsz, H_B, hb, hb), F32)]
    lg = jnp.log(1.0 - 2.0 ** (-5.0 - jnp.arange(H_B, dtype=F32)))
    idx = jnp.arange(L, dtype=F32)
    rel = idx[:, None] - idx[None, :]
    causal = rel >= 0
    dmat = jnp.where(causal, jnp.exp(jnp.where(causal, rel, 0.0) * lg[:, None, None]), 0.0)
    q_dec = jnp.exp((idx + 1.0) * lg[:, None])
    k_dec = jnp.exp((L - 1.0 - idx) * lg[:, None])
    c_dec = jnp.broadcast_to(jnp.exp(L * lg)[:, None], (H_B, L))
    dvec = jnp.pad(jnp.stack([q_dec, k_dec, c_dec], axis=-1), ((0, 0), (0, 0), (0, hb - 3)))
    kern = functools.partial(_retention_kernel_aliased, L=L, n_chunks=n_chunks, n_heads=nh)
    return pl.pallas_call(
        kern, grid=(bsz, ng, nt), in_specs=in_specs, out_specs=out_specs, out_shape=out_shape,
        scratch_shapes=[pltpu.VMEM((nh, hb, hb), F32)], input_output_aliases={10: 0},
        compiler_params=_cparams(("parallel", "parallel", "arbitrary")), name="retention_scan",
    )(z, z, z, z, cos, sin, s0, norm_b.reshape(1, hb), dmat, dvec, y_prev)


def _retention_kernel_aliased(*refs, **kw):
    return _retention_kernel(*refs[:10], *refs[11:], **kw)


def _gdn_scan(z, gc, y_prev, row0, bsz, t_len, conv_w, conv_hist, s0, norm_c):
    tb, L, nt, n_chunks = _scan_geometry(t_len)
    rb = row0 // tb
    hb = HEAD_DIM
    nh = GDN_HEADS
    ng = H_C // nh
    cw_spec = lambda off: pl.BlockSpec((CONV_W, nh * hb), lambda b, g, t: (0, off + g))
    hist_spec = lambda off: pl.BlockSpec((1, GDN_HIST_ROWS, nh * hb), lambda b, g, t: (b, 0, off + g))
    in_specs = [_col_spec(tb, nt, rb, lambda g: g, nh), _col_spec(tb, nt, rb, lambda g: ng + g, nh),
                _col_spec(tb, nt, rb, lambda g: 2 * ng + g, nh), _col_spec(tb, nt, rb, lambda g: 3 * ng + g, nh),
                pl.BlockSpec((tb, GATE_LANES), lambda b, g, t: (rb + b * nt + t, 0)),
                cw_spec(0), cw_spec(ng), cw_spec(2 * ng),
                hist_spec(0), hist_spec(ng), hist_spec(2 * ng),
                _state_spec((hb, hb), nh), pl.BlockSpec((1, hb), lambda b, g, t: (0, 0)),
                pl.BlockSpec(memory_space=pl.ANY)]
    out_specs = [_col_spec(tb, nt, rb, lambda g: g, nh), _state_spec((hb, hb), nh)]
    out_shape = [jax.ShapeDtypeStruct(y_prev.shape, F32), jax.ShapeDtypeStruct((bsz, H_C, hb, hb), F32)]
    kern = functools.partial(_gdn_kernel_aliased, L=L, n_chunks=n_chunks, n_heads=nh)
    ext = pltpu.VMEM((tb + GDN_HIST_ROWS, nh * hb), F32)
    blk = pltpu.VMEM((tb, nh * hb), F32)
    n_items = nh * n_chunks
    return pl.pallas_call(
        kern, grid=(bsz, ng, nt), in_specs=in_specs, out_specs=out_specs, out_shape=out_shape,
        scratch_shapes=[pltpu.VMEM((nh, hb, hb), F32), ext, ext, ext, blk, blk, blk,
                        pltpu.VMEM((n_items, hb + L, hb), F32), pltpu.VMEM((n_items, hb, hb), F32),
                        pltpu.VMEM((n_items, L, hb), F32)],
        input_output_aliases={13: 0},
        compiler_params=_cparams(("parallel", "parallel", "arbitrary")), name="gdn_scan",
    )(z, z, z, z, gc, conv_w, conv_w, conv_w, conv_hist, conv_hist, conv_hist, s0, norm_c.reshape(1, hb), y_prev)


def _gdn_kernel_aliased(*refs, **kw):
    return _gdn_kernel(*refs[:13], *refs[14:], **kw)


def _rope_tables(pos):
    half = HEAD_DIM // 2
    inv = ROPE_BASE ** (-jnp.arange(half, dtype=F32) / half)
    ang = pos.astype(F32)[:, None] * inv[None, :]
    cos, sin = jnp.cos(ang), jnp.sin(ang)
    return jnp.concatenate([cos, cos], axis=-1), jnp.concatenate([-sin, sin], axis=-1)


def _gate_rows(*rows):
    out = jnp.zeros((8, GATE_LANES), F32)
    for i, rvals in enumerate(rows):
        out = out.at[i, :rvals.shape[0]].set(rvals.astype(F32))
    return out


def _pad_cols(w, width):
    return jnp.pad(w, ((0, 0), (0, width - w.shape[1])))


def kernel(x_prompt, x_sample, state_mlstm_c, state_mlstm_n, state_mlstm_m, state_ret, state_gdn, state_gdn_conv, norm_mix, norm_ffn, norm_final, ab_w_in, ab_gate_bias, ab_w_out, ab_norm_a, ab_norm_b, c_w_in, c_conv_w, c_a_log, c_dt_bias, c_norm, c_w_out, moe_w_grp, moe_b_grp, moe_w_exp, moe_b_exp, moe_w_gate, moe_w_up, moe_w_down):
    bp, tp, d = x_prompt.shape
    bs, ts, _ = x_sample.shape
    n_p, n_s = bp * tp, bs * ts
    n = n_p + n_s
    x = jnp.concatenate([x_prompt.reshape(n_p, d), x_sample.reshape(n_s, d)], axis=0)
    hd = HEAD_DIM
    qkv_a = 4 * H_A * hd
    n_gate_a = 2 * H_A
    c_qkv = 3 * H_C * hd
    n_gate_c = 2 * H_C

    def moe_layer(y, xres, w_out, layer):
        w_router = _pad_cols(jnp.concatenate([moe_w_grp[layer], moe_w_exp[layer]], axis=1), LANES)
        b_router = _gate_rows(jnp.concatenate([moe_b_grp[layer], moe_b_exp[layer]]))
        x1, hmoe, rt, cnt = _out_proj_router(y, xres, w_out, norm_ffn[layer], w_router, b_router, n_p)
        gab, second_row0 = _moe(hmoe, rt, cnt[0], moe_w_gate, moe_w_up, moe_w_down, layer)
        return x1, gab, second_row0, rt

    w_in = ab_w_in[0]
    w_main = jnp.concatenate([w_in[:, :qkv_a], w_in[:, qkv_a + n_gate_a:]], axis=1)
    w_gate = _pad_cols(w_in[:, qkv_a:qkv_a + n_gate_a], GATE_LANES)
    z, gc = _in_proj(x, norm_mix[0], w_main, w_gate, _gate_rows(ab_gate_bias[0]), 0, n_p)
    y = jnp.zeros((n, d), F32)
    zeros = lambda *s: jnp.zeros(s, F32)
    cos_p, sin_p = _rope_tables(jnp.arange(tp))
    cos_s, sin_s = _rope_tables(PAST_LEN + jnp.arange(ts))
    y, c_p, n_pm, m_p = _mlstm_scan(z, gc, y, 0, bp, tp, zeros(bp, H_A, hd, hd), zeros(bp, H_A, hd),
                                    zeros(bp, H_A), ab_norm_a[0])
    y, c_s, n_sm, m_s = _mlstm_scan(z, gc, y, n_p, bs, ts, state_mlstm_c[0], state_mlstm_n[0],
                                    state_mlstm_m[0], ab_norm_a[0])
    y, r_p = _retention_scan(z, cos_p, sin_p, y, 0, bp, tp, zeros(bp, H_B, hd, hd), ab_norm_b[0])
    y, r_s = _retention_scan(z, cos_s, sin_s, y, n_p, bs, ts, state_ret[0], ab_norm_b[0])
    x1, gab, second_row0, rt = moe_layer(y, x, ab_w_out[0], 0)

    w_in = c_w_in[0]
    w_main = jnp.concatenate([w_in[:, :c_qkv], w_in[:, c_qkv + n_gate_c:]], axis=1)
    w_gate = _pad_cols(w_in[:, c_qkv:c_qkv + n_gate_c], GATE_LANES)
    zero8 = jnp.zeros((H_C,), F32)
    gparams = _gate_rows(jnp.concatenate([zero8, c_dt_bias[0]]), jnp.concatenate([zero8, c_a_log[0]]))
    x2, z, gc = _in_proj(x1, norm_mix[1], w_main, w_gate, gparams, 1, n_p, comb=(gab, second_row0, rt))
    hist_p = jnp.zeros((bp, GDN_HIST_ROWS, c_qkv), F32)
    hist_s = jnp.pad(state_gdn_conv[0], ((0, 0), (GDN_HIST_ROWS - (CONV_W - 1), 0), (0, 0)))
    y = jnp.zeros((n, d), F32)
    y, g_p = _gdn_scan(z, gc, y, 0, bp, tp, c_conv_w[0], hist_p, zeros(bp, H_C, hd, hd), c_norm[0])
    y, g_s = _gdn_scan(z, gc, y, n_p, bs, ts, c_conv_w[0], hist_s, state_gdn[0], c_norm[0])
    keep = CONV_W - 1
    last_rows = lambda row0, b, t: (row0 + np.arange(b)[:, None] * t + np.arange(t - keep, t)[None, :]).reshape(-1)
    conv_p = jnp.take(z, last_rows(0, bp, tp), axis=0)[:, :c_qkv].reshape(bp, keep, c_qkv)
    conv_s = jnp.take(z, last_rows(n_p, bs, ts), axis=0)[:, :c_qkv].reshape(bs, keep, c_qkv)
    x3, gab, second_row0, rt = moe_layer(y, x2, c_w_out[0], 1)

    yf_p, yf_s = _final_norm(x3, gab, second_row0, rt, norm_final, n_p)
    y_prompt = yf_p.reshape(bp, tp, d)
    y_sample = yf_s.reshape(bs, ts, d)
    return (y_prompt, y_sample, c_p[None], n_pm[None], m_p[None], r_p[None], g_p[None], conv_p[None],
            c_s[None], n_sm[None], m_s[None], r_s[None], g_s[None], conv_s[None])
```

```python
import functools
import math

import jax
import jax.numpy as jnp
import numpy as np
from jax import lax
from jax.experimental import pallas as pl
from jax.experimental.pallas import tpu as pltpu
from jax.experimental.pallas import tpu_sc as plsc

F32 = jnp.float32
BF16 = jnp.bfloat16

D_MODEL = 1024
H_A = 4
H_B = 4
H_C = 8
HEAD_DIM = 128
CONV_W = 4
N_GROUPS = 4
EXP_PER_GROUP = 8
N_EXPERTS = N_GROUPS * EXP_PER_GROUP
D_EXPERT = 512
RMS_EPS = 1e-6
ROPE_BASE = 10000.0
PAST_LEN = 2048

LANES = 128
GATE_LANES = LANES
VMEM_LIMIT = 56 * 1024 * 1024

ROW_TILE = 512
SCAN_BLOCK = 512
SCAN_CHUNK = 64
SCAN_HEADS = 4
GDN_HEADS = 4
MOE_ROWS = 512
SC_GATHER_BYTES = 256 * 1024
SC_MAX_INDEX_VECTOR = 128
SC_WORKERS = 32
GDN_HIST_ROWS = 8


def _cparams(sem):
    return pltpu.CompilerParams(dimension_semantics=sem, vmem_limit_bytes=VMEM_LIMIT)


def _dot(a, b):
    return jnp.dot(a.astype(BF16), b.astype(BF16), preferred_element_type=F32)


def _dot_nt(a, b):
    return lax.dot_general(a.astype(BF16), b.astype(BF16), (((1,), (1,)), ((), ())),
                           preferred_element_type=F32)


def _dot_tn(a, b):
    return lax.dot_general(a.astype(BF16), b.astype(BF16), (((0,), (0,)), ((), ())),
                           preferred_element_type=F32)


def _pack_bf16_pairs(x):
    w = x.shape[1] // 2
    bits = lax.bitcast_convert_type(_round_bf16(x), jnp.uint32)
    return lax.bitcast_convert_type((bits[:, :w] >> 16) | bits[:, w:], jnp.int32)


def _unpack_bf16_pairs(p):
    bits = lax.bitcast_convert_type(p, jnp.uint32)
    lo = lax.bitcast_convert_type(bits << 16, F32)
    hi = lax.bitcast_convert_type(bits & jnp.uint32(0xFFFF0000), F32)
    return jnp.concatenate([lo, hi], axis=1)


def _round_bf16(x):
    return x.astype(BF16).astype(F32)


def _softplus(x):
    return jnp.maximum(x, 0.0) + jnp.log1p(jnp.exp(-jnp.abs(x)))


def _sigmoid(x):
    return 1.0 / (1.0 + jnp.exp(-x))


def _silu(x):
    return x * _sigmoid(x)


def _rms(x, g):
    return x * lax.rsqrt(jnp.mean(x * x, axis=-1, keepdims=True) + RMS_EPS) * g


def _mean_sq_sublane_order(x):
    rows, d = x.shape
    acc = None
    for c in range(d // LANES):
        xc = x[:, c * LANES:(c + 1) * LANES]
        acc = xc * xc if acc is None else acc + xc * xc
    acc_t = acc.T
    s8 = acc_t[0:8, :]
    for a in range(1, LANES // 8):
        s8 = s8 + acc_t[8 * a:8 * a + 8, :]
    ms_row = jnp.sum(s8, axis=0, keepdims=True) * (1.0 / d)
    r = lax.broadcasted_iota(jnp.int32, (rows, rows), 0)
    c = lax.broadcasted_iota(jnp.int32, (rows, rows), 1)
    return jnp.sum(jnp.where(r == c, ms_row, 0.0), axis=1, keepdims=True)


def _rms_rows(x, g, sublane_order):
    ms = jnp.where(sublane_order, _mean_sq_sublane_order(x), jnp.mean(x * x, axis=-1, keepdims=True))
    return x * lax.rsqrt(ms + RMS_EPS) * g


def _lane_pick(tile, idx):
    lane = lax.broadcasted_iota(jnp.int32, tile.shape, 1)
    return jnp.sum(jnp.where(lane == idx, tile, 0.0), axis=-1, keepdims=True)


def _combine(x_ref, ga_ref, gb_ref, rt_ref):
    rt = rt_ref[...]
    return x_ref[...] + (_lane_pick(rt, 4) * ga_ref[...] + _lane_pick(rt, 5) * gb_ref[...])


def _two_group_specs(tm, d, n_first):
    return [pl.BlockSpec((tm, d), lambda i: (jnp.minimum(i, n_first - 1), 0)),
            pl.BlockSpec((tm, d), lambda i: (jnp.maximum(i - n_first, 0), 0))]


def _in_proj_kernel(*refs, combine, gate_mode, n_col_chunks, col_chunk, n_first):
    first = pl.program_id(0) < n_first
    if combine:
        x_ref, ga_ref, gb_ref, rt_ref = refs[:4]
        refs = refs[4:]
        x = _combine(x_ref, ga_ref, gb_ref, rt_ref)
    else:
        x1_ref, x2_ref = refs[:2]
        refs = refs[2:]
        x = jnp.where(first, x1_ref[...], x2_ref[...])
    g_ref, w_ref, wg_ref, gp_ref = refs[:4]
    outs = refs[4:]
    if combine:
        xo_ref, z_ref, gc_ref = outs
        xo_ref[...] = x
    else:
        z_ref, gc_ref = outs
    xn = _rms_rows(x, g_ref[...], first)
    xh = xn.astype(BF16)
    for c in range(n_col_chunks):
        cs = slice(c * col_chunk, (c + 1) * col_chunk)
        z_ref[:, cs] = jnp.dot(xh, w_ref[:, cs], preferred_element_type=F32)
    raw = jnp.dot(xh, wg_ref[...], preferred_element_type=F32)
    lane = lax.broadcasted_iota(jnp.int32, raw.shape, 1)
    bias = gp_ref[0:1, :]
    if gate_mode == 0:
        val = raw + bias
        gc_ref[...] = jnp.where(lane < H_A, val, -_softplus(-val))
    else:
        neg_a = -jnp.exp(gp_ref[1:2, :])
        gc_ref[...] = jnp.where(lane < H_C, _sigmoid(raw), neg_a * _softplus(raw + bias))


def _in_proj(x, norm_g, w_main, w_gate, gate_params, gate_mode, n_first_rows, comb=None):
    tm = ROW_TILE
    n_first = n_first_rows // tm
    col_chunk = 512
    row = lambda i: (i, 0)
    const = lambda i: (0, 0)
    out_specs = []
    out_shape = []
    once = dict(pipeline_mode=pl.Buffered(1))
    if comb is None:
        n, d = x[0].shape[0] + x[1].shape[0], x[0].shape[1]
        in_specs = _two_group_specs(tm, d, n_first)
        args = list(x)
    else:
        n, d = x.shape
        in_specs = [pl.BlockSpec((tm, d), row)]
        args = [x]
    nz = w_main.shape[1]
    if comb is not None:
        gab, second_row0, rt = comb
        second = second_row0 // tm
        in_specs += [pl.BlockSpec((tm, d), row), pl.BlockSpec((tm, d), lambda i: (second + i, 0)),
                     pl.BlockSpec((tm, LANES), row)]
        args += [gab, gab, rt]
        out_specs.append(pl.BlockSpec((tm, d), row))
        out_shape.append(jax.ShapeDtypeStruct((n, d), F32))
    in_specs += [pl.BlockSpec((1, d), const), pl.BlockSpec((d, nz), const, **once),
                 pl.BlockSpec((d, GATE_LANES), const, **once), pl.BlockSpec((8, GATE_LANES), const)]
    args += [norm_g.reshape(1, d), w_main.astype(BF16), w_gate.astype(BF16), gate_params]
    out_specs += [pl.BlockSpec((tm, nz), row), pl.BlockSpec((tm, GATE_LANES), row)]
    out_shape += [jax.ShapeDtypeStruct((n, nz), F32), jax.ShapeDtypeStruct((n, GATE_LANES), F32)]
    kern = functools.partial(_in_proj_kernel, combine=comb is not None, gate_mode=gate_mode,
                             n_col_chunks=nz // col_chunk, col_chunk=col_chunk, n_first=n_first)
    return pl.pallas_call(
        kern, grid=(n // tm,), in_specs=in_specs, out_specs=out_specs, out_shape=out_shape,
        compiler_params=_cparams(("parallel",)), name=f"in_proj_{gate_mode}")(*args)


def _final_kernel(x_ref, ga_ref, gb_ref, rt_ref, g_ref, y1_ref, y2_ref, *, n_first):
    i = pl.program_id(0)
    y = _rms_rows(_combine(x_ref, ga_ref, gb_ref, rt_ref), g_ref[...], i < n_first)

    @pl.when(i < n_first)
    def _():
        y1_ref[...] = y

    @pl.when(i >= n_first)
    def _():
        y2_ref[...] = y


def _final_norm(x, gab, second_row0, rt, norm_g, n_first_rows):
    n, d = x.shape
    tm = ROW_TILE
    n_first = n_first_rows // tm
    second = second_row0 // tm
    row = lambda i: (i, 0)
    return pl.pallas_call(
        functools.partial(_final_kernel, n_first=n_first), grid=(n // tm,),
        in_specs=[pl.BlockSpec((tm, d), row), pl.BlockSpec((tm, d), row),
                  pl.BlockSpec((tm, d), lambda i: (second + i, 0)),
                  pl.BlockSpec((tm, LANES), row), pl.BlockSpec((1, d), lambda i: (0, 0))],
        out_specs=[pl.BlockSpec((tm, d), lambda i: (jnp.minimum(i, n_first - 1), 0)),
                   pl.BlockSpec((tm, d), lambda i: (jnp.maximum(i - n_first, 0), 0))],
        out_shape=[jax.ShapeDtypeStruct((n_first_rows, d), F32), jax.ShapeDtypeStruct((n - n_first_rows, d), F32)],
        compiler_params=_cparams(("arbitrary",)), name="final_norm")(x, gab, gab, rt, norm_g.reshape(1, d))


def _out_proj_router_kernel(y_ref, *refs, n_first, two_groups):
    i = pl.program_id(0)
    if two_groups:
        xres = jnp.where(i < n_first, refs[0][...], refs[1][...])
        refs = refs[2:]
    else:
        xres = refs[0][...]
        refs = refs[1:]
    w_ref, g_ref, wr_ref, br_ref, x1_ref, h_ref, rt_ref, cnt_ref, carry = refs

    @pl.when(i == 0)
    def _():
        carry[...] = jnp.zeros_like(carry)

    x1 = xres + _dot(y_ref[...], w_ref[...])
    x1_ref[...] = x1
    hn = _rms_rows(x1, g_ref[...], i < n_first)
    h_ref[...] = _pack_bf16_pairs(hn)
    logits = _dot(hn, wr_ref[...]) + br_ref[0:1, :]
    tm = logits.shape[0]
    lane = lax.broadcasted_iota(jnp.int32, logits.shape, 1).astype(F32)
    neg = jnp.float32(-jnp.inf)
    far = jnp.float32(LANES)
    lg = jnp.where(lane < N_GROUPS, logits, neg)
    gmax = jnp.max(lg, axis=-1, keepdims=True)
    grp = jnp.min(jnp.where(lg == gmax, lane, far), axis=-1, keepdims=True)
    p_sel = 1.0 / jnp.sum(jnp.exp(lg - gmax), axis=-1, keepdims=True)
    lo = N_GROUPS + EXP_PER_GROUP * grp
    fm = jnp.where((lane >= lo) & (lane < lo + EXP_PER_GROUP), logits, neg)
    f1 = jnp.max(fm, axis=-1, keepdims=True)
    i1 = jnp.min(jnp.where(fm == f1, lane, far), axis=-1, keepdims=True)
    fm2 = jnp.where(lane == i1, neg, fm)
    f2 = jnp.max(fm2, axis=-1, keepdims=True)
    i2 = jnp.min(jnp.where(fm2 == f2, lane, far), axis=-1, keepdims=True)
    r = jnp.exp(f2 - f1)
    g1 = p_sel / (1.0 + r)
    g2 = p_sel * r / (1.0 + r)
    onehot = jnp.where((lane == i1) | (lane == i2), 1.0, 0.0)
    rr = lax.broadcasted_iota(jnp.int32, (tm, tm), 0)
    cc = lax.broadcasted_iota(jnp.int32, (tm, tm), 1)
    before = jnp.where(rr > cc, 1.0, 0.0).astype(BF16)
    cnt = jnp.dot(before, onehot.astype(BF16), preferred_element_type=F32) + carry[0:1, :]
    rank1 = jnp.sum(jnp.where(lane == i1, cnt, 0.0), axis=-1, keepdims=True)
    rank2 = jnp.sum(jnp.where(lane == i2, cnt, 0.0), axis=-1, keepdims=True)
    new_carry = carry[0:1, :] + jnp.sum(onehot, axis=0, keepdims=True)
    carry[...] = jnp.broadcast_to(new_carry, carry.shape)
    cnt_ref[...] = jnp.broadcast_to(new_carry, cnt_ref.shape)
    e1 = i1 - N_GROUPS
    e2 = i2 - N_GROUPS
    rt = jnp.where(lane == 0, e1, 0.0)
    rt = jnp.where(lane == 1, e2, rt)
    rt = jnp.where(lane == 2, rank1, rt)
    rt = jnp.where(lane == 3, rank2, rt)
    rt = jnp.where(lane == 4, g1, rt)
    rt = jnp.where(lane == 5, g2, rt)
    rt_ref[...] = rt


def _out_proj_router(y, x, w_out, norm_g, w_router, b_router, n_first_rows):
    n, d = y.shape
    tm = ROW_TILE
    n_first = n_first_rows // tm
    row = lambda i: (i, 0)
    const = lambda i: (0, 0)
    once = dict(pipeline_mode=pl.Buffered(1))
    two_groups = isinstance(x, tuple)
    x_specs = _two_group_specs(tm, d, n_first) if two_groups else [pl.BlockSpec((tm, d), row)]
    x_args = list(x) if two_groups else [x]
    return pl.pallas_call(
        functools.partial(_out_proj_router_kernel, n_first=n_first, two_groups=two_groups), grid=(n // tm,),
        in_specs=[pl.BlockSpec((tm, d), row)] + x_specs + [
            pl.BlockSpec((d, d), const, **once), pl.BlockSpec((1, d), const),
            pl.BlockSpec((d, LANES), const, **once), pl.BlockSpec((8, LANES), const)],
        out_specs=[pl.BlockSpec((tm, d), row), pl.BlockSpec((tm, d // 2), row), pl.BlockSpec((tm, LANES), row),
                   pl.BlockSpec((8, LANES), const)],
        out_shape=[jax.ShapeDtypeStruct((n, d), F32), jax.ShapeDtypeStruct((n, d // 2), jnp.int32),
                   jax.ShapeDtypeStruct((n, LANES), F32), jax.ShapeDtypeStruct((8, LANES), F32)],
        scratch_shapes=[pltpu.VMEM((8, LANES), F32)],
        compiler_params=_cparams(("arbitrary",)), name="out_proj_router",
    )(y, *x_args, w_out.astype(BF16), norm_g.reshape(1, d), w_router.astype(BF16), b_router)


def _expert_kernel(be_ref, nv_ref, x_ref, wg_ref, wu_ref, wd_ref, o_ref, wg_s, wu_s, wd_s):
    i = pl.program_id(0)
    nv = nv_ref[i]

    @pl.when((i == 0) | (be_ref[i] != be_ref[jnp.maximum(i - 1, 0)]))
    def _():
        wg_s[...] = wg_ref[0, 0].astype(BF16)
        wu_s[...] = wu_ref[0, 0].astype(BF16)
        wd_s[...] = wd_ref[0, 0].astype(BF16)

    @pl.when(nv > 0)
    def _():
        rows = lax.broadcasted_iota(jnp.int32, (x_ref.shape[0], 1), 0)
        packed = jnp.where(rows < nv, x_ref[...], 0)
        xb = _unpack_bf16_pairs(packed).astype(BF16)
        hg = jnp.dot(xb, wg_s[...], preferred_element_type=F32)
        hu = jnp.dot(xb, wu_s[...], preferred_element_type=F32)
        hb = (_silu(hg) * hu).astype(BF16)
        o_ref[...] = jnp.dot(hb, wd_s[...], preferred_element_type=F32)

    @pl.when(nv == 0)
    def _():
        o_ref[...] = jnp.zeros_like(o_ref)


def _experts(rows, block_exp, block_valid, w_gate, w_up, w_down, layer):
    r, half_d = rows.shape
    d = 2 * half_d
    bm = MOE_ROWS
    nblk = r // bm
    grid_spec = pltpu.PrefetchScalarGridSpec(
        num_scalar_prefetch=2, grid=(nblk,),
        in_specs=[pl.BlockSpec((bm, half_d), lambda i, be, nv: (i, 0)),
                  pl.BlockSpec((1, 1, d, D_EXPERT), lambda i, be, nv: (layer, be[i], 0, 0)),
                  pl.BlockSpec((1, 1, d, D_EXPERT), lambda i, be, nv: (layer, be[i], 0, 0)),
                  pl.BlockSpec((1, 1, D_EXPERT, d), lambda i, be, nv: (layer, be[i], 0, 0))],
        out_specs=pl.BlockSpec((bm, d), lambda i, be, nv: (i, 0)),
        scratch_shapes=[pltpu.VMEM((d, D_EXPERT), BF16), pltpu.VMEM((d, D_EXPERT), BF16),
                        pltpu.VMEM((D_EXPERT, d), BF16)])
    return pl.pallas_call(
        _expert_kernel, grid_spec=grid_spec, out_shape=jax.ShapeDtypeStruct((r, d), F32),
        compiler_params=_cparams(("arbitrary",)), name="experts",
    )(block_exp, block_valid, rows, w_gate, w_up, w_down)


def _sc_chunk_rows(width, dtype):
    return min(SC_MAX_INDEX_VECTOR, SC_GATHER_BYTES // (width * jnp.dtype(dtype).itemsize))


def _gather_rows(table, idx):
    info = plsc.get_sparse_core_info()
    n_workers = info.num_cores * info.num_subcores
    n_rows = idx.shape[0]
    width = table.shape[1]
    ch = _sc_chunk_rows(width, table.dtype)
    per_worker = n_rows // n_workers
    n_chunks = per_worker // ch
    assert n_workers * n_chunks * ch == n_rows
    mesh = plsc.VectorSubcoreMesh(core_axis_name="c", subcore_axis_name="s")

    @functools.partial(
        pl.kernel, mesh=mesh, out_type=jax.ShapeDtypeStruct((n_rows, width), table.dtype),
        scratch_types=[pltpu.VMEM((ch,), jnp.int32), pltpu.VMEM((ch, width), table.dtype),
                       pltpu.SemaphoreType.DMA])
    def gather(table_hbm, idx_hbm, out_hbm, idx_v, rows_v, sem):
        wid = lax.axis_index("s") * info.num_cores + lax.axis_index("c")
        base = wid * per_worker

        @pl.loop(0, n_chunks)
        def _(j):
            off = pl.multiple_of(base + j * ch, ch)
            pltpu.sync_copy(idx_hbm.at[pl.ds(off, ch)], idx_v)
            pltpu.async_copy(table_hbm.at[idx_v], rows_v, sem).wait()
            pltpu.sync_copy(rows_v, out_hbm.at[pl.ds(off, ch)])

    return gather(table, idx)


def _pad_to(n, m):
    return -(-n // m) * m


def _moe(h, rt, counts_row, w_gate, w_up, w_down, layer):
    n, half_d = h.shape
    bm = MOE_ROWS
    sc_quant = SC_WORKERS * _sc_chunk_rows(half_d, h.dtype)
    counts = counts_row[N_GROUPS:N_GROUPS + N_EXPERTS].astype(jnp.int32)
    e = rt[:, 0:2].astype(jnp.int32)
    rank = rt[:, 2:4].astype(jnp.int32)
    padded = ((counts + bm - 1) // bm) * bm
    pend = jnp.cumsum(padded)
    pstart = pend - padded
    start = jnp.cumsum(counts) - counts
    n_rows = _pad_to(_pad_to(2 * n, bm) + N_EXPERTS * bm, sc_quant)
    n_rows = _pad_to(n_rows, bm)
    nblk = n_rows // bm
    blk_start = jnp.arange(nblk, dtype=jnp.int32) * bm
    block_exp = jnp.minimum(jnp.sum(blk_start[:, None] >= pend[None, :], axis=1), N_EXPERTS - 1).astype(jnp.int32)
    blk_off = blk_start - pstart[block_exp]
    block_valid = jnp.clip(counts[block_exp] - blk_off, 0, bm).astype(jnp.int32)
    order = jnp.argsort(e.reshape(-1), stable=True).astype(jnp.int32)
    tok_sorted = order // 2
    within = jnp.arange(bm, dtype=jnp.int32)[None, :]
    pos = jnp.clip((start[block_exp] + blk_off)[:, None] + within, 0, 2 * n - 1)
    filler = (blk_start[:, None] + within) % n
    src = jnp.where(within < block_valid[:, None], tok_sorted[pos], filler).reshape(-1)
    rows = _gather_rows(h, src)
    out_rows = _experts(rows, block_exp, block_valid, w_gate, w_up, w_down, layer)
    experts_iota = jnp.arange(N_EXPERTS, dtype=jnp.int32)
    dest = jnp.sum(jnp.where(e[:, :, None] == experts_iota, pstart, 0), axis=-1) + rank
    n_pad = _pad_to(n, SC_WORKERS * _sc_chunk_rows(out_rows.shape[1], out_rows.dtype))
    dest = jnp.concatenate([dest, jnp.arange(2 * (n_pad - n), dtype=jnp.int32).reshape(-1, 2)], axis=0)
    return _gather_rows(out_rows, dest.T.reshape(-1)), n_pad


def _chunk_masks(L):
    r = lax.broadcasted_iota(jnp.int32, (L, L), 0)
    c = lax.broadcasted_iota(jnp.int32, (L, L), 1)
    return r, c


def _row_from_col(col, r, c):
    return jnp.sum(jnp.where(r == c, col, 0.0), axis=0, keepdims=True)


def _cumsum_col_row(col, r, c):
    row = _row_from_col(col, r, c)
    cs_col = jnp.sum(jnp.where(r >= c, row, 0.0), axis=1, keepdims=True)
    cs_row = jnp.sum(jnp.where(r <= c, col, 0.0), axis=0, keepdims=True)
    return cs_col, cs_row


def _mlstm_kernel(q_ref, k_ref, v_ref, o_ref, gc_ref, c0_ref, n0_ref, m0_ref, na_ref,
                  y_ref, c1_ref, n1_ref, m1_ref, c_s, n_s, m_s, *, L, n_chunks, n_heads):
    h0 = pl.program_id(1) * n_heads
    t = pl.program_id(2)

    @pl.when(t == 0)
    def _():
        c_s[...] = c0_ref[0]
        n_s[...] = n0_ref[0]
        m_s[...] = m0_ref[0]

    r, c = _chunk_masks(L)
    causal = r >= c
    na = na_ref[...]

    heads = range(n_heads)
    cols = [slice(j * HEAD_DIM, (j + 1) * HEAD_DIM) for j in heads]
    rowsum = lambda x: jnp.sum(x, axis=-1, keepdims=True)

    def chunk(ci, carry):
        sl = pl.ds(pl.multiple_of(ci * L, L), L)
        gates = gc_ref[sl, :]
        q = [q_ref[sl, hs] for hs in cols]
        k = [k_ref[sl, hs] * HEAD_DIM ** -0.5 for hs in cols]
        v = [v_ref[sl, hs] for hs in cols]
        ig_col = [_lane_pick(gates, h0 + j) for j in heads]
        ig_row = [_row_from_col(x, r, c) for x in ig_col]
        bcr = [_cumsum_col_row(_lane_pick(gates, H_A + h0 + j), r, c) for j in heads]
        cmat = [c_s[j] for j in heads]
        nvec = [n_s[j] for j in heads]
        m_prev = [m_s[j, 0:1, 0:1] for j in heads]
        dmat = [jnp.where(causal, bc - br + ir, -jnp.inf) for (bc, br), ir in zip(bcr, ig_row)]
        inter = [bc + mp for (bc, _), mp in zip(bcr, m_prev)]
        m_t = [jnp.maximum(it, jnp.max(dm, axis=-1, keepdims=True)) for it, dm in zip(inter, dmat)]
        qk = [_dot_nt(a, b) for a, b in zip(q, k)]
        qc = [_dot(a, b) for a, b in zip(q, cmat)]
        s = [x * jnp.exp(dm - mt) for x, dm, mt in zip(qk, dmat, m_t)]
        w_inter = [jnp.exp(it - mt) for it, mt in zip(inter, m_t)]
        sv = [_dot(a, b) for a, b in zip(s, v)]
        num = [wi * a + b for wi, a, b in zip(w_inter, qc, sv)]
        qn = [_dot_nt(a, jnp.broadcast_to(b, (8, HEAD_DIM)))[:, 0:1] for a, b in zip(q, nvec)]
        den = [wi * a + rowsum(x) for wi, a, x in zip(w_inter, qn, s)]
        hout = [a / jnp.maximum(jnp.abs(b), jnp.exp(-mt)) for a, b, mt in zip(num, den, m_t)]
        m_new = [mt[L - 1:L, :] for mt in m_t]
        b_last = [bc[L - 1:L, :] for bc, _ in bcr]
        wk = [jnp.exp(bl - bc + ic - mn) for bl, (bc, _), ic, mn in zip(b_last, bcr, ig_col, m_new)]
        dec = [jnp.exp(bl + mp - mn) for bl, mp, mn in zip(b_last, m_prev, m_new)]
        kv = [_dot_tn(a * w, b) for a, w, b in zip(k, wk, v)]
        wkk = [_dot(jnp.broadcast_to(_row_from_col(w, r, c), (8, L)), a)[0:1, :] for w, a in zip(wk, k)]
        for j in heads:
            c_s[j] = dec[j] * cmat[j] + kv[j]
            n_s[j] = dec[j] * nvec[j] + wkk[j]
            m_s[j] = jnp.broadcast_to(m_new[j], (1, HEAD_DIM))
            y_ref[sl, cols[j]] = _rms(hout[j], na) * _sigmoid(o_ref[sl, cols[j]])
        return carry

    lax.fori_loop(0, n_chunks, chunk, 0)

    @pl.when(t == pl.num_programs(2) - 1)
    def _():
        c1_ref[0] = c_s[...]
        n1_ref[0] = n_s[...]
        m1_ref[0] = m_s[...]


def _retention_kernel(q_ref, k_ref, v_ref, g_ref, cos_ref, sin_ref, s0_ref, nb_ref, dm_ref, dv_ref,
                      y_ref, s1_ref, s_s, *, L, n_chunks, n_heads):
    t = pl.program_id(2)

    @pl.when(t == 0)
    def _():
        s_s[...] = s0_ref[0]

    nb = nb_ref[...]
    half = HEAD_DIM // 2
    decays = [(dm_ref[j], dv_ref[j, :, 0:1], dv_ref[j, :, 1:2], dv_ref[j, 0:1, 2:3]) for j in range(n_heads)]

    def rot(x, cos, sin):
        return x * cos + pltpu.roll(x, half, 1) * sin

    def chunk(ci, carry):
        sl = pl.ds(pl.multiple_of(ci * L, L), L)
        cos = cos_ref[sl, :]
        sin = sin_ref[sl, :]
        heads = range(n_heads)
        cols = [slice(j * HEAD_DIM, (j + 1) * HEAD_DIM) for j in heads]
        q = [rot(q_ref[sl, hs], cos, sin) for hs in cols]
        k = [rot(k_ref[sl, hs], cos, sin) * HEAD_DIM ** -0.5 for hs in cols]
        v = [v_ref[sl, hs] for hs in cols]
        smat = [s_s[j] for j in heads]
        qk = [_dot_nt(a, b) for a, b in zip(q, k)]
        qs = [_dot(a, b) for a, b in zip(q, smat)]
        kv = [_dot_tn(a * decays[j][2], b) for j, (a, b) in enumerate(zip(k, v))]
        av = [_dot(x * decays[j][0], b) for j, (x, b) in enumerate(zip(qk, v))]
        for j in heads:
            o = av[j] + decays[j][1] * qs[j]
            s_s[j] = decays[j][3] * smat[j] + kv[j]
            y_ref[sl, cols[j]] = _rms(o, nb) * _silu(g_ref[sl, cols[j]])
        return carry

    lax.fori_loop(0, n_chunks, chunk, 0)

    @pl.when(t == pl.num_programs(2) - 1)
    def _():
        s1_ref[0] = s_s[...]


def _unit_lower_inverses(mats, r, c):
    L = mats[0].shape[0]
    eye = jnp.where(r == c, 1.0, 0.0)
    shift = 4
    same = (r >> shift) == (c >> shift)
    ps = [-jnp.where(same, a, 0.0) for a in mats]
    invs = [eye + p for p in ps]
    mm = lambda a, b: jnp.dot(a, b, preferred_element_type=F32)
    for _ in range(3):
        pb = [p.astype(BF16) for p in ps]
        ps = [mm(p, p) for p in pb]
        pb = [p.astype(BF16) for p in ps]
        invs = [v + mm(v.astype(BF16), p) for v, p in zip(invs, pb)]
    while (1 << shift) < L:
        shift += 1
        same2 = (r >> shift) == (c >> shift)
        offs = [jnp.where(same2 & jnp.logical_not(same), a, 0.0).astype(BF16) for a in mats]
        ib = [v.astype(BF16) for v in invs]
        mid = [mm(o, v).astype(BF16) for o, v in zip(offs, ib)]
        invs = [v - mm(vb, m) for v, vb, m in zip(invs, ib, mid)]
        same = same2
    return invs


def _gdn_kernel(q_ref, k_ref, v_ref, gt_ref, gc_ref, cwq_ref, cwk_ref, cwv_ref, hq_ref, hk_ref, hv_ref,
                s0_ref, nc_ref, y_ref, s1_ref, s_s, eq_s, ek_s, ev_s, qn_s, kn_s, vn_s, nq_s, b_s, o0_s,
                *, L, n_chunks, n_heads):
    h0 = pl.program_id(1) * n_heads
    t = pl.program_id(2)
    tb = q_ref.shape[0]
    hist = GDN_HIST_ROWS

    @pl.when(t == 0)
    def _():
        s_s[...] = s0_ref[0]
        eq_s[0:hist, :] = _round_bf16(hq_ref[0])
        ek_s[0:hist, :] = _round_bf16(hk_ref[0])
        ev_s[0:hist, :] = _round_bf16(hv_ref[0])

    def conv_silu(x_ref, e_s, cw_ref):
        e_s[hist:hist + tb, :] = _round_bf16(x_ref[...])
        cw = _round_bf16(cw_ref[...])
        acc = e_s[hist - (CONV_W - 1):hist - (CONV_W - 1) + tb, :] * cw[0:1, :]
        for tap in range(1, CONV_W):
            lo = hist - (CONV_W - 1) + tap
            acc = acc + e_s[lo:lo + tb, :] * cw[tap:tap + 1, :]
        e_s[0:hist, :] = e_s[tb:tb + hist, :]
        return _silu(acc)

    hd = HEAD_DIM
    heads = range(n_heads)
    cols = [slice(j * hd, (j + 1) * hd) for j in heads]

    def l2n(x):
        parts = [x[:, hs] for hs in cols]
        return jnp.concatenate(
            [p * lax.rsqrt(jnp.sum(p * p, axis=-1, keepdims=True) + RMS_EPS) for p in parts], axis=1)

    qn_s[...] = l2n(conv_silu(q_ref, eq_s, cwq_ref)) * HEAD_DIM ** -0.5
    kn_s[...] = l2n(conv_silu(k_ref, ek_s, cwk_ref))
    vn_s[...] = conv_silu(v_ref, ev_s, cwv_ref)

    r, c = _chunk_masks(L)
    lower = r >= c
    strict = r > c
    ncw = nc_ref[...]

    items = [(j, ci) for j in heads for ci in range(n_chunks)]
    rows = [slice(ci * L, (ci + 1) * L) for _, ci in items]
    qs = [qn_s[sl, cols[j]] for sl, (j, _) in zip(rows, items)]
    ks = [kn_s[sl, cols[j]] for sl, (j, _) in zip(rows, items)]
    vs = [vn_s[sl, cols[j]] for sl, (j, _) in zip(rows, items)]
    gates = [gc_ref[sl, :] for sl in rows]
    betas = [_lane_pick(g, h0 + j) for g, (j, _) in zip(gates, items)]
    cums = [_cumsum_col_row(_lane_pick(g, H_C + h0 + j), r, c) for g, (j, _) in zip(gates, items)]
    gams = [jnp.where(lower, jnp.exp(jnp.where(lower, gc - gr, 0.0)), 0.0) for gc, gr in cums]
    kqs = [_dot_nt(jnp.concatenate([k, q], axis=0), k) for k, q in zip(ks, qs)]
    tinvs = _unit_lower_inverses(
        [jnp.where(strict, b * kq[0:L, :] * gam, 0.0) for b, kq, gam in zip(betas, kqs, gams)], r, c)
    egs = [jnp.exp(gc) for gc, _ in cums]
    rhss = [jnp.concatenate([v * b, k * (b * eg)], axis=1) for v, k, b, eg in zip(vs, ks, betas, egs)]
    uws = [_dot(ti, rh) for ti, rh in zip(tinvs, rhss)]
    g_lasts = [gc[L - 1:L, :] for gc, _ in cums]
    bns = [_dot_tn(k * jnp.exp(gl - gc), uw) for k, gl, (gc, _), uw in zip(ks, g_lasts, cums, uws)]
    aos = [_dot(kq[L:2 * L, :] * gam, uw) for kq, gam, uw in zip(kqs, gams, uws)]
    for i in range(len(items)):
        nq_s[i, 0:hd, :] = bns[i][:, hd:2 * hd]
        nq_s[i, hd:hd + L, :] = qs[i] * egs[i] - aos[i][:, hd:2 * hd]
        b_s[i] = bns[i][:, 0:hd]
        o0_s[i] = aos[i][:, 0:hd]
    decays = [jnp.exp(gl) for gl in g_lasts]

    smat = [s_s[j] for j in heads]
    for ci in range(n_chunks):
        sl = slice(ci * L, (ci + 1) * L)
        ids = [j * n_chunks + ci for j in heads]
        ns = [_dot(nq_s[i], s) for i, s in zip(ids, smat)]
        for j, i in enumerate(ids):
            o = ns[j][hd:hd + L, :] + o0_s[i]
            y_ref[sl, cols[j]] = _rms(o, ncw) * _silu(gt_ref[sl, cols[j]])
        smat = [decays[i] * s - n[0:hd, :] + b_s[i] for i, s, n in zip(ids, smat, ns)]
    for j in heads:
        s_s[j] = smat[j]

    @pl.when(t == pl.num_programs(2) - 1)
    def _():
        s1_ref[0] = s_s[...]


def _scan_geometry(t_len):
    tb = min(SCAN_BLOCK, t_len)
    L = min(SCAN_CHUNK, t_len)
    return tb, L, t_len // tb, tb // L


def _col_spec(tb, nt, row0_blocks, col_fn, nh=1):
    return pl.BlockSpec((tb, nh * HEAD_DIM), lambda b, g, t: (row0_blocks + b * nt + t, col_fn(g)))


def _state_spec(shape_tail, nh=1):
    nd = len(shape_tail)
    return pl.BlockSpec((1, nh) + shape_tail, lambda b, g, t: (b, g) + (0,) * nd)


def _mlstm_scan(z, gc, y_prev, row0, bsz, t_len, c0, n0, m0, norm_a):
    tb, L, nt, n_chunks = _scan_geometry(t_len)
    rb = row0 // tb
    hb = HEAD_DIM
    nh = SCAN_HEADS
    ng = H_A // nh
    in_specs = [_col_spec(tb, nt, rb, lambda g: g, nh), _col_spec(tb, nt, rb, lambda g: ng + g, nh),
                _col_spec(tb, nt, rb, lambda g: 2 * ng + g, nh), _col_spec(tb, nt, rb, lambda g: 3 * ng + g, nh),
                pl.BlockSpec((tb, GATE_LANES), lambda b, g, t: (rb + b * nt + t, 0)),
                _state_spec((hb, hb), nh), _state_spec((1, hb), nh), _state_spec((1, hb), nh),
                pl.BlockSpec((1, hb), lambda b, g, t: (0, 0))]
    out_specs = [_col_spec(tb, nt, rb, lambda g: g, nh),
                 _state_spec((hb, hb), nh), _state_spec((1, hb), nh), _state_spec((1, hb), nh)]
    out_shape = [jax.ShapeDtypeStruct(y_prev.shape, F32),
                 jax.ShapeDtypeStruct((bsz, H_A, hb, hb), F32), jax.ShapeDtypeStruct((bsz, H_A, 1, hb), F32),
                 jax.ShapeDtypeStruct((bsz, H_A, 1, hb), F32)]
    m0b = jnp.broadcast_to(m0[:, :, None, None], (bsz, H_A, 1, hb))
    kern = functools.partial(_mlstm_kernel_aliased, L=L, n_chunks=n_chunks, n_heads=nh)
    y, c1, n1, m1 = pl.pallas_call(
        kern, grid=(bsz, ng, nt), in_specs=in_specs + [pl.BlockSpec(memory_space=pl.ANY)],
        out_specs=out_specs, out_shape=out_shape,
        scratch_shapes=[pltpu.VMEM((nh, hb, hb), F32), pltpu.VMEM((nh, 1, hb), F32), pltpu.VMEM((nh, 1, hb), F32)],
        input_output_aliases={9: 0},
        compiler_params=_cparams(("parallel", "parallel", "arbitrary")), name="mlstm_scan",
    )(z, z, z, z, gc, c0, n0[:, :, None, :], m0b, norm_a.reshape(1, hb), y_prev)
    return y, c1, n1[:, :, 0, :], m1[:, :, 0, 0]


def _mlstm_kernel_aliased(*refs, **kw):
    return _mlstm_kernel(*refs[:9], *refs[10:], **kw)


def _retention_scan(z, cos, sin, y_prev, row0, bsz, t_len, s0, norm_b):
    tb, L, nt, n_chunks = _scan_geometry(t_len)
    rb = row0 // tb
    hb = HEAD_DIM
    nh = SCAN_HEADS
    ng = H_B // nh
    base = 4 * (H_A // nh)
    in_specs = [_col_spec(tb, nt, rb, lambda g: base + g, nh), _col_spec(tb, nt, rb, lambda g: base + ng + g, nh),
                _col_spec(tb, nt, rb, lambda g: base + 2 * ng + g, nh),
                _col_spec(tb, nt, rb, lambda g: base + 3 * ng + g, nh),
                pl.BlockSpec((tb, hb), lambda b, g, t: (t, 0)), pl.BlockSpec((tb, hb), lambda b, g, t: (t, 0)),
                _state_spec((hb, hb), nh), pl.BlockSpec((1, hb), lambda b, g, t: (0, 0)),
                pl.BlockSpec((nh, L, L), lambda b, g, t: (g, 0, 0)),
                pl.BlockSpec((nh, L, hb), lambda b, g, t: (g, 0, 0)),
                pl.BlockSpec(memory_space=pl.ANY)]
    out_specs = [_col_spec(tb, nt, rb, lambda g: H_A // nh + g, nh), _state_spec((hb, hb), nh)]
    out_shape = [jax.ShapeDtypeStruct(y_prev.shape, F32), jax.ShapeDtypeStruct((bsz, H_B, hb, hb), F32)]
    lg = jnp.log(1.0 - 2.0 ** (-5.0 - jnp.arange(H_B, dtype=F32)))
    idx = jnp.arange(L, dtype=F32)
    rel = idx[:, None] - idx[None, :]
    causal = rel >= 0
    dmat = jnp.where(causal, jnp.exp(jnp.where(causal, rel, 0.0) * lg[:, None, None]), 0.0)
    q_dec = jnp.exp((idx + 1.0) * lg[:, None])
    k_dec = jnp.exp((L - 1.0 - idx) * lg[:, None])
    c_dec = jnp.broadcast_to(jnp.exp(L * lg)[:, None], (H_B, L))
    dvec = jnp.pad(jnp.stack([q_dec, k_dec, c_dec], axis=-1), ((0, 0), (0, 0), (0, hb - 3)))
    kern = functools.partial(_retention_kernel_aliased, L=L, n_chunks=n_chunks, n_heads=nh)
    return pl.pallas_call(
        kern, grid=(bsz, ng, nt), in_specs=in_specs, out_specs=out_specs, out_shape=out_shape,
        scratch_shapes=[pltpu.VMEM((nh, hb, hb), F32)], input_output_aliases={10: 0},
        compiler_params=_cparams(("parallel", "parallel", "arbitrary")), name="retention_scan",
    )(z, z, z, z, cos, sin, s0, norm_b.reshape(1, hb), dmat, dvec, y_prev)


def _retention_kernel_aliased(*refs, **kw):
    return _retention_kernel(*refs[:10], *refs[11:], **kw)


def _gdn_scan(z, gc, y_prev, row0, bsz, t_len, conv_w, conv_hist, s0, norm_c):
    tb, L, nt, n_chunks = _scan_geometry(t_len)
    rb = row0 // tb
    hb = HEAD_DIM
    nh = GDN_HEADS
    ng = H_C // nh
    cw_spec = lambda off: pl.BlockSpec((CONV_W, nh * hb), lambda b, g, t: (0, off + g))
    hist_spec = lambda off: pl.BlockSpec((1, GDN_HIST_ROWS, nh * hb), lambda b, g, t: (b, 0, off + g))
    in_specs = [_col_spec(tb, nt, rb, lambda g: g, nh), _col_spec(tb, nt, rb, lambda g: ng + g, nh),
                _col_spec(tb, nt, rb, lambda g: 2 * ng + g, nh), _col_spec(tb, nt, rb, lambda g: 3 * ng + g, nh),
                pl.BlockSpec((tb, GATE_LANES), lambda b, g, t: (rb + b * nt + t, 0)),
                cw_spec(0), cw_spec(ng), cw_spec(2 * ng),
                hist_spec(0), hist_spec(ng), hist_spec(2 * ng),
                _state_spec((hb, hb), nh), pl.BlockSpec((1, hb), lambda b, g, t: (0, 0)),
                pl.BlockSpec(memory_space=pl.ANY)]
    out_specs = [_col_spec(tb, nt, rb, lambda g: g, nh), _state_spec((hb, hb), nh)]
    out_shape = [jax.ShapeDtypeStruct(y_prev.shape, F32), jax.ShapeDtypeStruct((bsz, H_C, hb, hb), F32)]
    kern = functools.partial(_gdn_kernel_aliased, L=L, n_chunks=n_chunks, n_heads=nh)
    ext = pltpu.VMEM((tb + GDN_HIST_ROWS, nh * hb), F32)
    blk = pltpu.VMEM((tb, nh * hb), F32)
    n_items = nh * n_chunks
    return pl.pallas_call(
        kern, grid=(bsz, ng, nt), in_specs=in_specs, out_specs=out_specs, out_shape=out_shape,
        scratch_shapes=[pltpu.VMEM((nh, hb, hb), F32), ext, ext, ext, blk, blk, blk,
                        pltpu.VMEM((n_items, hb + L, hb), F32), pltpu.VMEM((n_items, hb, hb), F32),
                        pltpu.VMEM((n_items, L, hb), F32)],
        input_output_aliases={13: 0},
        compiler_params=_cparams(("parallel", "parallel", "arbitrary")), name="gdn_scan",
    )(z, z, z, z, gc, conv_w, conv_w, conv_w, conv_hist, conv_hist, conv_hist, s0, norm_c.reshape(1, hb), y_prev)


def _gdn_kernel_aliased(*refs, **kw):
    return _gdn_kernel(*refs[:13], *refs[14:], **kw)


def _rope_tables(pos):
    half = HEAD_DIM // 2
    inv = ROPE_BASE ** (-jnp.arange(half, dtype=F32) / half)
    ang = pos.astype(F32)[:, None] * inv[None, :]
    cos, sin = jnp.cos(ang), jnp.sin(ang)
    return jnp.concatenate([cos, cos], axis=-1), jnp.concatenate([-sin, sin], axis=-1)


def _gate_rows(*rows):
    out = jnp.zeros((8, GATE_LANES), F32)
    for i, rvals in enumerate(rows):
        out = out.at[i, :rvals.shape[0]].set(rvals.astype(F32))
    return out


def _pad_cols(w, width):
    return jnp.pad(w, ((0, 0), (0, width - w.shape[1])))


def kernel(x_prompt, x_sample, state_mlstm_c, state_mlstm_n, state_mlstm_m, state_ret, state_gdn, state_gdn_conv, norm_mix, norm_ffn, norm_final, ab_w_in, ab_gate_bias, ab_w_out, ab_norm_a, ab_norm_b, c_w_in, c_conv_w, c_a_log, c_dt_bias, c_norm, c_w_out, moe_w_grp, moe_b_grp, moe_w_exp, moe_b_exp, moe_w_gate, moe_w_up, moe_w_down):
    bp, tp, d = x_prompt.shape
    bs, ts, _ = x_sample.shape
    n_p, n_s = bp * tp, bs * ts
    n = n_p + n_s
    x = (x_prompt.reshape(n_p, d), x_sample.reshape(n_s, d))
    hd = HEAD_DIM
    qkv_a = 4 * H_A * hd
    n_gate_a = 2 * H_A
    c_qkv = 3 * H_C * hd
    n_gate_c = 2 * H_C

    def moe_layer(y, xres, w_out, layer):
        w_router = _pad_cols(jnp.concatenate([moe_w_grp[layer], moe_w_exp[layer]], axis=1), LANES)
        b_router = _gate_rows(jnp.concatenate([moe_b_grp[layer], moe_b_exp[layer]]))
        x1, hmoe, rt, cnt = _out_proj_router(y, xres, w_out, norm_ffn[layer], w_router, b_router, n_p)
        gab, second_row0 = _moe(hmoe, rt, cnt[0], moe_w_gate, moe_w_up, moe_w_down, layer)
        return x1, gab, second_row0, rt

    w_in = ab_w_in[0]
    w_main = jnp.concatenate([w_in[:, :qkv_a], w_in[:, qkv_a + n_gate_a:]], axis=1)
    w_gate = _pad_cols(w_in[:, qkv_a:qkv_a + n_gate_a], GATE_LANES)
    z, gc = _in_proj(x, norm_mix[0], w_main, w_gate, _gate_rows(ab_gate_bias[0]), 0, n_p)
    y = jnp.zeros((n, d), F32)
    zeros = lambda *s: jnp.zeros(s, F32)
    cos_p, sin_p = _rope_tables(jnp.arange(tp))
    cos_s, sin_s = _rope_tables(PAST_LEN + jnp.arange(ts))
    y, c_p, n_pm, m_p = _mlstm_scan(z, gc, y, 0, bp, tp, zeros(bp, H_A, hd, hd), zeros(bp, H_A, hd),
                                    zeros(bp, H_A), ab_norm_a[0])
    y, c_s, n_sm, m_s = _mlstm_scan(z, gc, y, n_p, bs, ts, state_mlstm_c[0], state_mlstm_n[0],
                                    state_mlstm_m[0], ab_norm_a[0])
    y, r_p = _retention_scan(z, cos_p, sin_p, y, 0, bp, tp, zeros(bp, H_B, hd, hd), ab_norm_b[0])
    y, r_s = _retention_scan(z, cos_s, sin_s, y, n_p, bs, ts, state_ret[0], ab_norm_b[0])
    x1, gab, second_row0, rt = moe_layer(y, x, ab_w_out[0], 0)

    w_in = c_w_in[0]
    w_main = jnp.concatenate([w_in[:, :c_qkv], w_in[:, c_qkv + n_gate_c:]], axis=1)
    w_gate = _pad_cols(w_in[:, c_qkv:c_qkv + n_gate_c], GATE_LANES)
    zero8 = jnp.zeros((H_C,), F32)
    gparams = _gate_rows(jnp.concatenate([zero8, c_dt_bias[0]]), jnp.concatenate([zero8, c_a_log[0]]))
    x2, z, gc = _in_proj(x1, norm_mix[1], w_main, w_gate, gparams, 1, n_p, comb=(gab, second_row0, rt))
    hist_p = jnp.zeros((bp, GDN_HIST_ROWS, c_qkv), F32)
    hist_s = jnp.pad(state_gdn_conv[0], ((0, 0), (GDN_HIST_ROWS - (CONV_W - 1), 0), (0, 0)))
    y, g_p = _gdn_scan(z, gc, y, 0, bp, tp, c_conv_w[0], hist_p, zeros(bp, H_C, hd, hd), c_norm[0])
    y, g_s = _gdn_scan(z, gc, y, n_p, bs, ts, c_conv_w[0], hist_s, state_gdn[0], c_norm[0])
    keep = CONV_W - 1
    last_rows = lambda row0, b, t: (row0 + np.arange(b)[:, None] * t + np.arange(t - keep, t)[None, :]).reshape(-1)
    conv_p = jnp.take(z, last_rows(0, bp, tp), axis=0)[:, :c_qkv].reshape(bp, keep, c_qkv)
    conv_s = jnp.take(z, last_rows(n_p, bs, ts), axis=0)[:, :c_qkv].reshape(bs, keep, c_qkv)
    x3, gab, second_row0, rt = moe_layer(y, x2, c_w_out[0], 1)

    yf_p, yf_s = _final_norm(x3, gab, second_row0, rt, norm_final, n_p)
    y_prompt = yf_p.reshape(bp, tp, d)
    y_sample = yf_s.reshape(bs, ts, d)
    return (y_prompt, y_sample, c_p[None], n_pm[None], m_p[None], r_p[None], g_p[None], conv_p[None],
            c_s[None], n_sm[None], m_s[None], r_s[None], g_s[None], conv_s[None])
```

```python
import functools
import math

import jax
import jax.numpy as jnp
import numpy as np
from jax import lax
from jax.experimental import pallas as pl
from jax.experimental.pallas import tpu as pltpu
from jax.experimental.pallas import tpu_sc as plsc

F32 = jnp.float32
BF16 = jnp.bfloat16

D_MODEL = 1024
H_A = 4
H_B = 4
H_C = 8
HEAD_DIM = 128
CONV_W = 4
N_GROUPS = 4
EXP_PER_GROUP = 8
N_EXPERTS = N_GROUPS * EXP_PER_GROUP
D_EXPERT = 512
RMS_EPS = 1e-6
ROPE_BASE = 10000.0
PAST_LEN = 2048

LANES = 128
GATE_LANES = LANES
VMEM_LIMIT = 56 * 1024 * 1024

ROW_TILE = 512
SCAN_BLOCK = 512
SCAN_CHUNK = 64
SCAN_HEADS = 4
GDN_HEADS = 8
GDN_BLOCK = 256
MOE_ROWS = 512
SC_GATHER_BYTES = 256 * 1024
SC_MAX_INDEX_VECTOR = 128
SC_WORKERS = 32
GDN_HIST_ROWS = 8


def _cparams(sem):
    return pltpu.CompilerParams(dimension_semantics=sem, vmem_limit_bytes=VMEM_LIMIT)


def _dot(a, b):
    return jnp.dot(a.astype(BF16), b.astype(BF16), preferred_element_type=F32)


def _dot_nt(a, b):
    return lax.dot_general(a.astype(BF16), b.astype(BF16), (((1,), (1,)), ((), ())),
                           preferred_element_type=F32)


def _dot_tn(a, b):
    return lax.dot_general(a.astype(BF16), b.astype(BF16), (((0,), (0,)), ((), ())),
                           preferred_element_type=F32)


def _pack_bf16_pairs(x):
    w = x.shape[1] // 2
    bits = lax.bitcast_convert_type(_round_bf16(x), jnp.uint32)
    return lax.bitcast_convert_type((bits[:, :w] >> 16) | bits[:, w:], jnp.int32)


def _unpack_bf16_pairs(p):
    bits = lax.bitcast_convert_type(p, jnp.uint32)
    lo = lax.bitcast_convert_type(bits << 16, F32)
    hi = lax.bitcast_convert_type(bits & jnp.uint32(0xFFFF0000), F32)
    return jnp.concatenate([lo, hi], axis=1)


def _round_bf16(x):
    return x.astype(BF16).astype(F32)


def _softplus(x):
    return jnp.maximum(x, 0.0) + jnp.log1p(jnp.exp(-jnp.abs(x)))


def _sigmoid(x):
    return 1.0 / (1.0 + jnp.exp(-x))


def _silu(x):
    return x * _sigmoid(x)


def _rms(x, g):
    return x * lax.rsqrt(jnp.mean(x * x, axis=-1, keepdims=True) + RMS_EPS) * g


def _mean_sq_sublane_order(x):
    rows, d = x.shape
    acc = None
    for c in range(d // LANES):
        xc = x[:, c * LANES:(c + 1) * LANES]
        acc = xc * xc if acc is None else acc + xc * xc
    acc_t = acc.T
    s8 = acc_t[0:8, :]
    for a in range(1, LANES // 8):
        s8 = s8 + acc_t[8 * a:8 * a + 8, :]
    ms_row = jnp.sum(s8, axis=0, keepdims=True) * (1.0 / d)
    r = lax.broadcasted_iota(jnp.int32, (rows, rows), 0)
    c = lax.broadcasted_iota(jnp.int32, (rows, rows), 1)
    return jnp.sum(jnp.where(r == c, ms_row, 0.0), axis=1, keepdims=True)


def _rms_rows(x, g, sublane_order):
    ms = jnp.where(sublane_order, _mean_sq_sublane_order(x), jnp.mean(x * x, axis=-1, keepdims=True))
    return x * lax.rsqrt(ms + RMS_EPS) * g


def _lane_pick(tile, idx):
    lane = lax.broadcasted_iota(jnp.int32, tile.shape, 1)
    return jnp.sum(jnp.where(lane == idx, tile, 0.0), axis=-1, keepdims=True)


def _combine(x_ref, ga_ref, gb_ref, rt_ref):
    rt = rt_ref[...]
    return x_ref[...] + (_lane_pick(rt, 4) * ga_ref[...] + _lane_pick(rt, 5) * gb_ref[...])


def _two_group_specs(tm, d, n_first):
    return [pl.BlockSpec((tm, d), lambda i: (jnp.minimum(i, n_first - 1), 0)),
            pl.BlockSpec((tm, d), lambda i: (jnp.maximum(i - n_first, 0), 0))]


def _in_proj_kernel(*refs, combine, gate_mode, n_col_chunks, col_chunk, n_first):
    first = pl.program_id(0) < n_first
    if combine:
        x_ref, ga_ref, gb_ref, rt_ref = refs[:4]
        refs = refs[4:]
        x = _combine(x_ref, ga_ref, gb_ref, rt_ref)
    else:
        x1_ref, x2_ref = refs[:2]
        refs = refs[2:]
        x = jnp.where(first, x1_ref[...], x2_ref[...])
    g_ref, w_ref, wg_ref, gp_ref = refs[:4]
    outs = refs[4:]
    if combine:
        xo_ref, z_ref, gc_ref = outs
        xo_ref[...] = x
    else:
        z_ref, gc_ref = outs
    xn = _rms_rows(x, g_ref[...], first)
    xh = xn.astype(BF16)
    for c in range(n_col_chunks):
        cs = slice(c * col_chunk, (c + 1) * col_chunk)
        z_ref[:, cs] = jnp.dot(xh, w_ref[:, cs], preferred_element_type=F32)
    raw = jnp.dot(xh, wg_ref[...], preferred_element_type=F32)
    lane = lax.broadcasted_iota(jnp.int32, raw.shape, 1)
    bias = gp_ref[0:1, :]
    if gate_mode == 0:
        val = raw + bias
        gc_ref[...] = jnp.where(lane < H_A, val, -_softplus(-val))
    else:
        neg_a = -jnp.exp(gp_ref[1:2, :])
        gc_ref[...] = jnp.where(lane < H_C, _sigmoid(raw), neg_a * _softplus(raw + bias))


def _in_proj(x, norm_g, w_main, w_gate, gate_params, gate_mode, n_first_rows, comb=None):
    tm = ROW_TILE
    n_first = n_first_rows // tm
    col_chunk = 512
    row = lambda i: (i, 0)
    const = lambda i: (0, 0)
    out_specs = []
    out_shape = []
    once = dict(pipeline_mode=pl.Buffered(1))
    if comb is None:
        n, d = x[0].shape[0] + x[1].shape[0], x[0].shape[1]
        in_specs = _two_group_specs(tm, d, n_first)
        args = list(x)
    else:
        n, d = x.shape
        in_specs = [pl.BlockSpec((tm, d), row)]
        args = [x]
    nz = w_main.shape[1]
    if comb is not None:
        gab, second_row0, rt = comb
        second = second_row0 // tm
        in_specs += [pl.BlockSpec((tm, d), row), pl.BlockSpec((tm, d), lambda i: (second + i, 0)),
                     pl.BlockSpec((tm, LANES), row)]
        args += [gab, gab, rt]
        out_specs.append(pl.BlockSpec((tm, d), row))
        out_shape.append(jax.ShapeDtypeStruct((n, d), F32))
    in_specs += [pl.BlockSpec((1, d), const), pl.BlockSpec((d, nz), const, **once),
                 pl.BlockSpec((d, GATE_LANES), const, **once), pl.BlockSpec((8, GATE_LANES), const)]
    args += [norm_g.reshape(1, d), w_main.astype(BF16), w_gate.astype(BF16), gate_params]
    out_specs += [pl.BlockSpec((tm, nz), row), pl.BlockSpec((tm, GATE_LANES), row)]
    out_shape += [jax.ShapeDtypeStruct((n, nz), F32), jax.ShapeDtypeStruct((n, GATE_LANES), F32)]
    kern = functools.partial(_in_proj_kernel, combine=comb is not None, gate_mode=gate_mode,
                             n_col_chunks=nz // col_chunk, col_chunk=col_chunk, n_first=n_first)
    return pl.pallas_call(
        kern, grid=(n // tm,), in_specs=in_specs, out_specs=out_specs, out_shape=out_shape,
        compiler_params=_cparams(("parallel",)), name=f"in_proj_{gate_mode}")(*args)


def _final_kernel(x_ref, ga_ref, gb_ref, rt_ref, g_ref, y1_ref, y2_ref, *, n_first):
    i = pl.program_id(0)
    y = _rms_rows(_combine(x_ref, ga_ref, gb_ref, rt_ref), g_ref[...], i < n_first)

    @pl.when(i < n_first)
    def _():
        y1_ref[...] = y

    @pl.when(i >= n_first)
    def _():
        y2_ref[...] = y


def _final_norm(x, gab, second_row0, rt, norm_g, n_first_rows):
    n, d = x.shape
    tm = ROW_TILE
    n_first = n_first_rows // tm
    second = second_row0 // tm
    row = lambda i: (i, 0)
    return pl.pallas_call(
        functools.partial(_final_kernel, n_first=n_first), grid=(n // tm,),
        in_specs=[pl.BlockSpec((tm, d), row), pl.BlockSpec((tm, d), row),
                  pl.BlockSpec((tm, d), lambda i: (second + i, 0)),
                  pl.BlockSpec((tm, LANES), row), pl.BlockSpec((1, d), lambda i: (0, 0))],
        out_specs=[pl.BlockSpec((tm, d), lambda i: (jnp.minimum(i, n_first - 1), 0)),
                   pl.BlockSpec((tm, d), lambda i: (jnp.maximum(i - n_first, 0), 0))],
        out_shape=[jax.ShapeDtypeStruct((n_first_rows, d), F32), jax.ShapeDtypeStruct((n - n_first_rows, d), F32)],
        compiler_params=_cparams(("arbitrary",)), name="final_norm")(x, gab, gab, rt, norm_g.reshape(1, d))


def _out_proj_router_kernel(y_ref, *refs, n_first, two_groups):
    i = pl.program_id(0)
    if two_groups:
        xres = jnp.where(i < n_first, refs[0][...], refs[1][...])
        refs = refs[2:]
    else:
        xres = refs[0][...]
        refs = refs[1:]
    w_ref, g_ref, wr_ref, br_ref, x1_ref, h_ref, rt_ref, cnt_ref, carry = refs

    @pl.when(i == 0)
    def _():
        carry[...] = jnp.zeros_like(carry)

    x1 = xres + _dot(y_ref[...], w_ref[...])
    x1_ref[...] = x1
    hn = _rms_rows(x1, g_ref[...], i < n_first)
    h_ref[...] = _pack_bf16_pairs(hn)
    logits = _dot(hn, wr_ref[...]) + br_ref[0:1, :]
    tm = logits.shape[0]
    lane = lax.broadcasted_iota(jnp.int32, logits.shape, 1).astype(F32)
    neg = jnp.float32(-jnp.inf)
    far = jnp.float32(LANES)
    lg = jnp.where(lane < N_GROUPS, logits, neg)
    gmax = jnp.max(lg, axis=-1, keepdims=True)
    grp = jnp.min(jnp.where(lg == gmax, lane, far), axis=-1, keepdims=True)
    p_sel = 1.0 / jnp.sum(jnp.exp(lg - gmax), axis=-1, keepdims=True)
    lo = N_GROUPS + EXP_PER_GROUP * grp
    fm = jnp.where((lane >= lo) & (lane < lo + EXP_PER_GROUP), logits, neg)
    f1 = jnp.max(fm, axis=-1, keepdims=True)
    i1 = jnp.min(jnp.where(fm == f1, lane, far), axis=-1, keepdims=True)
    fm2 = jnp.where(lane == i1, neg, fm)
    f2 = jnp.max(fm2, axis=-1, keepdims=True)
    i2 = jnp.min(jnp.where(fm2 == f2, lane, far), axis=-1, keepdims=True)
    r = jnp.exp(f2 - f1)
    g1 = p_sel / (1.0 + r)
    g2 = p_sel * r / (1.0 + r)
    onehot = jnp.where((lane == i1) | (lane == i2), 1.0, 0.0)
    rr = lax.broadcasted_iota(jnp.int32, (tm, tm), 0)
    cc = lax.broadcasted_iota(jnp.int32, (tm, tm), 1)
    before = jnp.where(rr > cc, 1.0, 0.0).astype(BF16)
    cnt = jnp.dot(before, onehot.astype(BF16), preferred_element_type=F32) + carry[0:1, :]
    rank1 = jnp.sum(jnp.where(lane == i1, cnt, 0.0), axis=-1, keepdims=True)
    rank2 = jnp.sum(jnp.where(lane == i2, cnt, 0.0), axis=-1, keepdims=True)
    new_carry = carry[0:1, :] + jnp.sum(onehot, axis=0, keepdims=True)
    carry[...] = jnp.broadcast_to(new_carry, carry.shape)
    cnt_ref[...] = jnp.broadcast_to(new_carry, cnt_ref.shape)
    e1 = i1 - N_GROUPS
    e2 = i2 - N_GROUPS
    rt = jnp.where(lane == 0, e1, 0.0)
    rt = jnp.where(lane == 1, e2, rt)
    rt = jnp.where(lane == 2, rank1, rt)
    rt = jnp.where(lane == 3, rank2, rt)
    rt = jnp.where(lane == 4, g1, rt)
    rt = jnp.where(lane == 5, g2, rt)
    rt_ref[...] = rt


def _out_proj_router(y, x, w_out, norm_g, w_router, b_router, n_first_rows):
    n, d = y.shape
    tm = ROW_TILE
    n_first = n_first_rows // tm
    row = lambda i: (i, 0)
    const = lambda i: (0, 0)
    once = dict(pipeline_mode=pl.Buffered(1))
    two_groups = isinstance(x, tuple)
    x_specs = _two_group_specs(tm, d, n_first) if two_groups else [pl.BlockSpec((tm, d), row)]
    x_args = list(x) if two_groups else [x]
    return pl.pallas_call(
        functools.partial(_out_proj_router_kernel, n_first=n_first, two_groups=two_groups), grid=(n // tm,),
        in_specs=[pl.BlockSpec((tm, d), row)] + x_specs + [
            pl.BlockSpec((d, d), const, **once), pl.BlockSpec((1, d), const),
            pl.BlockSpec((d, LANES), const, **once), pl.BlockSpec((8, LANES), const)],
        out_specs=[pl.BlockSpec((tm, d), row), pl.BlockSpec((tm, d // 2), row), pl.BlockSpec((tm, LANES), row),
                   pl.BlockSpec((8, LANES), const)],
        out_shape=[jax.ShapeDtypeStruct((n, d), F32), jax.ShapeDtypeStruct((n, d // 2), jnp.int32),
                   jax.ShapeDtypeStruct((n, LANES), F32), jax.ShapeDtypeStruct((8, LANES), F32)],
        scratch_shapes=[pltpu.VMEM((8, LANES), F32)],
        compiler_params=_cparams(("arbitrary",)), name="out_proj_router",
    )(y, *x_args, w_out.astype(BF16), norm_g.reshape(1, d), w_router.astype(BF16), b_router)


def _expert_kernel(be_ref, nv_ref, x_ref, wg_ref, wu_ref, wd_ref, o_ref, wg_s, wu_s, wd_s):
    i = pl.program_id(0)
    nv = nv_ref[i]

    @pl.when((i == 0) | (be_ref[i] != be_ref[jnp.maximum(i - 1, 0)]))
    def _():
        wg_s[...] = wg_ref[0, 0].astype(BF16)
        wu_s[...] = wu_ref[0, 0].astype(BF16)
        wd_s[...] = wd_ref[0, 0].astype(BF16)

    @pl.when(nv > 0)
    def _():
        rows = lax.broadcasted_iota(jnp.int32, (x_ref.shape[0], 1), 0)
        packed = jnp.where(rows < nv, x_ref[...], 0)
        xb = _unpack_bf16_pairs(packed).astype(BF16)
        hg = jnp.dot(xb, wg_s[...], preferred_element_type=F32)
        hu = jnp.dot(xb, wu_s[...], preferred_element_type=F32)
        hb = (_silu(hg) * hu).astype(BF16)
        o_ref[...] = jnp.dot(hb, wd_s[...], preferred_element_type=F32)

    @pl.when(nv == 0)
    def _():
        o_ref[...] = jnp.zeros_like(o_ref)


def _experts(rows, block_exp, block_valid, w_gate, w_up, w_down, layer):
    r, half_d = rows.shape
    d = 2 * half_d
    bm = MOE_ROWS
    nblk = r // bm
    grid_spec = pltpu.PrefetchScalarGridSpec(
        num_scalar_prefetch=2, grid=(nblk,),
        in_specs=[pl.BlockSpec((bm, half_d), lambda i, be, nv: (i, 0)),
                  pl.BlockSpec((1, 1, d, D_EXPERT), lambda i, be, nv: (layer, be[i], 0, 0)),
                  pl.BlockSpec((1, 1, d, D_EXPERT), lambda i, be, nv: (layer, be[i], 0, 0)),
                  pl.BlockSpec((1, 1, D_EXPERT, d), lambda i, be, nv: (layer, be[i], 0, 0))],
        out_specs=pl.BlockSpec((bm, d), lambda i, be, nv: (i, 0)),
        scratch_shapes=[pltpu.VMEM((d, D_EXPERT), BF16), pltpu.VMEM((d, D_EXPERT), BF16),
                        pltpu.VMEM((D_EXPERT, d), BF16)])
    return pl.pallas_call(
        _expert_kernel, grid_spec=grid_spec, out_shape=jax.ShapeDtypeStruct((r, d), F32),
        compiler_params=_cparams(("arbitrary",)), name="experts",
    )(block_exp, block_valid, rows, w_gate, w_up, w_down)


def _sc_chunk_rows(width, dtype):
    return min(SC_MAX_INDEX_VECTOR, SC_GATHER_BYTES // (width * jnp.dtype(dtype).itemsize))


def _gather_rows(table, idx):
    info = plsc.get_sparse_core_info()
    n_workers = info.num_cores * info.num_subcores
    n_rows = idx.shape[0]
    width = table.shape[1]
    ch = _sc_chunk_rows(width, table.dtype)
    per_worker = n_rows // n_workers
    n_chunks = per_worker // ch
    assert n_workers * n_chunks * ch == n_rows
    mesh = plsc.VectorSubcoreMesh(core_axis_name="c", subcore_axis_name="s")

    @functools.partial(
        pl.kernel, mesh=mesh, out_type=jax.ShapeDtypeStruct((n_rows, width), table.dtype),
        scratch_types=[pltpu.VMEM((ch,), jnp.int32), pltpu.VMEM((ch, width), table.dtype),
                       pltpu.SemaphoreType.DMA])
    def gather(table_hbm, idx_hbm, out_hbm, idx_v, rows_v, sem):
        wid = lax.axis_index("s") * info.num_cores + lax.axis_index("c")
        base = wid * per_worker

        @pl.loop(0, n_chunks)
        def _(j):
            off = pl.multiple_of(base + j * ch, ch)
            pltpu.sync_copy(idx_hbm.at[pl.ds(off, ch)], idx_v)
            pltpu.async_copy(table_hbm.at[idx_v], rows_v, sem).wait()
            pltpu.sync_copy(rows_v, out_hbm.at[pl.ds(off, ch)])

    return gather(table, idx)


def _pad_to(n, m):
    return -(-n // m) * m


def _moe(h, rt, counts_row, w_gate, w_up, w_down, layer):
    n, half_d = h.shape
    bm = MOE_ROWS
    sc_quant = SC_WORKERS * _sc_chunk_rows(half_d, h.dtype)
    counts = counts_row[N_GROUPS:N_GROUPS + N_EXPERTS].astype(jnp.int32)
    e = rt[:, 0:2].astype(jnp.int32)
    rank = rt[:, 2:4].astype(jnp.int32)
    padded = ((counts + bm - 1) // bm) * bm
    pend = jnp.cumsum(padded)
    pstart = pend - padded
    start = jnp.cumsum(counts) - counts
    n_rows = _pad_to(_pad_to(2 * n, bm) + N_EXPERTS * bm, sc_quant)
    n_rows = _pad_to(n_rows, bm)
    nblk = n_rows // bm
    blk_start = jnp.arange(nblk, dtype=jnp.int32) * bm
    block_exp = jnp.minimum(jnp.sum(blk_start[:, None] >= pend[None, :], axis=1), N_EXPERTS - 1).astype(jnp.int32)
    blk_off = blk_start - pstart[block_exp]
    block_valid = jnp.clip(counts[block_exp] - blk_off, 0, bm).astype(jnp.int32)
    order = jnp.argsort(e.reshape(-1), stable=True).astype(jnp.int32)
    tok_sorted = order // 2
    within = jnp.arange(bm, dtype=jnp.int32)[None, :]
    pos = jnp.clip((start[block_exp] + blk_off)[:, None] + within, 0, 2 * n - 1)
    filler = (blk_start[:, None] + within) % n
    src = jnp.where(within < block_valid[:, None], tok_sorted[pos], filler).reshape(-1)
    rows = _gather_rows(h, src)
    out_rows = _experts(rows, block_exp, block_valid, w_gate, w_up, w_down, layer)
    experts_iota = jnp.arange(N_EXPERTS, dtype=jnp.int32)
    dest = jnp.sum(jnp.where(e[:, :, None] == experts_iota, pstart, 0), axis=-1) + rank
    n_pad = _pad_to(n, SC_WORKERS * _sc_chunk_rows(out_rows.shape[1], out_rows.dtype))
    dest = jnp.concatenate([dest, jnp.arange(2 * (n_pad - n), dtype=jnp.int32).reshape(-1, 2)], axis=0)
    return _gather_rows(out_rows, dest.T.reshape(-1)), n_pad


def _chunk_masks(L):
    r = lax.broadcasted_iota(jnp.int32, (L, L), 0)
    c = lax.broadcasted_iota(jnp.int32, (L, L), 1)
    return r, c


def _row_from_col(col, r, c):
    return jnp.sum(jnp.where(r == c, col, 0.0), axis=0, keepdims=True)


def _cumsum_col_row(col, r, c):
    row = _row_from_col(col, r, c)
    cs_col = jnp.sum(jnp.where(r >= c, row, 0.0), axis=1, keepdims=True)
    cs_row = jnp.sum(jnp.where(r <= c, col, 0.0), axis=0, keepdims=True)
    return cs_col, cs_row


def _mlstm_kernel(q_ref, k_ref, v_ref, o_ref, gc_ref, c0_ref, n0_ref, m0_ref, na_ref,
                  y_ref, c1_ref, n1_ref, m1_ref, c_s, n_s, m_s, *, L, n_chunks, n_heads):
    h0 = pl.program_id(1) * n_heads
    t = pl.program_id(2)

    @pl.when(t == 0)
    def _():
        c_s[...] = c0_ref[0]
        n_s[...] = n0_ref[0]
        m_s[...] = m0_ref[0]

    r, c = _chunk_masks(L)
    causal = r >= c
    na = na_ref[...]

    heads = range(n_heads)
    cols = [slice(j * HEAD_DIM, (j + 1) * HEAD_DIM) for j in heads]
    rowsum = lambda x: jnp.sum(x, axis=-1, keepdims=True)

    def chunk(ci, carry):
        sl = pl.ds(pl.multiple_of(ci * L, L), L)
        gates = gc_ref[sl, :]
        q = [q_ref[sl, hs] for hs in cols]
        k = [k_ref[sl, hs] * HEAD_DIM ** -0.5 for hs in cols]
        v = [v_ref[sl, hs] for hs in cols]
        ig_col = [_lane_pick(gates, h0 + j) for j in heads]
        ig_row = [_row_from_col(x, r, c) for x in ig_col]
        bcr = [_cumsum_col_row(_lane_pick(gates, H_A + h0 + j), r, c) for j in heads]
        cmat = [c_s[j] for j in heads]
        nvec = [n_s[j] for j in heads]
        m_prev = [m_s[j, 0:1, 0:1] for j in heads]
        dmat = [jnp.where(causal, bc - br + ir, -jnp.inf) for (bc, br), ir in zip(bcr, ig_row)]
        inter = [bc + mp for (bc, _), mp in zip(bcr, m_prev)]
        m_t = [jnp.maximum(it, jnp.max(dm, axis=-1, keepdims=True)) for it, dm in zip(inter, dmat)]
        qk = [_dot_nt(a, b) for a, b in zip(q, k)]
        qc = [_dot(a, b) for a, b in zip(q, cmat)]
        s = [x * jnp.exp(dm - mt) for x, dm, mt in zip(qk, dmat, m_t)]
        w_inter = [jnp.exp(it - mt) for it, mt in zip(inter, m_t)]
        sv = [_dot(a, b) for a, b in zip(s, v)]
        num = [wi * a + b for wi, a, b in zip(w_inter, qc, sv)]
        qn = [_dot_nt(a, jnp.broadcast_to(b, (8, HEAD_DIM)))[:, 0:1] for a, b in zip(q, nvec)]
        den = [wi * a + rowsum(x) for wi, a, x in zip(w_inter, qn, s)]
        hout = [a / jnp.maximum(jnp.abs(b), jnp.exp(-mt)) for a, b, mt in zip(num, den, m_t)]
        m_new = [mt[L - 1:L, :] for mt in m_t]
        b_last = [bc[L - 1:L, :] for bc, _ in bcr]
        wk = [jnp.exp(bl - bc + ic - mn) for bl, (bc, _), ic, mn in zip(b_last, bcr, ig_col, m_new)]
        dec = [jnp.exp(bl + mp - mn) for bl, mp, mn in zip(b_last, m_prev, m_new)]
        kv = [_dot_tn(a * w, b) for a, w, b in zip(k, wk, v)]
        wkk = [_dot(jnp.broadcast_to(_row_from_col(w, r, c), (8, L)), a)[0:1, :] for w, a in zip(wk, k)]
        for j in heads:
            c_s[j] = dec[j] * cmat[j] + kv[j]
            n_s[j] = dec[j] * nvec[j] + wkk[j]
            m_s[j] = jnp.broadcast_to(m_new[j], (1, HEAD_DIM))
            y_ref[sl, cols[j]] = _rms(hout[j], na) * _sigmoid(o_ref[sl, cols[j]])
        return carry

    lax.fori_loop(0, n_chunks, chunk, 0, unroll=min(2, n_chunks))

    @pl.when(t == pl.num_programs(2) - 1)
    def _():
        c1_ref[0] = c_s[...]
        n1_ref[0] = n_s[...]
        m1_ref[0] = m_s[...]


def _retention_kernel(q_ref, k_ref, v_ref, g_ref, cos_ref, sin_ref, s0_ref, nb_ref, dm_ref, dv_ref,
                      y_ref, s1_ref, s_s, *, L, n_chunks, n_heads):
    t = pl.program_id(2)

    @pl.when(t == 0)
    def _():
        s_s[...] = s0_ref[0]

    nb = nb_ref[...]
    half = HEAD_DIM // 2
    decays = [(dm_ref[j], dv_ref[j, :, 0:1], dv_ref[j, :, 1:2], dv_ref[j, 0:1, 2:3]) for j in range(n_heads)]

    def rot(x, cos, sin):
        return x * cos + pltpu.roll(x, half, 1) * sin

    def chunk(ci, carry):
        sl = pl.ds(pl.multiple_of(ci * L, L), L)
        cos = cos_ref[sl, :]
        sin = sin_ref[sl, :]
        heads = range(n_heads)
        cols = [slice(j * HEAD_DIM, (j + 1) * HEAD_DIM) for j in heads]
        q = [rot(q_ref[sl, hs], cos, sin) for hs in cols]
        k = [rot(k_ref[sl, hs], cos, sin) * HEAD_DIM ** -0.5 for hs in cols]
        v = [v_ref[sl, hs] for hs in cols]
        smat = [s_s[j] for j in heads]
        qk = [_dot_nt(a, b) for a, b in zip(q, k)]
        qs = [_dot(a, b) for a, b in zip(q, smat)]
        kv = [_dot_tn(a * decays[j][2], b) for j, (a, b) in enumerate(zip(k, v))]
        av = [_dot(x * decays[j][0], b) for j, (x, b) in enumerate(zip(qk, v))]
        for j in heads:
            o = av[j] + decays[j][1] * qs[j]
            s_s[j] = decays[j][3] * smat[j] + kv[j]
            y_ref[sl, cols[j]] = _rms(o, nb) * _silu(g_ref[sl, cols[j]])
        return carry

    lax.fori_loop(0, n_chunks, chunk, 0, unroll=min(2, n_chunks))

    @pl.when(t == pl.num_programs(2) - 1)
    def _():
        s1_ref[0] = s_s[...]


def _unit_lower_inverses(mats, r, c):
    L = mats[0].shape[0]
    eye = jnp.where(r == c, 1.0, 0.0)
    shift = 4
    same = (r >> shift) == (c >> shift)
    ps = [-jnp.where(same, a, 0.0) for a in mats]
    invs = [eye + p for p in ps]
    mm = lambda a, b: jnp.dot(a, b, preferred_element_type=F32)
    for _ in range(3):
        pb = [p.astype(BF16) for p in ps]
        ps = [mm(p, p) for p in pb]
        pb = [p.astype(BF16) for p in ps]
        invs = [v + mm(v.astype(BF16), p) for v, p in zip(invs, pb)]
    while (1 << shift) < L:
        shift += 1
        same2 = (r >> shift) == (c >> shift)
        offs = [jnp.where(same2 & jnp.logical_not(same), a, 0.0).astype(BF16) for a in mats]
        ib = [v.astype(BF16) for v in invs]
        mid = [mm(o, v).astype(BF16) for o, v in zip(offs, ib)]
        invs = [v - mm(vb, m) for v, vb, m in zip(invs, ib, mid)]
        same = same2
    return invs


def _gdn_kernel(q_ref, k_ref, v_ref, gt_ref, gc_ref, cwq_ref, cwk_ref, cwv_ref, hq_ref, hk_ref, hv_ref,
                s0_ref, nc_ref, y_ref, s1_ref, s_s, eq_s, ek_s, ev_s, qn_s, kn_s, vn_s, nq_s, b_s, o0_s,
                *, L, n_chunks, n_heads):
    h0 = pl.program_id(1) * n_heads
    t = pl.program_id(2)
    tb = q_ref.shape[0]
    hist = GDN_HIST_ROWS

    @pl.when(t == 0)
    def _():
        s_s[...] = s0_ref[0]
        eq_s[0:hist, :] = _round_bf16(hq_ref[0])
        ek_s[0:hist, :] = _round_bf16(hk_ref[0])
        ev_s[0:hist, :] = _round_bf16(hv_ref[0])

    def conv_silu(x_ref, e_s, cw_ref):
        e_s[hist:hist + tb, :] = _round_bf16(x_ref[...])
        cw = _round_bf16(cw_ref[...])
        acc = e_s[hist - (CONV_W - 1):hist - (CONV_W - 1) + tb, :] * cw[0:1, :]
        for tap in range(1, CONV_W):
            lo = hist - (CONV_W - 1) + tap
            acc = acc + e_s[lo:lo + tb, :] * cw[tap:tap + 1, :]
        e_s[0:hist, :] = e_s[tb:tb + hist, :]
        return _silu(acc)

    hd = HEAD_DIM
    heads = range(n_heads)
    cols = [slice(j * hd, (j + 1) * hd) for j in heads]

    def l2n(x):
        parts = [x[:, hs] for hs in cols]
        return jnp.concatenate(
            [p * lax.rsqrt(jnp.sum(p * p, axis=-1, keepdims=True) + RMS_EPS) for p in parts], axis=1)

    qn_s[...] = l2n(conv_silu(q_ref, eq_s, cwq_ref)) * HEAD_DIM ** -0.5
    kn_s[...] = l2n(conv_silu(k_ref, ek_s, cwk_ref))
    vn_s[...] = conv_silu(v_ref, ev_s, cwv_ref)

    r, c = _chunk_masks(L)
    lower = r >= c
    strict = r > c
    ncw = nc_ref[...]

    items = [(j, ci) for j in heads for ci in range(n_chunks)]
    rows = [slice(ci * L, (ci + 1) * L) for _, ci in items]
    qs = [qn_s[sl, cols[j]] for sl, (j, _) in zip(rows, items)]
    ks = [kn_s[sl, cols[j]] for sl, (j, _) in zip(rows, items)]
    vs = [vn_s[sl, cols[j]] for sl, (j, _) in zip(rows, items)]
    gates = [gc_ref[sl, :] for sl in rows]
    betas = [_lane_pick(g, h0 + j) for g, (j, _) in zip(gates, items)]
    cums = [_cumsum_col_row(_lane_pick(g, H_C + h0 + j), r, c) for g, (j, _) in zip(gates, items)]
    gams = [jnp.where(lower, jnp.exp(jnp.where(lower, gc - gr, 0.0)), 0.0) for gc, gr in cums]
    kqs = [_dot_nt(jnp.concatenate([k, q], axis=0), k) for k, q in zip(ks, qs)]
    tinvs = _unit_lower_inverses(
        [jnp.where(strict, b * kq[0:L, :] * gam, 0.0) for b, kq, gam in zip(betas, kqs, gams)], r, c)
    egs = [jnp.exp(gc) for gc, _ in cums]
    rhss = [jnp.concatenate([v * b, k * (b * eg)], axis=1) for v, k, b, eg in zip(vs, ks, betas, egs)]
    uws = [_dot(ti, rh) for ti, rh in zip(tinvs, rhss)]
    g_lasts = [gc[L - 1:L, :] for gc, _ in cums]
    bns = [_dot_tn(k * jnp.exp(gl - gc), uw) for k, gl, (gc, _), uw in zip(ks, g_lasts, cums, uws)]
    aos = [_dot(kq[L:2 * L, :] * gam, uw) for kq, gam, uw in zip(kqs, gams, uws)]
    for i in range(len(items)):
        nq_s[i, 0:hd, :] = bns[i][:, hd:2 * hd]
        nq_s[i, hd:hd + L, :] = qs[i] * egs[i] - aos[i][:, hd:2 * hd]
        b_s[i] = bns[i][:, 0:hd]
        o0_s[i] = aos[i][:, 0:hd]
    decays = [jnp.exp(gl) for gl in g_lasts]

    smat = [s_s[j] for j in heads]
    for ci in range(n_chunks):
        sl = slice(ci * L, (ci + 1) * L)
        ids = [j * n_chunks + ci for j in heads]
        ns = [_dot(nq_s[i], s) for i, s in zip(ids, smat)]
        for j, i in enumerate(ids):
            o = ns[j][hd:hd + L, :] + o0_s[i]
            y_ref[sl, cols[j]] = _rms(o, ncw) * _silu(gt_ref[sl, cols[j]])
        smat = [decays[i] * s - n[0:hd, :] + b_s[i] for i, s, n in zip(ids, smat, ns)]
    for j in heads:
        s_s[j] = smat[j]

    @pl.when(t == pl.num_programs(2) - 1)
    def _():
        s1_ref[0] = s_s[...]


def _scan_geometry(t_len, block=None):
    tb = min(block or SCAN_BLOCK, t_len)
    L = min(SCAN_CHUNK, t_len)
    return tb, L, t_len // tb, tb // L


def _col_spec(tb, nt, row0_blocks, col_fn, nh=1):
    return pl.BlockSpec((tb, nh * HEAD_DIM), lambda b, g, t: (row0_blocks + b * nt + t, col_fn(g)))


def _state_spec(shape_tail, nh=1):
    nd = len(shape_tail)
    return pl.BlockSpec((1, nh) + shape_tail, lambda b, g, t: (b, g) + (0,) * nd)


def _mlstm_scan(z, gc, y_prev, row0, bsz, t_len, c0, n0, m0, norm_a):
    tb, L, nt, n_chunks = _scan_geometry(t_len)
    rb = row0 // tb
    hb = HEAD_DIM
    nh = SCAN_HEADS
    ng = H_A // nh
    in_specs = [_col_spec(tb, nt, rb, lambda g: g, nh), _col_spec(tb, nt, rb, lambda g: ng + g, nh),
                _col_spec(tb, nt, rb, lambda g: 2 * ng + g, nh), _col_spec(tb, nt, rb, lambda g: 3 * ng + g, nh),
                pl.BlockSpec((tb, GATE_LANES), lambda b, g, t: (rb + b * nt + t, 0)),
                _state_spec((hb, hb), nh), _state_spec((1, hb), nh), _state_spec((1, hb), nh),
                pl.BlockSpec((1, hb), lambda b, g, t: (0, 0))]
    out_specs = [_col_spec(tb, nt, rb, lambda g: g, nh),
                 _state_spec((hb, hb), nh), _state_spec((1, hb), nh), _state_spec((1, hb), nh)]
    out_shape = [jax.ShapeDtypeStruct(y_prev.shape, F32),
                 jax.ShapeDtypeStruct((bsz, H_A, hb, hb), F32), jax.ShapeDtypeStruct((bsz, H_A, 1, hb), F32),
                 jax.ShapeDtypeStruct((bsz, H_A, 1, hb), F32)]
    m0b = jnp.broadcast_to(m0[:, :, None, None], (bsz, H_A, 1, hb))
    kern = functools.partial(_mlstm_kernel_aliased, L=L, n_chunks=n_chunks, n_heads=nh)
    y, c1, n1, m1 = pl.pallas_call(
        kern, grid=(bsz, ng, nt), in_specs=in_specs + [pl.BlockSpec(memory_space=pl.ANY)],
        out_specs=out_specs, out_shape=out_shape,
        scratch_shapes=[pltpu.VMEM((nh, hb, hb), F32), pltpu.VMEM((nh, 1, hb), F32), pltpu.VMEM((nh, 1, hb), F32)],
        input_output_aliases={9: 0},
        compiler_params=_cparams(("parallel", "parallel", "arbitrary")), name="mlstm_scan",
    )(z, z, z, z, gc, c0, n0[:, :, None, :], m0b, norm_a.reshape(1, hb), y_prev)
    return y, c1, n1[:, :, 0, :], m1[:, :, 0, 0]


def _mlstm_kernel_aliased(*refs, **kw):
    return _mlstm_kernel(*refs[:9], *refs[10:], **kw)


def _retention_scan(z, cos, sin, y_prev, row0, bsz, t_len, s0, norm_b):
    tb, L, nt, n_chunks = _scan_geometry(t_len)
    rb = row0 // tb
    hb = HEAD_DIM
    nh = SCAN_HEADS
    ng = H_B // nh
    base = 4 * (H_A // nh)
    in_specs = [_col_spec(tb, nt, rb, lambda g: base + g, nh), _col_spec(tb, nt, rb, lambda g: base + ng + g, nh),
                _col_spec(tb, nt, rb, lambda g: base + 2 * ng + g, nh),
                _col_spec(tb, nt, rb, lambda g: base + 3 * ng + g, nh),
                pl.BlockSpec((tb, hb), lambda b, g, t: (t, 0)), pl.BlockSpec((tb, hb), lambda b, g, t: (t, 0)),
                _state_spec((hb, hb), nh), pl.BlockSpec((1, hb), lambda b, g, t: (0, 0)),
                pl.BlockSpec((nh, L, L), lambda b, g, t: (g, 0, 0)),
                pl.BlockSpec((nh, L, hb), lambda b, g, t: (g, 0, 0)),
                pl.BlockSpec(memory_space=pl.ANY)]
    out_specs = [_col_spec(tb, nt, rb, lambda g: H_A // nh + g, nh), _state_spec((hb, hb), nh)]
    out_shape = [jax.ShapeDtypeStruct(y_prev.shape, F32), jax.ShapeDtypeStruct((bsz, H_B, hb, hb), F32)]
    lg = jnp.log(1.0 - 2.0 ** (-5.0 - jnp.arange(H_B, dtype=F32)))
    idx = jnp.arange(L, dtype=F32)
    rel = idx[:, None] - idx[None, :]
    causal = rel >= 0
    dmat = jnp.where(causal, jnp.exp(jnp.where(causal, rel, 0.0) * lg[:, None, None]), 0.0)
    q_dec = jnp.exp((idx + 1.0) * lg[:, None])
    k_dec = jnp.exp((L - 1.0 - idx) * lg[:, None])
    c_dec = jnp.broadcast_to(jnp.exp(L * lg)[:, None], (H_B, L))
    dvec = jnp.pad(jnp.stack([q_dec, k_dec, c_dec], axis=-1), ((0, 0), (0, 0), (0, hb - 3)))
    kern = functools.partial(_retention_kernel_aliased, L=L, n_chunks=n_chunks, n_heads=nh)
    return pl.pallas_call(
        kern, grid=(bsz, ng, nt), in_specs=in_specs, out_specs=out_specs, out_shape=out_shape,
        scratch_shapes=[pltpu.VMEM((nh, hb, hb), F32)], input_output_aliases={10: 0},
        compiler_params=_cparams(("parallel", "parallel", "arbitrary")), name="retention_scan",
    )(z, z, z, z, cos, sin, s0, norm_b.reshape(1, hb), dmat, dvec, y_prev)


def _retention_kernel_aliased(*refs, **kw):
    return _retention_kernel(*refs[:10], *refs[11:], **kw)


def _gdn_scan(z, gc, y_prev, row0, bsz, t_len, conv_w, conv_hist, s0, norm_c):
    tb, L, nt, n_chunks = _scan_geometry(t_len, GDN_BLOCK)
    rb = row0 // tb
    hb = HEAD_DIM
    nh = GDN_HEADS
    ng = H_C // nh
    cw_spec = lambda off: pl.BlockSpec((CONV_W, nh * hb), lambda b, g, t: (0, off + g))
    hist_spec = lambda off: pl.BlockSpec((1, GDN_HIST_ROWS, nh * hb), lambda b, g, t: (b, 0, off + g))
    in_specs = [_col_spec(tb, nt, rb, lambda g: g, nh), _col_spec(tb, nt, rb, lambda g: ng + g, nh),
                _col_spec(tb, nt, rb, lambda g: 2 * ng + g, nh), _col_spec(tb, nt, rb, lambda g: 3 * ng + g, nh),
                pl.BlockSpec((tb, GATE_LANES), lambda b, g, t: (rb + b * nt + t, 0)),
                cw_spec(0), cw_spec(ng), cw_spec(2 * ng),
                hist_spec(0), hist_spec(ng), hist_spec(2 * ng),
                _state_spec((hb, hb), nh), pl.BlockSpec((1, hb), lambda b, g, t: (0, 0)),
                pl.BlockSpec(memory_space=pl.ANY)]
    out_specs = [_col_spec(tb, nt, rb, lambda g: g, nh), _state_spec((hb, hb), nh)]
    out_shape = [jax.ShapeDtypeStruct(y_prev.shape, F32), jax.ShapeDtypeStruct((bsz, H_C, hb, hb), F32)]
    kern = functools.partial(_gdn_kernel_aliased, L=L, n_chunks=n_chunks, n_heads=nh)
    ext = pltpu.VMEM((tb + GDN_HIST_ROWS, nh * hb), F32)
    blk = pltpu.VMEM((tb, nh * hb), F32)
    n_items = nh * n_chunks
    return pl.pallas_call(
        kern, grid=(bsz, ng, nt), in_specs=in_specs, out_specs=out_specs, out_shape=out_shape,
        scratch_shapes=[pltpu.VMEM((nh, hb, hb), F32), ext, ext, ext, blk, blk, blk,
                        pltpu.VMEM((n_items, hb + L, hb), F32), pltpu.VMEM((n_items, hb, hb), F32),
                        pltpu.VMEM((n_items, L, hb), F32)],
        input_output_aliases={13: 0},
        compiler_params=_cparams(("parallel", "parallel", "arbitrary")), name="gdn_scan",
    )(z, z, z, z, gc, conv_w, conv_w, conv_w, conv_hist, conv_hist, conv_hist, s0, norm_c.reshape(1, hb), y_prev)


def _gdn_kernel_aliased(*refs, **kw):
    return _gdn_kernel(*refs[:13], *refs[14:], **kw)


def _rope_tables(pos):
    half = HEAD_DIM // 2
    inv = ROPE_BASE ** (-jnp.arange(half, dtype=F32) / half)
    ang = pos.astype(F32)[:, None] * inv[None, :]
    cos, sin = jnp.cos(ang), jnp.sin(ang)
    return jnp.concatenate([cos, cos], axis=-1), jnp.concatenate([-sin, sin], axis=-1)


def _gate_rows(*rows):
    out = jnp.zeros((8, GATE_LANES), F32)
    for i, rvals in enumerate(rows):
        out = out.at[i, :rvals.shape[0]].set(rvals.astype(F32))
    return out


def _pad_cols(w, width):
    return jnp.pad(w, ((0, 0), (0, width - w.shape[1])))


def kernel(x_prompt, x_sample, state_mlstm_c, state_mlstm_n, state_mlstm_m, state_ret, state_gdn, state_gdn_conv, norm_mix, norm_ffn, norm_final, ab_w_in, ab_gate_bias, ab_w_out, ab_norm_a, ab_norm_b, c_w_in, c_conv_w, c_a_log, c_dt_bias, c_norm, c_w_out, moe_w_grp, moe_b_grp, moe_w_exp, moe_b_exp, moe_w_gate, moe_w_up, moe_w_down):
    bp, tp, d = x_prompt.shape
    bs, ts, _ = x_sample.shape
    n_p, n_s = bp * tp, bs * ts
    n = n_p + n_s
    x = (x_prompt.reshape(n_p, d), x_sample.reshape(n_s, d))
    hd = HEAD_DIM
    qkv_a = 4 * H_A * hd
    n_gate_a = 2 * H_A
    c_qkv = 3 * H_C * hd
    n_gate_c = 2 * H_C

    def moe_layer(y, xres, w_out, layer):
        w_router = _pad_cols(jnp.concatenate([moe_w_grp[layer], moe_w_exp[layer]], axis=1), LANES)
        b_router = _gate_rows(jnp.concatenate([moe_b_grp[layer], moe_b_exp[layer]]))
        x1, hmoe, rt, cnt = _out_proj_router(y, xres, w_out, norm_ffn[layer], w_router, b_router, n_p)
        gab, second_row0 = _moe(hmoe, rt, cnt[0], moe_w_gate, moe_w_up, moe_w_down, layer)
        return x1, gab, second_row0, rt

    w_in = ab_w_in[0]
    w_main = jnp.concatenate([w_in[:, :qkv_a], w_in[:, qkv_a + n_gate_a:]], axis=1)
    w_gate = _pad_cols(w_in[:, qkv_a:qkv_a + n_gate_a], GATE_LANES)
    z, gc = _in_proj(x, norm_mix[0], w_main, w_gate, _gate_rows(ab_gate_bias[0]), 0, n_p)
    y = jnp.zeros((n, d), F32)
    zeros = lambda *s: jnp.zeros(s, F32)
    cos_p, sin_p = _rope_tables(jnp.arange(tp))
    cos_s, sin_s = _rope_tables(PAST_LEN + jnp.arange(ts))
    y, c_p, n_pm, m_p = _mlstm_scan(z, gc, y, 0, bp, tp, zeros(bp, H_A, hd, hd), zeros(bp, H_A, hd),
                                    zeros(bp, H_A), ab_norm_a[0])
    y, c_s, n_sm, m_s = _mlstm_scan(z, gc, y, n_p, bs, ts, state_mlstm_c[0], state_mlstm_n[0],
                                    state_mlstm_m[0], ab_norm_a[0])
    y, r_p = _retention_scan(z, cos_p, sin_p, y, 0, bp, tp, zeros(bp, H_B, hd, hd), ab_norm_b[0])
    y, r_s = _retention_scan(z, cos_s, sin_s, y, n_p, bs, ts, state_ret[0], ab_norm_b[0])
    x1, gab, second_row0, rt = moe_layer(y, x, ab_w_out[0], 0)

    w_in = c_w_in[0]
    w_main = jnp.concatenate([w_in[:, :c_qkv], w_in[:, c_qkv + n_gate_c:]], axis=1)
    w_gate = _pad_cols(w_in[:, c_qkv:c_qkv + n_gate_c], GATE_LANES)
    zero8 = jnp.zeros((H_C,), F32)
    gparams = _gate_rows(jnp.concatenate([zero8, c_dt_bias[0]]), jnp.concatenate([zero8, c_a_log[0]]))
    x2, z, gc = _in_proj(x1, norm_mix[1], w_main, w_gate, gparams, 1, n_p, comb=(gab, second_row0, rt))
    hist_p = jnp.zeros((bp, GDN_HIST_ROWS, c_qkv), F32)
    hist_s = jnp.pad(state_gdn_conv[0], ((0, 0), (GDN_HIST_ROWS - (CONV_W - 1), 0), (0, 0)))
    y, g_p = _gdn_scan(z, gc, y, 0, bp, tp, c_conv_w[0], hist_p, zeros(bp, H_C, hd, hd), c_norm[0])
    y, g_s = _gdn_scan(z, gc, y, n_p, bs, ts, c_conv_w[0], hist_s, state_gdn[0], c_norm[0])
    keep = CONV_W - 1
    last_rows = lambda row0, b, t: (row0 + np.arange(b)[:, None] * t + np.arange(t - keep, t)[None, :]).reshape(-1)
    conv_p = jnp.take(z, last_rows(0, bp, tp), axis=0)[:, :c_qkv].reshape(bp, keep, c_qkv)
    conv_s = jnp.take(z, last_rows(n_p, bs, ts), axis=0)[:, :c_qkv].reshape(bs, keep, c_qkv)
    x3, gab, second_row0, rt = moe_layer(y, x2, c_w_out[0], 1)

    yf_p, yf_s = _final_norm(x3, gab, second_row0, rt, norm_final, n_p)
    y_prompt = yf_p.reshape(bp, tp, d)
    y_sample = yf_s.reshape(bs, ts, d)
    return (y_prompt, y_sample, c_p[None], n_pm[None], m_p[None], r_p[None], g_p[None], conv_p[None],
            c_s[None], n_sm[None], m_s[None], r_s[None], g_s[None], conv_s[None])
```

```python
import functools
import math

import jax
import jax.numpy as jnp
import numpy as np
from jax import lax
from jax.experimental import pallas as pl
from jax.experimental.pallas import tpu as pltpu
from jax.experimental.pallas import tpu_sc as plsc

F32 = jnp.float32
BF16 = jnp.bfloat16

D_MODEL = 1024
H_A = 4
H_B = 4
H_C = 8
HEAD_DIM = 128
CONV_W = 4
N_GROUPS = 4
EXP_PER_GROUP = 8
N_EXPERTS = N_GROUPS * EXP_PER_GROUP
D_EXPERT = 512
RMS_EPS = 1e-6
ROPE_BASE = 10000.0
PAST_LEN = 2048

LANES = 128
GATE_LANES = LANES
VMEM_LIMIT = 56 * 1024 * 1024

ROW_TILE = 512
SCAN_BLOCK = 512
SCAN_CHUNK = 64
SCAN_HEADS = 4
GDN_HEADS = 8
GDN_BLOCK = 256
MOE_ROWS = 512
SC_GATHER_BYTES = 256 * 1024
SC_MAX_INDEX_VECTOR = 128
SC_WORKERS = 32
GDN_HIST_ROWS = 8


def _cparams(sem):
    return pltpu.CompilerParams(dimension_semantics=sem, vmem_limit_bytes=VMEM_LIMIT)


def _dot(a, b):
    return jnp.dot(a.astype(BF16), b.astype(BF16), preferred_element_type=F32)


def _dot_nt(a, b):
    return lax.dot_general(a.astype(BF16), b.astype(BF16), (((1,), (1,)), ((), ())),
                           preferred_element_type=F32)


def _dot_tn(a, b):
    return lax.dot_general(a.astype(BF16), b.astype(BF16), (((0,), (0,)), ((), ())),
                           preferred_element_type=F32)


def _pack_bf16_pairs(x):
    w = x.shape[1] // 2
    bits = lax.bitcast_convert_type(_round_bf16(x), jnp.uint32)
    return lax.bitcast_convert_type((bits[:, :w] >> 16) | bits[:, w:], jnp.int32)


def _unpack_bf16_pairs(p):
    bits = lax.bitcast_convert_type(p, jnp.uint32)
    lo = lax.bitcast_convert_type(bits << 16, F32)
    hi = lax.bitcast_convert_type(bits & jnp.uint32(0xFFFF0000), F32)
    return jnp.concatenate([lo, hi], axis=1)


def _round_bf16(x):
    return x.astype(BF16).astype(F32)


def _softplus(x):
    return jnp.maximum(x, 0.0) + jnp.log1p(jnp.exp(-jnp.abs(x)))


def _sigmoid(x):
    return 1.0 / (1.0 + jnp.exp(-x))


def _silu(x):
    return x * _sigmoid(x)


def _rms(x, g):
    return x * lax.rsqrt(jnp.mean(x * x, axis=-1, keepdims=True) + RMS_EPS) * g


def _mean_sq_sublane_order(x):
    rows, d = x.shape
    acc = None
    for c in range(d // LANES):
        xc = x[:, c * LANES:(c + 1) * LANES]
        acc = xc * xc if acc is None else acc + xc * xc
    acc_t = acc.T
    s8 = acc_t[0:8, :]
    for a in range(1, LANES // 8):
        s8 = s8 + acc_t[8 * a:8 * a + 8, :]
    ms_row = jnp.sum(s8, axis=0, keepdims=True) * (1.0 / d)
    r = lax.broadcasted_iota(jnp.int32, (rows, rows), 0)
    c = lax.broadcasted_iota(jnp.int32, (rows, rows), 1)
    return jnp.sum(jnp.where(r == c, ms_row, 0.0), axis=1, keepdims=True)


def _rms_rows(x, g, sublane_order):
    ms = jnp.where(sublane_order, _mean_sq_sublane_order(x), jnp.mean(x * x, axis=-1, keepdims=True))
    return x * lax.rsqrt(ms + RMS_EPS) * g


def _lane_pick(tile, idx):
    lane = lax.broadcasted_iota(jnp.int32, tile.shape, 1)
    return jnp.sum(jnp.where(lane == idx, tile, 0.0), axis=-1, keepdims=True)


def _combine(x_ref, ga_ref, gb_ref, rt_ref):
    rt = rt_ref[...]
    return x_ref[...] + (_lane_pick(rt, 4) * ga_ref[...] + _lane_pick(rt, 5) * gb_ref[...])


def _two_group_specs(tm, d, n_first):
    return [pl.BlockSpec((tm, d), lambda i: (jnp.minimum(i, n_first - 1), 0)),
            pl.BlockSpec((tm, d), lambda i: (jnp.maximum(i - n_first, 0), 0))]


def _in_proj_kernel(*refs, combine, gate_mode, n_col_chunks, col_chunk, n_first):
    first = pl.program_id(0) < n_first
    if combine:
        x_ref, ga_ref, gb_ref, rt_ref = refs[:4]
        refs = refs[4:]
        x = _combine(x_ref, ga_ref, gb_ref, rt_ref)
    else:
        x1_ref, x2_ref = refs[:2]
        refs = refs[2:]
        x = jnp.where(first, x1_ref[...], x2_ref[...])
    g_ref, w_ref, wg_ref, gp_ref = refs[:4]
    outs = refs[4:]
    if combine:
        xo_ref, z_ref, gc_ref = outs
        xo_ref[...] = x
    else:
        z_ref, gc_ref = outs
    xn = _rms_rows(x, g_ref[...], first)
    xh = xn.astype(BF16)
    for c in range(n_col_chunks):
        cs = slice(c * col_chunk, (c + 1) * col_chunk)
        z_ref[:, cs] = jnp.dot(xh, w_ref[:, cs], preferred_element_type=F32)
    raw = jnp.dot(xh, wg_ref[...], preferred_element_type=F32)
    lane = lax.broadcasted_iota(jnp.int32, raw.shape, 1)
    bias = gp_ref[0:1, :]
    if gate_mode == 0:
        val = raw + bias
        gc_ref[...] = jnp.where(lane < H_A, val, -_softplus(-val))
    else:
        neg_a = -jnp.exp(gp_ref[1:2, :])
        gc_ref[...] = jnp.where(lane < H_C, _sigmoid(raw), neg_a * _softplus(raw + bias))


def _in_proj(x, norm_g, w_main, w_gate, gate_params, gate_mode, n_first_rows, comb=None):
    tm = ROW_TILE
    n_first = n_first_rows // tm
    col_chunk = 512
    row = lambda i: (i, 0)
    const = lambda i: (0, 0)
    out_specs = []
    out_shape = []
    once = dict(pipeline_mode=pl.Buffered(1))
    if comb is None:
        n, d = x[0].shape[0] + x[1].shape[0], x[0].shape[1]
        in_specs = _two_group_specs(tm, d, n_first)
        args = list(x)
    else:
        n, d = x.shape
        in_specs = [pl.BlockSpec((tm, d), row)]
        args = [x]
    nz = w_main.shape[1]
    if comb is not None:
        gab, second_row0, rt = comb
        second = second_row0 // tm
        in_specs += [pl.BlockSpec((tm, d), row), pl.BlockSpec((tm, d), lambda i: (second + i, 0)),
                     pl.BlockSpec((tm, LANES), row)]
        args += [gab, gab, rt]
        out_specs.append(pl.BlockSpec((tm, d), row))
        out_shape.append(jax.ShapeDtypeStruct((n, d), F32))
    in_specs += [pl.BlockSpec((1, d), const), pl.BlockSpec((d, nz), const, **once),
                 pl.BlockSpec((d, GATE_LANES), const, **once), pl.BlockSpec((8, GATE_LANES), const)]
    args += [norm_g.reshape(1, d), w_main.astype(BF16), w_gate.astype(BF16), gate_params]
    out_specs += [pl.BlockSpec((tm, nz), row), pl.BlockSpec((tm, GATE_LANES), row)]
    out_shape += [jax.ShapeDtypeStruct((n, nz), F32), jax.ShapeDtypeStruct((n, GATE_LANES), F32)]
    kern = functools.partial(_in_proj_kernel, combine=comb is not None, gate_mode=gate_mode,
                             n_col_chunks=nz // col_chunk, col_chunk=col_chunk, n_first=n_first)
    return pl.pallas_call(
        kern, grid=(n // tm,), in_specs=in_specs, out_specs=out_specs, out_shape=out_shape,
        compiler_params=_cparams(("parallel",)), name=f"in_proj_{gate_mode}")(*args)


def _final_kernel(x_ref, ga_ref, gb_ref, rt_ref, g_ref, y1_ref, y2_ref, *, n_first):
    i = pl.program_id(0)
    y = _rms_rows(_combine(x_ref, ga_ref, gb_ref, rt_ref), g_ref[...], i < n_first)

    @pl.when(i < n_first)
    def _():
        y1_ref[...] = y

    @pl.when(i >= n_first)
    def _():
        y2_ref[...] = y


def _final_norm(x, gab, second_row0, rt, norm_g, n_first_rows):
    n, d = x.shape
    tm = ROW_TILE
    n_first = n_first_rows // tm
    second = second_row0 // tm
    row = lambda i: (i, 0)
    return pl.pallas_call(
        functools.partial(_final_kernel, n_first=n_first), grid=(n // tm,),
        in_specs=[pl.BlockSpec((tm, d), row), pl.BlockSpec((tm, d), row),
                  pl.BlockSpec((tm, d), lambda i: (second + i, 0)),
                  pl.BlockSpec((tm, LANES), row), pl.BlockSpec((1, d), lambda i: (0, 0))],
        out_specs=[pl.BlockSpec((tm, d), lambda i: (jnp.minimum(i, n_first - 1), 0)),
                   pl.BlockSpec((tm, d), lambda i: (jnp.maximum(i - n_first, 0), 0))],
        out_shape=[jax.ShapeDtypeStruct((n_first_rows, d), F32), jax.ShapeDtypeStruct((n - n_first_rows, d), F32)],
        compiler_params=_cparams(("arbitrary",)), name="final_norm")(x, gab, gab, rt, norm_g.reshape(1, d))


def _out_proj_router_kernel(y_ref, *refs, n_first, two_groups):
    i = pl.program_id(0)
    if two_groups:
        xres = jnp.where(i < n_first, refs[0][...], refs[1][...])
        refs = refs[2:]
    else:
        xres = refs[0][...]
        refs = refs[1:]
    w_ref, g_ref, wr_ref, br_ref, x1_ref, h_ref, rt_ref, cnt_ref, carry = refs

    @pl.when(i == 0)
    def _():
        carry[...] = jnp.zeros_like(carry)

    x1 = xres + _dot(y_ref[...], w_ref[...])
    x1_ref[...] = x1
    hn = _rms_rows(x1, g_ref[...], i < n_first)
    h_ref[...] = _pack_bf16_pairs(hn)
    logits = _dot(hn, wr_ref[...]) + br_ref[0:1, :]
    tm = logits.shape[0]
    lane = lax.broadcasted_iota(jnp.int32, logits.shape, 1).astype(F32)
    neg = jnp.float32(-jnp.inf)
    far = jnp.float32(LANES)
    lg = jnp.where(lane < N_GROUPS, logits, neg)
    gmax = jnp.max(lg, axis=-1, keepdims=True)
    grp = jnp.min(jnp.where(lg == gmax, lane, far), axis=-1, keepdims=True)
    p_sel = 1.0 / jnp.sum(jnp.exp(lg - gmax), axis=-1, keepdims=True)
    lo = N_GROUPS + EXP_PER_GROUP * grp
    fm = jnp.where((lane >= lo) & (lane < lo + EXP_PER_GROUP), logits, neg)
    f1 = jnp.max(fm, axis=-1, keepdims=True)
    i1 = jnp.min(jnp.where(fm == f1, lane, far), axis=-1, keepdims=True)
    fm2 = jnp.where(lane == i1, neg, fm)
    f2 = jnp.max(fm2, axis=-1, keepdims=True)
    i2 = jnp.min(jnp.where(fm2 == f2, lane, far), axis=-1, keepdims=True)
    r = jnp.exp(f2 - f1)
    g1 = p_sel / (1.0 + r)
    g2 = p_sel * r / (1.0 + r)
    onehot = jnp.where((lane == i1) | (lane == i2), 1.0, 0.0)
    rr = lax.broadcasted_iota(jnp.int32, (tm, tm), 0)
    cc = lax.broadcasted_iota(jnp.int32, (tm, tm), 1)
    before = jnp.where(rr > cc, 1.0, 0.0).astype(BF16)
    cnt = jnp.dot(before, onehot.astype(BF16), preferred_element_type=F32) + carry[0:1, :]
    rank1 = jnp.sum(jnp.where(lane == i1, cnt, 0.0), axis=-1, keepdims=True)
    rank2 = jnp.sum(jnp.where(lane == i2, cnt, 0.0), axis=-1, keepdims=True)
    new_carry = carry[0:1, :] + jnp.sum(onehot, axis=0, keepdims=True)
    carry[...] = jnp.broadcast_to(new_carry, carry.shape)
    cnt_ref[...] = jnp.broadcast_to(new_carry, cnt_ref.shape)
    e1 = i1 - N_GROUPS
    e2 = i2 - N_GROUPS
    rt = jnp.where(lane == 0, e1, 0.0)
    rt = jnp.where(lane == 1, e2, rt)
    rt = jnp.where(lane == 2, rank1, rt)
    rt = jnp.where(lane == 3, rank2, rt)
    rt = jnp.where(lane == 4, g1, rt)
    rt = jnp.where(lane == 5, g2, rt)
    rt_ref[...] = rt


def _out_proj_router(y, x, w_out, norm_g, w_router, b_router, n_first_rows):
    n, d = y.shape
    tm = ROW_TILE
    n_first = n_first_rows // tm
    row = lambda i: (i, 0)
    const = lambda i: (0, 0)
    once = dict(pipeline_mode=pl.Buffered(1))
    two_groups = isinstance(x, tuple)
    x_specs = _two_group_specs(tm, d, n_first) if two_groups else [pl.BlockSpec((tm, d), row)]
    x_args = list(x) if two_groups else [x]
    return pl.pallas_call(
        functools.partial(_out_proj_router_kernel, n_first=n_first, two_groups=two_groups), grid=(n // tm,),
        in_specs=[pl.BlockSpec((tm, d), row)] + x_specs + [
            pl.BlockSpec((d, d), const, **once), pl.BlockSpec((1, d), const),
            pl.BlockSpec((d, LANES), const, **once), pl.BlockSpec((8, LANES), const)],
        out_specs=[pl.BlockSpec((tm, d), row), pl.BlockSpec((tm, d // 2), row), pl.BlockSpec((tm, LANES), row),
                   pl.BlockSpec((8, LANES), const)],
        out_shape=[jax.ShapeDtypeStruct((n, d), F32), jax.ShapeDtypeStruct((n, d // 2), jnp.int32),
                   jax.ShapeDtypeStruct((n, LANES), F32), jax.ShapeDtypeStruct((8, LANES), F32)],
        scratch_shapes=[pltpu.VMEM((8, LANES), F32)],
        compiler_params=_cparams(("arbitrary",)), name="out_proj_router",
    )(y, *x_args, w_out.astype(BF16), norm_g.reshape(1, d), w_router.astype(BF16), b_router)


def _expert_kernel(be_ref, nv_ref, x_ref, wg_ref, wu_ref, wd_ref, o_ref, wg_s, wu_s, wd_s):
    i = pl.program_id(0)
    nv = nv_ref[i]

    @pl.when((i == 0) | (be_ref[i] != be_ref[jnp.maximum(i - 1, 0)]))
    def _():
        wg_s[...] = wg_ref[0, 0].astype(BF16)
        wu_s[...] = wu_ref[0, 0].astype(BF16)
        wd_s[...] = wd_ref[0, 0].astype(BF16)

    @pl.when(nv > 0)
    def _():
        rows = lax.broadcasted_iota(jnp.int32, (x_ref.shape[0], 1), 0)
        packed = jnp.where(rows < nv, x_ref[...], 0)
        xb = _unpack_bf16_pairs(packed).astype(BF16)
        hg = jnp.dot(xb, wg_s[...], preferred_element_type=F32)
        hu = jnp.dot(xb, wu_s[...], preferred_element_type=F32)
        hb = (_silu(hg) * hu).astype(BF16)
        o_ref[...] = jnp.dot(hb, wd_s[...], preferred_element_type=F32)

    @pl.when(nv == 0)
    def _():
        o_ref[...] = jnp.zeros_like(o_ref)


def _experts(rows, block_exp, block_valid, w_gate, w_up, w_down, layer):
    r, half_d = rows.shape
    d = 2 * half_d
    bm = MOE_ROWS
    nblk = r // bm
    grid_spec = pltpu.PrefetchScalarGridSpec(
        num_scalar_prefetch=2, grid=(nblk,),
        in_specs=[pl.BlockSpec((bm, half_d), lambda i, be, nv: (i, 0)),
                  pl.BlockSpec((1, 1, d, D_EXPERT), lambda i, be, nv: (layer, be[i], 0, 0)),
                  pl.BlockSpec((1, 1, d, D_EXPERT), lambda i, be, nv: (layer, be[i], 0, 0)),
                  pl.BlockSpec((1, 1, D_EXPERT, d), lambda i, be, nv: (layer, be[i], 0, 0))],
        out_specs=pl.BlockSpec((bm, d), lambda i, be, nv: (i, 0)),
        scratch_shapes=[pltpu.VMEM((d, D_EXPERT), BF16), pltpu.VMEM((d, D_EXPERT), BF16),
                        pltpu.VMEM((D_EXPERT, d), BF16)])
    return pl.pallas_call(
        _expert_kernel, grid_spec=grid_spec, out_shape=jax.ShapeDtypeStruct((r, d), F32),
        compiler_params=_cparams(("arbitrary",)), name="experts",
    )(block_exp, block_valid, rows, w_gate, w_up, w_down)


def _sc_chunk_rows(width, dtype):
    return min(SC_MAX_INDEX_VECTOR, SC_GATHER_BYTES // (width * jnp.dtype(dtype).itemsize))


def _gather_rows(table, idx):
    info = plsc.get_sparse_core_info()
    n_workers = info.num_cores * info.num_subcores
    n_rows = idx.shape[0]
    width = table.shape[1]
    ch = _sc_chunk_rows(width, table.dtype)
    per_worker = n_rows // n_workers
    n_chunks = per_worker // ch
    assert n_workers * n_chunks * ch == n_rows
    mesh = plsc.VectorSubcoreMesh(core_axis_name="c", subcore_axis_name="s")

    @functools.partial(
        pl.kernel, mesh=mesh, out_type=jax.ShapeDtypeStruct((n_rows, width), table.dtype),
        scratch_types=[pltpu.VMEM((ch,), jnp.int32), pltpu.VMEM((ch, width), table.dtype),
                       pltpu.SemaphoreType.DMA])
    def gather(table_hbm, idx_hbm, out_hbm, idx_v, rows_v, sem):
        wid = lax.axis_index("s") * info.num_cores + lax.axis_index("c")
        base = wid * per_worker

        @pl.loop(0, n_chunks)
        def _(j):
            off = pl.multiple_of(base + j * ch, ch)
            pltpu.sync_copy(idx_hbm.at[pl.ds(off, ch)], idx_v)
            pltpu.async_copy(table_hbm.at[idx_v], rows_v, sem).wait()
            pltpu.sync_copy(rows_v, out_hbm.at[pl.ds(off, ch)])

    return gather(table, idx)


def _pad_to(n, m):
    return -(-n // m) * m


def _moe(h, rt, counts_row, w_gate, w_up, w_down, layer):
    n, half_d = h.shape
    bm = MOE_ROWS
    sc_quant = SC_WORKERS * _sc_chunk_rows(half_d, h.dtype)
    counts = counts_row[N_GROUPS:N_GROUPS + N_EXPERTS].astype(jnp.int32)
    e = rt[:, 0:2].astype(jnp.int32)
    rank = rt[:, 2:4].astype(jnp.int32)
    padded = ((counts + bm - 1) // bm) * bm
    pend = jnp.cumsum(padded)
    pstart = pend - padded
    start = jnp.cumsum(counts) - counts
    n_rows = _pad_to(_pad_to(2 * n, bm) + N_EXPERTS * bm, sc_quant)
    n_rows = _pad_to(n_rows, bm)
    nblk = n_rows // bm
    blk_start = jnp.arange(nblk, dtype=jnp.int32) * bm
    block_exp = jnp.minimum(jnp.sum(blk_start[:, None] >= pend[None, :], axis=1), N_EXPERTS - 1).astype(jnp.int32)
    blk_off = blk_start - pstart[block_exp]
    block_valid = jnp.clip(counts[block_exp] - blk_off, 0, bm).astype(jnp.int32)
    order = jnp.argsort(e.reshape(-1), stable=True).astype(jnp.int32)
    tok_sorted = order // 2
    within = jnp.arange(bm, dtype=jnp.int32)[None, :]
    pos = jnp.clip((start[block_exp] + blk_off)[:, None] + within, 0, 2 * n - 1)
    filler = (blk_start[:, None] + within) % n
    src = jnp.where(within < block_valid[:, None], tok_sorted[pos], filler).reshape(-1)
    rows = _gather_rows(h, src)
    out_rows = _experts(rows, block_exp, block_valid, w_gate, w_up, w_down, layer)
    experts_iota = jnp.arange(N_EXPERTS, dtype=jnp.int32)
    dest = jnp.sum(jnp.where(e[:, :, None] == experts_iota, pstart, 0), axis=-1) + rank
    n_pad = _pad_to(n, SC_WORKERS * _sc_chunk_rows(out_rows.shape[1], out_rows.dtype))
    dest = jnp.concatenate([dest, jnp.arange(2 * (n_pad - n), dtype=jnp.int32).reshape(-1, 2)], axis=0)
    return _gather_rows(out_rows, dest.T.reshape(-1)), n_pad


def _chunk_masks(L):
    r = lax.broadcasted_iota(jnp.int32, (L, L), 0)
    c = lax.broadcasted_iota(jnp.int32, (L, L), 1)
    return r, c


def _row_from_col(col, r, c):
    return jnp.sum(jnp.where(r == c, col, 0.0), axis=0, keepdims=True)


def _cumsum_col_row(col, r, c):
    row = _row_from_col(col, r, c)
    cs_col = jnp.sum(jnp.where(r >= c, row, 0.0), axis=1, keepdims=True)
    cs_row = jnp.sum(jnp.where(r <= c, col, 0.0), axis=0, keepdims=True)
    return cs_col, cs_row


def _mlstm_kernel(q_ref, k_ref, v_ref, o_ref, gc_ref, c0_ref, n0_ref, m0_ref, na_ref,
                  y_ref, c1_ref, n1_ref, m1_ref, c_s, n_s, m_s, *, L, n_chunks, n_heads):
    h0 = pl.program_id(1) * n_heads
    t = pl.program_id(2)

    @pl.when(t == 0)
    def _():
        c_s[...] = c0_ref[0]
        n_s[...] = n0_ref[0]
        m_s[...] = m0_ref[0]

    r, c = _chunk_masks(L)
    causal = r >= c
    na = na_ref[...]

    heads = range(n_heads)
    cols = [slice(j * HEAD_DIM, (j + 1) * HEAD_DIM) for j in heads]
    rowsum = lambda x: jnp.sum(x, axis=-1, keepdims=True)

    def chunk(ci, carry):
        sl = pl.ds(pl.multiple_of(ci * L, L), L)
        gates = gc_ref[sl, :]
        q = [q_ref[sl, hs] for hs in cols]
        k = [k_ref[sl, hs] * HEAD_DIM ** -0.5 for hs in cols]
        v = [v_ref[sl, hs] for hs in cols]
        ig_col = [_lane_pick(gates, h0 + j) for j in heads]
        ig_row = [_row_from_col(x, r, c) for x in ig_col]
        bcr = [_cumsum_col_row(_lane_pick(gates, H_A + h0 + j), r, c) for j in heads]
        cmat = [c_s[j] for j in heads]
        nvec = [n_s[j] for j in heads]
        m_prev = [m_s[j, 0:1, 0:1] for j in heads]
        dmat = [jnp.where(causal, bc - br + ir, -jnp.inf) for (bc, br), ir in zip(bcr, ig_row)]
        inter = [bc + mp for (bc, _), mp in zip(bcr, m_prev)]
        m_t = [jnp.maximum(it, jnp.max(dm, axis=-1, keepdims=True)) for it, dm in zip(inter, dmat)]
        qk = [_dot_nt(a, b) for a, b in zip(q, k)]
        qc = [_dot(a, b) for a, b in zip(q, cmat)]
        s = [x * jnp.exp(dm - mt) for x, dm, mt in zip(qk, dmat, m_t)]
        w_inter = [jnp.exp(it - mt) for it, mt in zip(inter, m_t)]
        sv = [_dot(a, b) for a, b in zip(s, v)]
        num = [wi * a + b for wi, a, b in zip(w_inter, qc, sv)]
        qn = [_dot_nt(a, jnp.broadcast_to(b, (8, HEAD_DIM)))[:, 0:1] for a, b in zip(q, nvec)]
        den = [wi * a + rowsum(x) for wi, a, x in zip(w_inter, qn, s)]
        hout = [a / jnp.maximum(jnp.abs(b), jnp.exp(-mt)) for a, b, mt in zip(num, den, m_t)]
        m_new = [mt[L - 1:L, :] for mt in m_t]
        b_last = [bc[L - 1:L, :] for bc, _ in bcr]
        wk = [jnp.exp(bl - bc + ic - mn) for bl, (bc, _), ic, mn in zip(b_last, bcr, ig_col, m_new)]
        dec = [jnp.exp(bl + mp - mn) for bl, mp, mn in zip(b_last, m_prev, m_new)]
        kv = [_dot_tn(a * w, b) for a, w, b in zip(k, wk, v)]
        wkk = [_dot(jnp.broadcast_to(_row_from_col(w, r, c), (8, L)), a)[0:1, :] for w, a in zip(wk, k)]
        for j in heads:
            c_s[j] = dec[j] * cmat[j] + kv[j]
            n_s[j] = dec[j] * nvec[j] + wkk[j]
            m_s[j] = jnp.broadcast_to(m_new[j], (1, HEAD_DIM))
            y_ref[sl, cols[j]] = _rms(hout[j], na) * _sigmoid(o_ref[sl, cols[j]])
        return carry

    lax.fori_loop(0, n_chunks, chunk, 0)

    @pl.when(t == pl.num_programs(2) - 1)
    def _():
        c1_ref[0] = c_s[...]
        n1_ref[0] = n_s[...]
        m1_ref[0] = m_s[...]


def _retention_kernel(q_ref, k_ref, v_ref, g_ref, cos_ref, sin_ref, s0_ref, nb_ref, dm_ref, dv_ref,
                      y_ref, s1_ref, s_s, *, L, n_chunks, n_heads):
    t = pl.program_id(2)

    @pl.when(t == 0)
    def _():
        s_s[...] = s0_ref[0]

    nb = nb_ref[...]
    half = HEAD_DIM // 2
    decays = [(dm_ref[j], dv_ref[j, :, 0:1], dv_ref[j, :, 1:2], dv_ref[j, 0:1, 2:3]) for j in range(n_heads)]

    def rot(x, cos, sin):
        return x * cos + pltpu.roll(x, half, 1) * sin

    def chunk(ci, carry):
        sl = pl.ds(pl.multiple_of(ci * L, L), L)
        cos = cos_ref[sl, :]
        sin = sin_ref[sl, :]
        heads = range(n_heads)
        cols = [slice(j * HEAD_DIM, (j + 1) * HEAD_DIM) for j in heads]
        q = [rot(q_ref[sl, hs], cos, sin) for hs in cols]
        k = [rot(k_ref[sl, hs], cos, sin) * HEAD_DIM ** -0.5 for hs in cols]
        v = [v_ref[sl, hs] for hs in cols]
        smat = [s_s[j] for j in heads]
        qk = [_dot_nt(a, b) for a, b in zip(q, k)]
        qs = [_dot(a, b) for a, b in zip(q, smat)]
        kv = [_dot_tn(a * decays[j][2], b) for j, (a, b) in enumerate(zip(k, v))]
        av = [_dot(x * decays[j][0], b) for j, (x, b) in enumerate(zip(qk, v))]
        for j in heads:
            o = av[j] + decays[j][1] * qs[j]
            s_s[j] = decays[j][3] * smat[j] + kv[j]
            y_ref[sl, cols[j]] = _rms(o, nb) * _silu(g_ref[sl, cols[j]])
        return carry

    lax.fori_loop(0, n_chunks, chunk, 0, unroll=min(2, n_chunks))

    @pl.when(t == pl.num_programs(2) - 1)
    def _():
        s1_ref[0] = s_s[...]


def _unit_lower_inverses(mats, r, c):
    L = mats[0].shape[0]
    eye = jnp.where(r == c, 1.0, 0.0)
    shift = 4
    same = (r >> shift) == (c >> shift)
    ps = [-jnp.where(same, a, 0.0) for a in mats]
    invs = [eye + p for p in ps]
    mm = lambda a, b: jnp.dot(a, b, preferred_element_type=F32)
    for _ in range(3):
        pb = [p.astype(BF16) for p in ps]
        ps = [mm(p, p) for p in pb]
        pb = [p.astype(BF16) for p in ps]
        invs = [v + mm(v.astype(BF16), p) for v, p in zip(invs, pb)]
    while (1 << shift) < L:
        shift += 1
        same2 = (r >> shift) == (c >> shift)
        offs = [jnp.where(same2 & jnp.logical_not(same), a, 0.0).astype(BF16) for a in mats]
        ib = [v.astype(BF16) for v in invs]
        mid = [mm(o, v).astype(BF16) for o, v in zip(offs, ib)]
        invs = [v - mm(vb, m) for v, vb, m in zip(invs, ib, mid)]
        same = same2
    return invs


def _gdn_kernel(q_ref, k_ref, v_ref, gt_ref, gc_ref, cwq_ref, cwk_ref, cwv_ref, hq_ref, hk_ref, hv_ref,
                s0_ref, nc_ref, y_ref, s1_ref, s_s, eq_s, ek_s, ev_s, qn_s, kn_s, vn_s, nq_s, b_s, o0_s,
                *, L, n_chunks, n_heads):
    h0 = pl.program_id(1) * n_heads
    t = pl.program_id(2)
    tb = q_ref.shape[0]
    hist = GDN_HIST_ROWS

    @pl.when(t == 0)
    def _():
        s_s[...] = s0_ref[0]
        eq_s[0:hist, :] = _round_bf16(hq_ref[0])
        ek_s[0:hist, :] = _round_bf16(hk_ref[0])
        ev_s[0:hist, :] = _round_bf16(hv_ref[0])

    def conv_silu(x_ref, e_s, cw_ref):
        e_s[hist:hist + tb, :] = _round_bf16(x_ref[...])
        cw = _round_bf16(cw_ref[...])
        acc = e_s[hist - (CONV_W - 1):hist - (CONV_W - 1) + tb, :] * cw[0:1, :]
        for tap in range(1, CONV_W):
            lo = hist - (CONV_W - 1) + tap
            acc = acc + e_s[lo:lo + tb, :] * cw[tap:tap + 1, :]
        e_s[0:hist, :] = e_s[tb:tb + hist, :]
        return _silu(acc)

    hd = HEAD_DIM
    heads = range(n_heads)
    cols = [slice(j * hd, (j + 1) * hd) for j in heads]

    def l2n(x):
        parts = [x[:, hs] for hs in cols]
        return jnp.concatenate(
            [p * lax.rsqrt(jnp.sum(p * p, axis=-1, keepdims=True) + RMS_EPS) for p in parts], axis=1)

    qn_s[...] = l2n(conv_silu(q_ref, eq_s, cwq_ref)) * HEAD_DIM ** -0.5
    kn_s[...] = l2n(conv_silu(k_ref, ek_s, cwk_ref))
    vn_s[...] = conv_silu(v_ref, ev_s, cwv_ref)

    r, c = _chunk_masks(L)
    lower = r >= c
    strict = r > c
    ncw = nc_ref[...]

    items = [(j, ci) for j in heads for ci in range(n_chunks)]
    rows = [slice(ci * L, (ci + 1) * L) for _, ci in items]
    qs = [qn_s[sl, cols[j]] for sl, (j, _) in zip(rows, items)]
    ks = [kn_s[sl, cols[j]] for sl, (j, _) in zip(rows, items)]
    vs = [vn_s[sl, cols[j]] for sl, (j, _) in zip(rows, items)]
    gates = [gc_ref[sl, :] for sl in rows]
    betas = [_lane_pick(g, h0 + j) for g, (j, _) in zip(gates, items)]
    cums = [_cumsum_col_row(_lane_pick(g, H_C + h0 + j), r, c) for g, (j, _) in zip(gates, items)]
    gams = [jnp.where(lower, jnp.exp(jnp.where(lower, gc - gr, 0.0)), 0.0) for gc, gr in cums]
    kqs = [_dot_nt(jnp.concatenate([k, q], axis=0), k) for k, q in zip(ks, qs)]
    tinvs = _unit_lower_inverses(
        [jnp.where(strict, b * kq[0:L, :] * gam, 0.0) for b, kq, gam in zip(betas, kqs, gams)], r, c)
    egs = [jnp.exp(gc) for gc, _ in cums]
    rhss = [jnp.concatenate([v * b, k * (b * eg)], axis=1) for v, k, b, eg in zip(vs, ks, betas, egs)]
    uws = [_dot(ti, rh) for ti, rh in zip(tinvs, rhss)]
    g_lasts = [gc[L - 1:L, :] for gc, _ in cums]
    bns = [_dot_tn(k * jnp.exp(gl - gc), uw) for k, gl, (gc, _), uw in zip(ks, g_lasts, cums, uws)]
    aos = [_dot(kq[L:2 * L, :] * gam, uw) for kq, gam, uw in zip(kqs, gams, uws)]
    for i in range(len(items)):
        nq_s[i, 0:hd, :] = bns[i][:, hd:2 * hd]
        nq_s[i, hd:hd + L, :] = qs[i] * egs[i] - aos[i][:, hd:2 * hd]
        b_s[i] = bns[i][:, 0:hd]
        o0_s[i] = aos[i][:, 0:hd]
    decays = [jnp.exp(gl) for gl in g_lasts]

    smat = [s_s[j] for j in heads]
    for ci in range(n_chunks):
        sl = slice(ci * L, (ci + 1) * L)
        ids = [j * n_chunks + ci for j in heads]
        ns = [_dot(nq_s[i], s) for i, s in zip(ids, smat)]
        for j, i in enumerate(ids):
            o = ns[j][hd:hd + L, :] + o0_s[i]
            y_ref[sl, cols[j]] = _rms(o, ncw) * _silu(gt_ref[sl, cols[j]])
        smat = [decays[i] * s - n[0:hd, :] + b_s[i] for i, s, n in zip(ids, smat, ns)]
    for j in heads:
        s_s[j] = smat[j]

    @pl.when(t == pl.num_programs(2) - 1)
    def _():
        s1_ref[0] = s_s[...]


def _scan_geometry(t_len, block=None):
    tb = min(block or SCAN_BLOCK, t_len)
    L = min(SCAN_CHUNK, t_len)
    return tb, L, t_len // tb, tb // L


def _col_spec(tb, nt, row0_blocks, col_fn, nh=1):
    return pl.BlockSpec((tb, nh * HEAD_DIM), lambda b, g, t: (row0_blocks + b * nt + t, col_fn(g)))


def _state_spec(shape_tail, nh=1):
    nd = len(shape_tail)
    return pl.BlockSpec((1, nh) + shape_tail, lambda b, g, t: (b, g) + (0,) * nd)


def _mlstm_scan(z, gc, y_prev, row0, bsz, t_len, c0, n0, m0, norm_a):
    tb, L, nt, n_chunks = _scan_geometry(t_len)
    rb = row0 // tb
    hb = HEAD_DIM
    nh = SCAN_HEADS
    ng = H_A // nh
    in_specs = [_col_spec(tb, nt, rb, lambda g: g, nh), _col_spec(tb, nt, rb, lambda g: ng + g, nh),
                _col_spec(tb, nt, rb, lambda g: 2 * ng + g, nh), _col_spec(tb, nt, rb, lambda g: 3 * ng + g, nh),
                pl.BlockSpec((tb, GATE_LANES), lambda b, g, t: (rb + b * nt + t, 0)),
                _state_spec((hb, hb), nh), _state_spec((1, hb), nh), _state_spec((1, hb), nh),
                pl.BlockSpec((1, hb), lambda b, g, t: (0, 0))]
    out_specs = [_col_spec(tb, nt, rb, lambda g: g, nh),
                 _state_spec((hb, hb), nh), _state_spec((1, hb), nh), _state_spec((1, hb), nh)]
    out_shape = [jax.ShapeDtypeStruct(y_prev.shape, F32),
                 jax.ShapeDtypeStruct((bsz, H_A, hb, hb), F32), jax.ShapeDtypeStruct((bsz, H_A, 1, hb), F32),
                 jax.ShapeDtypeStruct((bsz, H_A, 1, hb), F32)]
    m0b = jnp.broadcast_to(m0[:, :, None, None], (bsz, H_A, 1, hb))
    kern = functools.partial(_mlstm_kernel_aliased, L=L, n_chunks=n_chunks, n_heads=nh)
    y, c1, n1, m1 = pl.pallas_call(
        kern, grid=(bsz, ng, nt), in_specs=in_specs + [pl.BlockSpec(memory_space=pl.ANY)],
        out_specs=out_specs, out_shape=out_shape,
        scratch_shapes=[pltpu.VMEM((nh, hb, hb), F32), pltpu.VMEM((nh, 1, hb), F32), pltpu.VMEM((nh, 1, hb), F32)],
        input_output_aliases={9: 0},
        compiler_params=_cparams(("parallel", "parallel", "arbitrary")), name="mlstm_scan",
    )(z, z, z, z, gc, c0, n0[:, :, None, :], m0b, norm_a.reshape(1, hb), y_prev)
    return y, c1, n1[:, :, 0, :], m1[:, :, 0, 0]


def _mlstm_kernel_aliased(*refs, **kw):
    return _mlstm_kernel(*refs[:9], *refs[10:], **kw)


def _retention_scan(z, cos, sin, y_prev, row0, bsz, t_len, s0, norm_b):
    tb, L, nt, n_chunks = _scan_geometry(t_len)
    rb = row0 // tb
    hb = HEAD_DIM
    nh = SCAN_HEADS
    ng = H_B // nh
    base = 4 * (H_A // nh)
    in_specs = [_col_spec(tb, nt, rb, lambda g: base + g, nh), _col_spec(tb, nt, rb, lambda g: base + ng + g, nh),
                _col_spec(tb, nt, rb, lambda g: base + 2 * ng + g, nh),
                _col_spec(tb, nt, rb, lambda g: base + 3 * ng + g, nh),
                pl.BlockSpec((tb, hb), lambda b, g, t: (t, 0)), pl.BlockSpec((tb, hb), lambda b, g, t: (t, 0)),
                _state_spec((hb, hb), nh), pl.BlockSpec((1, hb), lambda b, g, t: (0, 0)),
                pl.BlockSpec((nh, L, L), lambda b, g, t: (g, 0, 0)),
                pl.BlockSpec((nh, L, hb), lambda b, g, t: (g, 0, 0)),
                pl.BlockSpec(memory_space=pl.ANY)]
    out_specs = [_col_spec(tb, nt, rb, lambda g: H_A // nh + g, nh), _state_spec((hb, hb), nh)]
    out_shape = [jax.ShapeDtypeStruct(y_prev.shape, F32), jax.ShapeDtypeStruct((bsz, H_B, hb, hb), F32)]
    lg = jnp.log(1.0 - 2.0 ** (-5.0 - jnp.arange(H_B, dtype=F32)))
    idx = jnp.arange(L, dtype=F32)
    rel = idx[:, None] - idx[None, :]
    causal = rel >= 0
    dmat = jnp.where(causal, jnp.exp(jnp.where(causal, rel, 0.0) * lg[:, None, None]), 0.0)
    q_dec = jnp.exp((idx + 1.0) * lg[:, None])
    k_dec = jnp.exp((L - 1.0 - idx) * lg[:, None])
    c_dec = jnp.broadcast_to(jnp.exp(L * lg)[:, None], (H_B, L))
    dvec = jnp.pad(jnp.stack([q_dec, k_dec, c_dec], axis=-1), ((0, 0), (0, 0), (0, hb - 3)))
    kern = functools.partial(_retention_kernel_aliased, L=L, n_chunks=n_chunks, n_heads=nh)
    return pl.pallas_call(
        kern, grid=(bsz, ng, nt), in_specs=in_specs, out_specs=out_specs, out_shape=out_shape,
        scratch_shapes=[pltpu.VMEM((nh, hb, hb), F32)], input_output_aliases={10: 0},
        compiler_params=_cparams(("parallel", "parallel", "arbitrary")), name="retention_scan",
    )(z, z, z, z, cos, sin, s0, norm_b.reshape(1, hb), dmat, dvec, y_prev)


def _retention_kernel_aliased(*refs, **kw):
    return _retention_kernel(*refs[:10], *refs[11:], **kw)


def _gdn_scan(z, gc, y_prev, row0, bsz, t_len, conv_w, conv_hist, s0, norm_c):
    tb, L, nt, n_chunks = _scan_geometry(t_len, GDN_BLOCK)
    rb = row0 // tb
    hb = HEAD_DIM
    nh = GDN_HEADS
    ng = H_C // nh
    cw_spec = lambda off: pl.BlockSpec((CONV_W, nh * hb), lambda b, g, t: (0, off + g))
    hist_spec = lambda off: pl.BlockSpec((1, GDN_HIST_ROWS, nh * hb), lambda b, g, t: (b, 0, off + g))
    in_specs = [_col_spec(tb, nt, rb, lambda g: g, nh), _col_spec(tb, nt, rb, lambda g: ng + g, nh),
                _col_spec(tb, nt, rb, lambda g: 2 * ng + g, nh), _col_spec(tb, nt, rb, lambda g: 3 * ng + g, nh),
                pl.BlockSpec((tb, GATE_LANES), lambda b, g, t: (rb + b * nt + t, 0)),
                cw_spec(0), cw_spec(ng), cw_spec(2 * ng),
                hist_spec(0), hist_spec(ng), hist_spec(2 * ng),
                _state_spec((hb, hb), nh), pl.BlockSpec((1, hb), lambda b, g, t: (0, 0)),
                pl.BlockSpec(memory_space=pl.ANY)]
    out_specs = [_col_spec(tb, nt, rb, lambda g: g, nh), _state_spec((hb, hb), nh)]
    out_shape = [jax.ShapeDtypeStruct(y_prev.shape, F32), jax.ShapeDtypeStruct((bsz, H_C, hb, hb), F32)]
    kern = functools.partial(_gdn_kernel_aliased, L=L, n_chunks=n_chunks, n_heads=nh)
    ext = pltpu.VMEM((tb + GDN_HIST_ROWS, nh * hb), F32)
    blk = pltpu.VMEM((tb, nh * hb), F32)
    n_items = nh * n_chunks
    return pl.pallas_call(
        kern, grid=(bsz, ng, nt), in_specs=in_specs, out_specs=out_specs, out_shape=out_shape,
        scratch_shapes=[pltpu.VMEM((nh, hb, hb), F32), ext, ext, ext, blk, blk, blk,
                        pltpu.VMEM((n_items, hb + L, hb), F32), pltpu.VMEM((n_items, hb, hb), F32),
                        pltpu.VMEM((n_items, L, hb), F32)],
        input_output_aliases={13: 0},
        compiler_params=_cparams(("parallel", "parallel", "arbitrary")), name="gdn_scan",
    )(z, z, z, z, gc, conv_w, conv_w, conv_w, conv_hist, conv_hist, conv_hist, s0, norm_c.reshape(1, hb), y_prev)


def _gdn_kernel_aliased(*refs, **kw):
    return _gdn_kernel(*refs[:13], *refs[14:], **kw)


def _rope_tables(pos):
    half = HEAD_DIM // 2
    inv = ROPE_BASE ** (-jnp.arange(half, dtype=F32) / half)
    ang = pos.astype(F32)[:, None] * inv[None, :]
    cos, sin = jnp.cos(ang), jnp.sin(ang)
    return jnp.concatenate([cos, cos], axis=-1), jnp.concatenate([-sin, sin], axis=-1)


def _gate_rows(*rows):
    out = jnp.zeros((8, GATE_LANES), F32)
    for i, rvals in enumerate(rows):
        out = out.at[i, :rvals.shape[0]].set(rvals.astype(F32))
    return out


def _pad_cols(w, width):
    return jnp.pad(w, ((0, 0), (0, width - w.shape[1])))


def kernel(x_prompt, x_sample, state_mlstm_c, state_mlstm_n, state_mlstm_m, state_ret, state_gdn, state_gdn_conv, norm_mix, norm_ffn, norm_final, ab_w_in, ab_gate_bias, ab_w_out, ab_norm_a, ab_norm_b, c_w_in, c_conv_w, c_a_log, c_dt_bias, c_norm, c_w_out, moe_w_grp, moe_b_grp, moe_w_exp, moe_b_exp, moe_w_gate, moe_w_up, moe_w_down):
    bp, tp, d = x_prompt.shape
    bs, ts, _ = x_sample.shape
    n_p, n_s = bp * tp, bs * ts
    n = n_p + n_s
    x = (x_prompt.reshape(n_p, d), x_sample.reshape(n_s, d))
    hd = HEAD_DIM
    qkv_a = 4 * H_A * hd
    n_gate_a = 2 * H_A
    c_qkv = 3 * H_C * hd
    n_gate_c = 2 * H_C

    def moe_layer(y, xres, w_out, layer):
        w_router = _pad_cols(jnp.concatenate([moe_w_grp[layer], moe_w_exp[layer]], axis=1), LANES)
        b_router = _gate_rows(jnp.concatenate([moe_b_grp[layer], moe_b_exp[layer]]))
        x1, hmoe, rt, cnt = _out_proj_router(y, xres, w_out, norm_ffn[layer], w_router, b_router, n_p)
        gab, second_row0 = _moe(hmoe, rt, cnt[0], moe_w_gate, moe_w_up, moe_w_down, layer)
        return x1, gab, second_row0, rt

    w_in = ab_w_in[0]
    w_main = jnp.concatenate([w_in[:, :qkv_a], w_in[:, qkv_a + n_gate_a:]], axis=1)
    w_gate = _pad_cols(w_in[:, qkv_a:qkv_a + n_gate_a], GATE_LANES)
    z, gc = _in_proj(x, norm_mix[0], w_main, w_gate, _gate_rows(ab_gate_bias[0]), 0, n_p)
    y = jnp.zeros((n, d), F32)
    zeros = lambda *s: jnp.zeros(s, F32)
    cos_p, sin_p = _rope_tables(jnp.arange(tp))
    cos_s, sin_s = _rope_tables(PAST_LEN + jnp.arange(ts))
    y, c_p, n_pm, m_p = _mlstm_scan(z, gc, y, 0, bp, tp, zeros(bp, H_A, hd, hd), zeros(bp, H_A, hd),
                                    zeros(bp, H_A), ab_norm_a[0])
    y, c_s, n_sm, m_s = _mlstm_scan(z, gc, y, n_p, bs, ts, state_mlstm_c[0], state_mlstm_n[0],
                                    state_mlstm_m[0], ab_norm_a[0])
    y, r_p = _retention_scan(z, cos_p, sin_p, y, 0, bp, tp, zeros(bp, H_B, hd, hd), ab_norm_b[0])
    y, r_s = _retention_scan(z, cos_s, sin_s, y, n_p, bs, ts, state_ret[0], ab_norm_b[0])
    x1, gab, second_row0, rt = moe_layer(y, x, ab_w_out[0], 0)

    w_in = c_w_in[0]
    w_main = jnp.concatenate([w_in[:, :c_qkv], w_in[:, c_qkv + n_gate_c:]], axis=1)
    w_gate = _pad_cols(w_in[:, c_qkv:c_qkv + n_gate_c], GATE_LANES)
    zero8 = jnp.zeros((H_C,), F32)
    gparams = _gate_rows(jnp.concatenate([zero8, c_dt_bias[0]]), jnp.concatenate([zero8, c_a_log[0]]))
    x2, z, gc = _in_proj(x1, norm_mix[1], w_main, w_gate, gparams, 1, n_p, comb=(gab, second_row0, rt))
    hist_p = jnp.zeros((bp, GDN_HIST_ROWS, c_qkv), F32)
    hist_s = jnp.pad(state_gdn_conv[0], ((0, 0), (GDN_HIST_ROWS - (CONV_W - 1), 0), (0, 0)))
    y, g_p = _gdn_scan(z, gc, y, 0, bp, tp, c_conv_w[0], hist_p, zeros(bp, H_C, hd, hd), c_norm[0])
    y, g_s = _gdn_scan(z, gc, y, n_p, bs, ts, c_conv_w[0], hist_s, state_gdn[0], c_norm[0])
    keep = CONV_W - 1
    last_rows = lambda row0, b, t: (row0 + np.arange(b)[:, None] * t + np.arange(t - keep, t)[None, :]).reshape(-1)
    conv_p = jnp.take(z, last_rows(0, bp, tp), axis=0)[:, :c_qkv].reshape(bp, keep, c_qkv)
    conv_s = jnp.take(z, last_rows(n_p, bs, ts), axis=0)[:, :c_qkv].reshape(bs, keep, c_qkv)
    x3, gab, second_row0, rt = moe_layer(y, x2, c_w_out[0], 1)

    yf_p, yf_s = _final_norm(x3, gab, second_row0, rt, norm_final, n_p)
    y_prompt = yf_p.reshape(bp, tp, d)
    y_sample = yf_s.reshape(bs, ts, d)
    return (y_prompt, y_sample, c_p[None], n_pm[None], m_p[None], r_p[None], g_p[None], conv_p[None],
            c_s[None], n_sm[None], m_s[None], r_s[None], g_s[None], conv_s[None])
```

```python
import functools
import math

import jax
import jax.numpy as jnp
import numpy as np
from jax import lax
from jax.experimental import pallas as pl
from jax.experimental.pallas import tpu as pltpu
from jax.experimental.pallas import tpu_sc as plsc

F32 = jnp.float32
BF16 = jnp.bfloat16

D_MODEL = 1024
H_A = 4
H_B = 4
H_C = 8
HEAD_DIM = 128
CONV_W = 4
N_GROUPS = 4
EXP_PER_GROUP = 8
N_EXPERTS = N_GROUPS * EXP_PER_GROUP
D_EXPERT = 512
RMS_EPS = 1e-6
ROPE_BASE = 10000.0
PAST_LEN = 2048

LANES = 128
GATE_LANES = LANES
VMEM_LIMIT = 56 * 1024 * 1024

ROW_TILE = 512
SCAN_BLOCK = 512
SCAN_CHUNK = 64
SCAN_HEADS = 4
GDN_HEADS = 8
GDN_BLOCK = 256
MOE_ROWS = 512
MOE_STAGES = 2
SC_GATHER_BYTES = 256 * 1024
SC_MAX_INDEX_VECTOR = 128
SC_WORKERS = 32
GDN_HIST_ROWS = 8


def _cparams(sem):
    return pltpu.CompilerParams(dimension_semantics=sem, vmem_limit_bytes=VMEM_LIMIT)


def _dot(a, b):
    return jnp.dot(a.astype(BF16), b.astype(BF16), preferred_element_type=F32)


def _dot_nt(a, b):
    return lax.dot_general(a.astype(BF16), b.astype(BF16), (((1,), (1,)), ((), ())),
                           preferred_element_type=F32)


def _dot_tn(a, b):
    return lax.dot_general(a.astype(BF16), b.astype(BF16), (((0,), (0,)), ((), ())),
                           preferred_element_type=F32)


def _pack_bf16_pairs(x):
    w = x.shape[1] // 2
    bits = lax.bitcast_convert_type(_round_bf16(x), jnp.uint32)
    return lax.bitcast_convert_type((bits[:, :w] >> 16) | bits[:, w:], jnp.int32)


def _unpack_bf16_pairs(p):
    bits = lax.bitcast_convert_type(p, jnp.uint32)
    lo = lax.bitcast_convert_type(bits << 16, F32)
    hi = lax.bitcast_convert_type(bits & jnp.uint32(0xFFFF0000), F32)
    return jnp.concatenate([lo, hi], axis=1)


def _round_bf16(x):
    return x.astype(BF16).astype(F32)


def _softplus(x):
    return jnp.maximum(x, 0.0) + jnp.log1p(jnp.exp(-jnp.abs(x)))


def _sigmoid(x):
    return 1.0 / (1.0 + jnp.exp(-x))


def _silu(x):
    return x * _sigmoid(x)


def _rms(x, g):
    return x * lax.rsqrt(jnp.mean(x * x, axis=-1, keepdims=True) + RMS_EPS) * g


def _mean_sq_sublane_order(x):
    rows, d = x.shape
    acc = None
    for c in range(d // LANES):
        xc = x[:, c * LANES:(c + 1) * LANES]
        acc = xc * xc if acc is None else acc + xc * xc
    acc_t = acc.T
    s8 = acc_t[0:8, :]
    for a in range(1, LANES // 8):
        s8 = s8 + acc_t[8 * a:8 * a + 8, :]
    ms_row = jnp.sum(s8, axis=0, keepdims=True) * (1.0 / d)
    r = lax.broadcasted_iota(jnp.int32, (rows, rows), 0)
    c = lax.broadcasted_iota(jnp.int32, (rows, rows), 1)
    return jnp.sum(jnp.where(r == c, ms_row, 0.0), axis=1, keepdims=True)


def _rms_rows(x, g, sublane_order):
    ms = jnp.where(sublane_order, _mean_sq_sublane_order(x), jnp.mean(x * x, axis=-1, keepdims=True))
    return x * lax.rsqrt(ms + RMS_EPS) * g


def _lane_pick(tile, idx):
    lane = lax.broadcasted_iota(jnp.int32, tile.shape, 1)
    return jnp.sum(jnp.where(lane == idx, tile, 0.0), axis=-1, keepdims=True)


def _combine(x_ref, ga_ref, gb_ref, rt_ref):
    rt = rt_ref[...]
    return x_ref[...] + (_lane_pick(rt, 4) * ga_ref[...] + _lane_pick(rt, 5) * gb_ref[...])


def _two_group_specs(tm, d, n_first):
    return [pl.BlockSpec((tm, d), lambda i: (jnp.minimum(i, n_first - 1), 0)),
            pl.BlockSpec((tm, d), lambda i: (jnp.maximum(i - n_first, 0), 0))]


def _in_proj_kernel(*refs, combine, gate_mode, n_col_chunks, col_chunk, n_first):
    first = pl.program_id(0) < n_first
    if combine:
        x_ref, ga_ref, gb_ref, rt_ref = refs[:4]
        refs = refs[4:]
        x = _combine(x_ref, ga_ref, gb_ref, rt_ref)
    else:
        x1_ref, x2_ref = refs[:2]
        refs = refs[2:]
        x = jnp.where(first, x1_ref[...], x2_ref[...])
    g_ref, w_ref, wg_ref, gp_ref = refs[:4]
    outs = refs[4:]
    if combine:
        xo_ref, z_ref, gc_ref = outs
        xo_ref[...] = x
    else:
        z_ref, gc_ref = outs
    xn = _rms_rows(x, g_ref[...], first)
    xh = xn.astype(BF16)
    for c in range(n_col_chunks):
        cs = slice(c * col_chunk, (c + 1) * col_chunk)
        z_ref[:, cs] = jnp.dot(xh, w_ref[:, cs], preferred_element_type=F32)
    raw = jnp.dot(xh, wg_ref[...], preferred_element_type=F32)
    lane = lax.broadcasted_iota(jnp.int32, raw.shape, 1)
    bias = gp_ref[0:1, :]
    if gate_mode == 0:
        val = raw + bias
        gc_ref[...] = jnp.where(lane < H_A, val, -_softplus(-val))
    else:
        neg_a = -jnp.exp(gp_ref[1:2, :])
        gc_ref[...] = jnp.where(lane < H_C, _sigmoid(raw), neg_a * _softplus(raw + bias))


def _in_proj(x, norm_g, w_main, w_gate, gate_params, gate_mode, n_first_rows, comb=None):
    tm = ROW_TILE
    n_first = n_first_rows // tm
    col_chunk = 512
    row = lambda i: (i, 0)
    const = lambda i: (0, 0)
    out_specs = []
    out_shape = []
    once = dict(pipeline_mode=pl.Buffered(1))
    if comb is None:
        n, d = x[0].shape[0] + x[1].shape[0], x[0].shape[1]
        in_specs = _two_group_specs(tm, d, n_first)
        args = list(x)
    else:
        n, d = x.shape
        in_specs = [pl.BlockSpec((tm, d), row)]
        args = [x]
    nz = w_main.shape[1]
    if comb is not None:
        gab, second_row0, rt = comb
        second = second_row0 // tm
        in_specs += [pl.BlockSpec((tm, d), row), pl.BlockSpec((tm, d), lambda i: (second + i, 0)),
                     pl.BlockSpec((tm, LANES), row)]
        args += [gab, gab, rt]
        out_specs.append(pl.BlockSpec((tm, d), row))
        out_shape.append(jax.ShapeDtypeStruct((n, d), F32))
    in_specs += [pl.BlockSpec((1, d), const), pl.BlockSpec((d, nz), const, **once),
                 pl.BlockSpec((d, GATE_LANES), const, **once), pl.BlockSpec((8, GATE_LANES), const)]
    args += [norm_g.reshape(1, d), w_main.astype(BF16), w_gate.astype(BF16), gate_params]
    out_specs += [pl.BlockSpec((tm, nz), row), pl.BlockSpec((tm, GATE_LANES), row)]
    out_shape += [jax.ShapeDtypeStruct((n, nz), F32), jax.ShapeDtypeStruct((n, GATE_LANES), F32)]
    kern = functools.partial(_in_proj_kernel, combine=comb is not None, gate_mode=gate_mode,
                             n_col_chunks=nz // col_chunk, col_chunk=col_chunk, n_first=n_first)
    return pl.pallas_call(
        kern, grid=(n // tm,), in_specs=in_specs, out_specs=out_specs, out_shape=out_shape,
        compiler_params=_cparams(("parallel",)), name=f"in_proj_{gate_mode}")(*args)


def _final_kernel(x_ref, ga_ref, gb_ref, rt_ref, g_ref, y1_ref, y2_ref, *, n_first):
    i = pl.program_id(0)
    y = _rms_rows(_combine(x_ref, ga_ref, gb_ref, rt_ref), g_ref[...], i < n_first)

    @pl.when(i < n_first)
    def _():
        y1_ref[...] = y

    @pl.when(i >= n_first)
    def _():
        y2_ref[...] = y


def _final_norm(x, gab, second_row0, rt, norm_g, n_first_rows):
    n, d = x.shape
    tm = ROW_TILE
    n_first = n_first_rows // tm
    second = second_row0 // tm
    row = lambda i: (i, 0)
    return pl.pallas_call(
        functools.partial(_final_kernel, n_first=n_first), grid=(n // tm,),
        in_specs=[pl.BlockSpec((tm, d), row), pl.BlockSpec((tm, d), row),
                  pl.BlockSpec((tm, d), lambda i: (second + i, 0)),
                  pl.BlockSpec((tm, LANES), row), pl.BlockSpec((1, d), lambda i: (0, 0))],
        out_specs=[pl.BlockSpec((tm, d), lambda i: (jnp.minimum(i, n_first - 1), 0)),
                   pl.BlockSpec((tm, d), lambda i: (jnp.maximum(i - n_first, 0), 0))],
        out_shape=[jax.ShapeDtypeStruct((n_first_rows, d), F32), jax.ShapeDtypeStruct((n - n_first_rows, d), F32)],
        compiler_params=_cparams(("arbitrary",)), name="final_norm")(x, gab, gab, rt, norm_g.reshape(1, d))


def _out_proj_router_kernel(y_ref, *refs, n_first, two_groups):
    i = pl.program_id(0)
    if two_groups:
        xres = jnp.where(i < n_first, refs[0][...], refs[1][...])
        refs = refs[2:]
    else:
        xres = refs[0][...]
        refs = refs[1:]
    w_ref, g_ref, wr_ref, br_ref, x1_ref, h_ref, rt_ref, cnt_ref, carry = refs

    @pl.when(i == 0)
    def _():
        carry[...] = jnp.zeros_like(carry)

    x1 = xres + _dot(y_ref[...], w_ref[...])
    x1_ref[...] = x1
    hn = _rms_rows(x1, g_ref[...], i < n_first)
    h_ref[...] = _pack_bf16_pairs(hn)
    logits = _dot(hn, wr_ref[...]) + br_ref[0:1, :]
    tm = logits.shape[0]
    lane = lax.broadcasted_iota(jnp.int32, logits.shape, 1).astype(F32)
    neg = jnp.float32(-jnp.inf)
    far = jnp.float32(LANES)
    lg = jnp.where(lane < N_GROUPS, logits, neg)
    gmax = jnp.max(lg, axis=-1, keepdims=True)
    grp = jnp.min(jnp.where(lg == gmax, lane, far), axis=-1, keepdims=True)
    p_sel = 1.0 / jnp.sum(jnp.exp(lg - gmax), axis=-1, keepdims=True)
    lo = N_GROUPS + EXP_PER_GROUP * grp
    fm = jnp.where((lane >= lo) & (lane < lo + EXP_PER_GROUP), logits, neg)
    f1 = jnp.max(fm, axis=-1, keepdims=True)
    i1 = jnp.min(jnp.where(fm == f1, lane, far), axis=-1, keepdims=True)
    fm2 = jnp.where(lane == i1, neg, fm)
    f2 = jnp.max(fm2, axis=-1, keepdims=True)
    i2 = jnp.min(jnp.where(fm2 == f2, lane, far), axis=-1, keepdims=True)
    r = jnp.exp(f2 - f1)
    g1 = p_sel / (1.0 + r)
    g2 = p_sel * r / (1.0 + r)
    onehot = jnp.where((lane == i1) | (lane == i2), 1.0, 0.0)
    rr = lax.broadcasted_iota(jnp.int32, (tm, tm), 0)
    cc = lax.broadcasted_iota(jnp.int32, (tm, tm), 1)
    before = jnp.where(rr > cc, 1.0, 0.0).astype(BF16)
    cnt = jnp.dot(before, onehot.astype(BF16), preferred_element_type=F32) + carry[0:1, :]
    rank1 = jnp.sum(jnp.where(lane == i1, cnt, 0.0), axis=-1, keepdims=True)
    rank2 = jnp.sum(jnp.where(lane == i2, cnt, 0.0), axis=-1, keepdims=True)
    new_carry = carry[0:1, :] + jnp.sum(onehot, axis=0, keepdims=True)
    carry[...] = jnp.broadcast_to(new_carry, carry.shape)
    cnt_ref[...] = jnp.broadcast_to(new_carry, cnt_ref.shape)
    e1 = i1 - N_GROUPS
    e2 = i2 - N_GROUPS
    rt = jnp.where(lane == 0, e1, 0.0)
    rt = jnp.where(lane == 1, e2, rt)
    rt = jnp.where(lane == 2, rank1, rt)
    rt = jnp.where(lane == 3, rank2, rt)
    rt = jnp.where(lane == 4, g1, rt)
    rt = jnp.where(lane == 5, g2, rt)
    rt_ref[...] = rt


def _out_proj_router(y, x, w_out, norm_g, w_router, b_router, n_first_rows):
    n, d = y.shape
    tm = ROW_TILE
    n_first = n_first_rows // tm
    row = lambda i: (i, 0)
    const = lambda i: (0, 0)
    once = dict(pipeline_mode=pl.Buffered(1))
    two_groups = isinstance(x, tuple)
    x_specs = _two_group_specs(tm, d, n_first) if two_groups else [pl.BlockSpec((tm, d), row)]
    x_args = list(x) if two_groups else [x]
    return pl.pallas_call(
        functools.partial(_out_proj_router_kernel, n_first=n_first, two_groups=two_groups), grid=(n // tm,),
        in_specs=[pl.BlockSpec((tm, d), row)] + x_specs + [
            pl.BlockSpec((d, d), const, **once), pl.BlockSpec((1, d), const),
            pl.BlockSpec((d, LANES), const, **once), pl.BlockSpec((8, LANES), const)],
        out_specs=[pl.BlockSpec((tm, d), row), pl.BlockSpec((tm, d // 2), row), pl.BlockSpec((tm, LANES), row),
                   pl.BlockSpec((8, LANES), const)],
        out_shape=[jax.ShapeDtypeStruct((n, d), F32), jax.ShapeDtypeStruct((n, d // 2), jnp.int32),
                   jax.ShapeDtypeStruct((n, LANES), F32), jax.ShapeDtypeStruct((8, LANES), F32)],
        scratch_shapes=[pltpu.VMEM((8, LANES), F32)],
        compiler_params=_cparams(("arbitrary",)), name="out_proj_router",
    )(y, *x_args, w_out.astype(BF16), norm_g.reshape(1, d), w_router.astype(BF16), b_router)


def _expert_kernel(be_ref, nv_ref, x_ref, wg_ref, wu_ref, wd_ref, o_ref, wg_s, wu_s, wd_s):
    i = pl.program_id(0)
    nv = nv_ref[i]

    @pl.when((i == 0) | (be_ref[i] != be_ref[jnp.maximum(i - 1, 0)]))
    def _():
        wg_s[...] = wg_ref[0, 0].astype(BF16)
        wu_s[...] = wu_ref[0, 0].astype(BF16)
        wd_s[...] = wd_ref[0, 0].astype(BF16)

    @pl.when(nv > 0)
    def _():
        rows = lax.broadcasted_iota(jnp.int32, (x_ref.shape[0], 1), 0)
        packed = jnp.where(rows < nv, x_ref[...], 0)
        xb = _unpack_bf16_pairs(packed).astype(BF16)
        hg = jnp.dot(xb, wg_s[...], preferred_element_type=F32)
        hu = jnp.dot(xb, wu_s[...], preferred_element_type=F32)
        hb = (_silu(hg) * hu).astype(BF16)
        o_ref[...] = jnp.dot(hb, wd_s[...], preferred_element_type=F32)

    @pl.when(nv == 0)
    def _():
        o_ref[...] = jnp.zeros_like(o_ref)


def _expert_kernel_aliased(be_ref, nv_ref, x_ref, wg_ref, wu_ref, wd_ref, prev_ref, o_ref, wg_s, wu_s, wd_s):
    _expert_kernel(be_ref, nv_ref, x_ref, wg_ref, wu_ref, wd_ref, o_ref, wg_s, wu_s, wd_s)


def _experts(rows, block_exp, block_valid, w_gate, w_up, w_down, layer, total_rows, blk0, out_prev=None):
    r, half_d = rows.shape
    d = 2 * half_d
    bm = MOE_ROWS
    nblk = r // bm
    in_specs = [pl.BlockSpec((bm, half_d), lambda i, be, nv: (i, 0)),
                pl.BlockSpec((1, 1, d, D_EXPERT), lambda i, be, nv: (layer, be[i], 0, 0)),
                pl.BlockSpec((1, 1, d, D_EXPERT), lambda i, be, nv: (layer, be[i], 0, 0)),
                pl.BlockSpec((1, 1, D_EXPERT, d), lambda i, be, nv: (layer, be[i], 0, 0))]
    args = [block_exp, block_valid, rows, w_gate, w_up, w_down]
    kern, aliases = _expert_kernel, {}
    if out_prev is not None:
        in_specs.append(pl.BlockSpec(memory_space=pl.ANY))
        args.append(out_prev)
        kern, aliases = _expert_kernel_aliased, {len(args) - 1: 0}
    grid_spec = pltpu.PrefetchScalarGridSpec(
        num_scalar_prefetch=2, grid=(nblk,), in_specs=in_specs,
        out_specs=pl.BlockSpec((bm, d), lambda i, be, nv: (blk0 + i, 0)),
        scratch_shapes=[pltpu.VMEM((d, D_EXPERT), BF16), pltpu.VMEM((d, D_EXPERT), BF16),
                        pltpu.VMEM((D_EXPERT, d), BF16)])
    return pl.pallas_call(
        kern, grid_spec=grid_spec, out_shape=jax.ShapeDtypeStruct((total_rows, d), F32),
        input_output_aliases=aliases,
        compiler_params=_cparams(("arbitrary",)), name="experts",
    )(*args)


def _sc_chunk_rows(width, dtype):
    return min(SC_MAX_INDEX_VECTOR, SC_GATHER_BYTES // (width * jnp.dtype(dtype).itemsize))


def _gather_rows(table, idx):
    info = plsc.get_sparse_core_info()
    n_workers = info.num_cores * info.num_subcores
    n_rows = idx.shape[0]
    width = table.shape[1]
    ch = _sc_chunk_rows(width, table.dtype)
    per_worker = n_rows // n_workers
    n_chunks = per_worker // ch
    assert n_workers * n_chunks * ch == n_rows
    mesh = plsc.VectorSubcoreMesh(core_axis_name="c", subcore_axis_name="s")

    @functools.partial(
        pl.kernel, mesh=mesh, out_type=jax.ShapeDtypeStruct((n_rows, width), table.dtype),
        scratch_types=[pltpu.VMEM((ch,), jnp.int32), pltpu.VMEM((ch, width), table.dtype),
                       pltpu.SemaphoreType.DMA])
    def gather(table_hbm, idx_hbm, out_hbm, idx_v, rows_v, sem):
        wid = lax.axis_index("s") * info.num_cores + lax.axis_index("c")
        base = wid * per_worker

        @pl.loop(0, n_chunks)
        def _(j):
            off = pl.multiple_of(base + j * ch, ch)
            pltpu.sync_copy(idx_hbm.at[pl.ds(off, ch)], idx_v)
            pltpu.async_copy(table_hbm.at[idx_v], rows_v, sem).wait()
            pltpu.sync_copy(rows_v, out_hbm.at[pl.ds(off, ch)])

    return gather(table, idx)


def _pad_to(n, m):
    return -(-n // m) * m


def _moe(h, rt, counts_row, w_gate, w_up, w_down, layer):
    n, half_d = h.shape
    bm = MOE_ROWS
    sc_quant = SC_WORKERS * _sc_chunk_rows(half_d, h.dtype)
    counts = counts_row[N_GROUPS:N_GROUPS + N_EXPERTS].astype(jnp.int32)
    e = rt[:, 0:2].astype(jnp.int32)
    rank = rt[:, 2:4].astype(jnp.int32)
    padded = ((counts + bm - 1) // bm) * bm
    pend = jnp.cumsum(padded)
    pstart = pend - padded
    start = jnp.cumsum(counts) - counts
    n_rows = _pad_to(_pad_to(2 * n, bm) + N_EXPERTS * bm, MOE_STAGES * sc_quant)
    n_rows = _pad_to(n_rows, MOE_STAGES * bm)
    nblk = n_rows // bm
    blk_start = jnp.arange(nblk, dtype=jnp.int32) * bm
    block_exp = jnp.minimum(jnp.sum(blk_start[:, None] >= pend[None, :], axis=1), N_EXPERTS - 1).astype(jnp.int32)
    blk_off = blk_start - pstart[block_exp]
    block_valid = jnp.clip(counts[block_exp] - blk_off, 0, bm).astype(jnp.int32)
    order = jnp.argsort(e.reshape(-1), stable=True).astype(jnp.int32)
    tok_sorted = order // 2
    within = jnp.arange(bm, dtype=jnp.int32)[None, :]
    pos = jnp.clip((start[block_exp] + blk_off)[:, None] + within, 0, 2 * n - 1)
    filler = (blk_start[:, None] + within) % n
    src = jnp.where(within < block_valid[:, None], tok_sorted[pos], filler).reshape(-1)
    stage_rows = n_rows // MOE_STAGES
    stage_blks = nblk // MOE_STAGES
    out_rows = None
    for s in range(MOE_STAGES):
        rows = _gather_rows(h, src[s * stage_rows:(s + 1) * stage_rows])
        out_rows = _experts(rows, block_exp[s * stage_blks:(s + 1) * stage_blks],
                            block_valid[s * stage_blks:(s + 1) * stage_blks], w_gate, w_up, w_down, layer,
                            n_rows, s * stage_blks, out_rows)
    experts_iota = jnp.arange(N_EXPERTS, dtype=jnp.int32)
    dest = jnp.sum(jnp.where(e[:, :, None] == experts_iota, pstart, 0), axis=-1) + rank
    n_pad = _pad_to(n, SC_WORKERS * _sc_chunk_rows(out_rows.shape[1], out_rows.dtype))
    dest = jnp.concatenate([dest, jnp.arange(2 * (n_pad - n), dtype=jnp.int32).reshape(-1, 2)], axis=0)
    return _gather_rows(out_rows, dest.T.reshape(-1)), n_pad


def _chunk_masks(L):
    r = lax.broadcasted_iota(jnp.int32, (L, L), 0)
    c = lax.broadcasted_iota(jnp.int32, (L, L), 1)
    return r, c


def _row_from_col(col, r, c):
    return jnp.sum(jnp.where(r == c, col, 0.0), axis=0, keepdims=True)


def _cumsum_col_row(col, r, c):
    row = _row_from_col(col, r, c)
    cs_col = jnp.sum(jnp.where(r >= c, row, 0.0), axis=1, keepdims=True)
    cs_row = jnp.sum(jnp.where(r <= c, col, 0.0), axis=0, keepdims=True)
    return cs_col, cs_row


def _mlstm_kernel(q_ref, k_ref, v_ref, o_ref, gc_ref, c0_ref, n0_ref, m0_ref, na_ref,
                  y_ref, c1_ref, n1_ref, m1_ref, c_s, n_s, m_s, *, L, n_chunks, n_heads):
    h0 = pl.program_id(1) * n_heads
    t = pl.program_id(2)

    @pl.when(t == 0)
    def _():
        c_s[...] = c0_ref[0]
        n_s[...] = n0_ref[0]
        m_s[...] = m0_ref[0]

    r, c = _chunk_masks(L)
    causal = r >= c
    na = na_ref[...]

    heads = range(n_heads)
    cols = [slice(j * HEAD_DIM, (j + 1) * HEAD_DIM) for j in heads]
    rowsum = lambda x: jnp.sum(x, axis=-1, keepdims=True)

    def chunk(ci, carry):
        sl = pl.ds(pl.multiple_of(ci * L, L), L)
        gates = gc_ref[sl, :]
        q = [q_ref[sl, hs] for hs in cols]
        k = [k_ref[sl, hs] * HEAD_DIM ** -0.5 for hs in cols]
        v = [v_ref[sl, hs] for hs in cols]
        ig_col = [_lane_pick(gates, h0 + j) for j in heads]
        ig_row = [_row_from_col(x, r, c) for x in ig_col]
        bcr = [_cumsum_col_row(_lane_pick(gates, H_A + h0 + j), r, c) for j in heads]
        cmat = [c_s[j] for j in heads]
        nvec = [n_s[j] for j in heads]
        m_prev = [m_s[j, 0:1, 0:1] for j in heads]
        dmat = [jnp.where(causal, bc - br + ir, -jnp.inf) for (bc, br), ir in zip(bcr, ig_row)]
        inter = [bc + mp for (bc, _), mp in zip(bcr, m_prev)]
        m_t = [jnp.maximum(it, jnp.max(dm, axis=-1, keepdims=True)) for it, dm in zip(inter, dmat)]
        qk = [_dot_nt(a, b) for a, b in zip(q, k)]
        qc = [_dot(a, b) for a, b in zip(q, cmat)]
        s = [x * jnp.exp(dm - mt) for x, dm, mt in zip(qk, dmat, m_t)]
        w_inter = [jnp.exp(it - mt) for it, mt in zip(inter, m_t)]
        sv = [_dot(a, b) for a, b in zip(s, v)]
        num = [wi * a + b for wi, a, b in zip(w_inter, qc, sv)]
        qn = [_dot_nt(a, jnp.broadcast_to(b, (8, HEAD_DIM)))[:, 0:1] for a, b in zip(q, nvec)]
        den = [wi * a + rowsum(x) for wi, a, x in zip(w_inter, qn, s)]
        hout = [a / jnp.maximum(jnp.abs(b), jnp.exp(-mt)) for a, b, mt in zip(num, den, m_t)]
        m_new = [mt[L - 1:L, :] for mt in m_t]
        b_last = [bc[L - 1:L, :] for bc, _ in bcr]
        wk = [jnp.exp(bl - bc + ic - mn) for bl, (bc, _), ic, mn in zip(b_last, bcr, ig_col, m_new)]
        dec = [jnp.exp(bl + mp - mn) for bl, mp, mn in zip(b_last, m_prev, m_new)]
        kv = [_dot_tn(a * w, b) for a, w, b in zip(k, wk, v)]
        wkk = [_dot(jnp.broadcast_to(_row_from_col(w, r, c), (8, L)), a)[0:1, :] for w, a in zip(wk, k)]
        for j in heads:
            c_s[j] = dec[j] * cmat[j] + kv[j]
            n_s[j] = dec[j] * nvec[j] + wkk[j]
            m_s[j] = jnp.broadcast_to(m_new[j], (1, HEAD_DIM))
            y_ref[sl, cols[j]] = _rms(hout[j], na) * _sigmoid(o_ref[sl, cols[j]])
        return carry

    lax.fori_loop(0, n_chunks, chunk, 0)

    @pl.when(t == pl.num_programs(2) - 1)
    def _():
        c1_ref[0] = c_s[...]
        n1_ref[0] = n_s[...]
        m1_ref[0] = m_s[...]


def _retention_kernel(q_ref, k_ref, v_ref, g_ref, cos_ref, sin_ref, s0_ref, nb_ref, dm_ref, dv_ref,
                      y_ref, s1_ref, s_s, *, L, n_chunks, n_heads):
    t = pl.program_id(2)

    @pl.when(t == 0)
    def _():
        s_s[...] = s0_ref[0]

    nb = nb_ref[...]
    half = HEAD_DIM // 2
    decays = [(dm_ref[j], dv_ref[j, :, 0:1], dv_ref[j, :, 1:2], dv_ref[j, 0:1, 2:3]) for j in range(n_heads)]

    def rot(x, cos, sin):
        return x * cos + pltpu.roll(x, half, 1) * sin

    def chunk(ci, carry):
        sl = pl.ds(pl.multiple_of(ci * L, L), L)
        cos = cos_ref[sl, :]
        sin = sin_ref[sl, :]
        heads = range(n_heads)
        cols = [slice(j * HEAD_DIM, (j + 1) * HEAD_DIM) for j in heads]
        q = [rot(q_ref[sl, hs], cos, sin) for hs in cols]
        k = [rot(k_ref[sl, hs], cos, sin) * HEAD_DIM ** -0.5 for hs in cols]
        v = [v_ref[sl, hs] for hs in cols]
        smat = [s_s[j] for j in heads]
        qk = [_dot_nt(a, b) for a, b in zip(q, k)]
        qs = [_dot(a, b) for a, b in zip(q, smat)]
        kv = [_dot_tn(a * decays[j][2], b) for j, (a, b) in enumerate(zip(k, v))]
        av = [_dot(x * decays[j][0], b) for j, (x, b) in enumerate(zip(qk, v))]
        for j in heads:
            o = av[j] + decays[j][1] * qs[j]
            s_s[j] = decays[j][3] * smat[j] + kv[j]
            y_ref[sl, cols[j]] = _rms(o, nb) * _silu(g_ref[sl, cols[j]])
        return carry

    lax.fori_loop(0, n_chunks, chunk, 0, unroll=min(2, n_chunks))

    @pl.when(t == pl.num_programs(2) - 1)
    def _():
        s1_ref[0] = s_s[...]


def _unit_lower_inverses(mats, r, c):
    L = mats[0].shape[0]
    eye = jnp.where(r == c, 1.0, 0.0)
    shift = 4
    same = (r >> shift) == (c >> shift)
    ps = [-jnp.where(same, a, 0.0) for a in mats]
    invs = [eye + p for p in ps]
    mm = lambda a, b: jnp.dot(a, b, preferred_element_type=F32)
    for _ in range(3):
        pb = [p.astype(BF16) for p in ps]
        ps = [mm(p, p) for p in pb]
        pb = [p.astype(BF16) for p in ps]
        invs = [v + mm(v.astype(BF16), p) for v, p in zip(invs, pb)]
    while (1 << shift) < L:
        shift += 1
        same2 = (r >> shift) == (c >> shift)
        offs = [jnp.where(same2 & jnp.logical_not(same), a, 0.0).astype(BF16) for a in mats]
        ib = [v.astype(BF16) for v in invs]
        mid = [mm(o, v).astype(BF16) for o, v in zip(offs, ib)]
        invs = [v - mm(vb, m) for v, vb, m in zip(invs, ib, mid)]
        same = same2
    return invs


def _gdn_kernel(q_ref, k_ref, v_ref, gt_ref, gc_ref, cwq_ref, cwk_ref, cwv_ref, hq_ref, hk_ref, hv_ref,
                s0_ref, nc_ref, y_ref, s1_ref, s_s, eq_s, ek_s, ev_s, qn_s, kn_s, vn_s, nq_s, b_s, o0_s,
                *, L, n_chunks, n_heads):
    h0 = pl.program_id(1) * n_heads
    t = pl.program_id(2)
    tb = q_ref.shape[0]
    hist = GDN_HIST_ROWS

    @pl.when(t == 0)
    def _():
        s_s[...] = s0_ref[0]
        eq_s[0:hist, :] = _round_bf16(hq_ref[0])
        ek_s[0:hist, :] = _round_bf16(hk_ref[0])
        ev_s[0:hist, :] = _round_bf16(hv_ref[0])

    def conv_silu(x_ref, e_s, cw_ref):
        e_s[hist:hist + tb, :] = _round_bf16(x_ref[...])
        cw = _round_bf16(cw_ref[...])
        acc = e_s[hist - (CONV_W - 1):hist - (CONV_W - 1) + tb, :] * cw[0:1, :]
        for tap in range(1, CONV_W):
            lo = hist - (CONV_W - 1) + tap
            acc = acc + e_s[lo:lo + tb, :] * cw[tap:tap + 1, :]
        e_s[0:hist, :] = e_s[tb:tb + hist, :]
        return _silu(acc)

    hd = HEAD_DIM
    heads = range(n_heads)
    cols = [slice(j * hd, (j + 1) * hd) for j in heads]

    def l2n(x):
        parts = [x[:, hs] for hs in cols]
        return jnp.concatenate(
            [p * lax.rsqrt(jnp.sum(p * p, axis=-1, keepdims=True) + RMS_EPS) for p in parts], axis=1)

    qn_s[...] = l2n(conv_silu(q_ref, eq_s, cwq_ref)) * HEAD_DIM ** -0.5
    kn_s[...] = l2n(conv_silu(k_ref, ek_s, cwk_ref))
    vn_s[...] = conv_silu(v_ref, ev_s, cwv_ref)

    r, c = _chunk_masks(L)
    lower = r >= c
    strict = r > c
    ncw = nc_ref[...]

    items = [(j, ci) for j in heads for ci in range(n_chunks)]
    rows = [slice(ci * L, (ci + 1) * L) for _, ci in items]
    qs = [qn_s[sl, cols[j]] for sl, (j, _) in zip(rows, items)]
    ks = [kn_s[sl, cols[j]] for sl, (j, _) in zip(rows, items)]
    vs = [vn_s[sl, cols[j]] for sl, (j, _) in zip(rows, items)]
    gates = [gc_ref[sl, :] for sl in rows]
    betas = [_lane_pick(g, h0 + j) for g, (j, _) in zip(gates, items)]
    cums = [_cumsum_col_row(_lane_pick(g, H_C + h0 + j), r, c) for g, (j, _) in zip(gates, items)]
    gams = [jnp.where(lower, jnp.exp(jnp.where(lower, gc - gr, 0.0)), 0.0) for gc, gr in cums]
    kqs = [_dot_nt(jnp.concatenate([k, q], axis=0), k) for k, q in zip(ks, qs)]
    tinvs = _unit_lower_inverses(
        [jnp.where(strict, b * kq[0:L, :] * gam, 0.0) for b, kq, gam in zip(betas, kqs, gams)], r, c)
    egs = [jnp.exp(gc) for gc, _ in cums]
    rhss = [jnp.concatenate([v * b, k * (b * eg)], axis=1) for v, k, b, eg in zip(vs, ks, betas, egs)]
    uws = [_dot(ti, rh) for ti, rh in zip(tinvs, rhss)]
    g_lasts = [gc[L - 1:L, :] for gc, _ in cums]
    bns = [_dot_tn(k * jnp.exp(gl - gc), uw) for k, gl, (gc, _), uw in zip(ks, g_lasts, cums, uws)]
    aos = [_dot(kq[L:2 * L, :] * gam, uw) for kq, gam, uw in zip(kqs, gams, uws)]
    for i in range(len(items)):
        nq_s[i, 0:hd, :] = bns[i][:, hd:2 * hd]
        nq_s[i, hd:hd + L, :] = qs[i] * egs[i] - aos[i][:, hd:2 * hd]
        b_s[i] = bns[i][:, 0:hd]
        o0_s[i] = aos[i][:, 0:hd]
    decays = [jnp.exp(gl) for gl in g_lasts]

    smat = [s_s[j] for j in heads]
    for ci in range(n_chunks):
        sl = slice(ci * L, (ci + 1) * L)
        ids = [j * n_chunks + ci for j in heads]
        ns = [_dot(nq_s[i], s) for i, s in zip(ids, smat)]
        for j, i in enumerate(ids):
            o = ns[j][hd:hd + L, :] + o0_s[i]
            y_ref[sl, cols[j]] = _rms(o, ncw) * _silu(gt_ref[sl, cols[j]])
        smat = [decays[i] * s - n[0:hd, :] + b_s[i] for i, s, n in zip(ids, smat, ns)]
    for j in heads:
        s_s[j] = smat[j]

    @pl.when(t == pl.num_programs(2) - 1)
    def _():
        s1_ref[0] = s_s[...]


def _scan_geometry(t_len, block=None):
    tb = min(block or SCAN_BLOCK, t_len)
    L = min(SCAN_CHUNK, t_len)
    return tb, L, t_len // tb, tb // L


def _col_spec(tb, nt, row0_blocks, col_fn, nh=1):
    return pl.BlockSpec((tb, nh * HEAD_DIM), lambda b, g, t: (row0_blocks + b * nt + t, col_fn(g)))


def _state_spec(shape_tail, nh=1):
    nd = len(shape_tail)
    return pl.BlockSpec((1, nh) + shape_tail, lambda b, g, t: (b, g) + (0,) * nd)


def _mlstm_scan(z, gc, y_prev, row0, bsz, t_len, c0, n0, m0, norm_a):
    tb, L, nt, n_chunks = _scan_geometry(t_len)
    rb = row0 // tb
    hb = HEAD_DIM
    nh = SCAN_HEADS
    ng = H_A // nh
    in_specs = [_col_spec(tb, nt, rb, lambda g: g, nh), _col_spec(tb, nt, rb, lambda g: ng + g, nh),
                _col_spec(tb, nt, rb, lambda g: 2 * ng + g, nh), _col_spec(tb, nt, rb, lambda g: 3 * ng + g, nh),
                pl.BlockSpec((tb, GATE_LANES), lambda b, g, t: (rb + b * nt + t, 0)),
                _state_spec((hb, hb), nh), _state_spec((1, hb), nh), _state_spec((1, hb), nh),
                pl.BlockSpec((1, hb), lambda b, g, t: (0, 0))]
    out_specs = [_col_spec(tb, nt, rb, lambda g: g, nh),
                 _state_spec((hb, hb), nh), _state_spec((1, hb), nh), _state_spec((1, hb), nh)]
    out_shape = [jax.ShapeDtypeStruct(y_prev.shape, F32),
                 jax.ShapeDtypeStruct((bsz, H_A, hb, hb), F32), jax.ShapeDtypeStruct((bsz, H_A, 1, hb), F32),
                 jax.ShapeDtypeStruct((bsz, H_A, 1, hb), F32)]
    m0b = jnp.broadcast_to(m0[:, :, None, None], (bsz, H_A, 1, hb))
    kern = functools.partial(_mlstm_kernel_aliased, L=L, n_chunks=n_chunks, n_heads=nh)
    y, c1, n1, m1 = pl.pallas_call(
        kern, grid=(bsz, ng, nt), in_specs=in_specs + [pl.BlockSpec(memory_space=pl.ANY)],
        out_specs=out_specs, out_shape=out_shape,
        scratch_shapes=[pltpu.VMEM((nh, hb, hb), F32), pltpu.VMEM((nh, 1, hb), F32), pltpu.VMEM((nh, 1, hb), F32)],
        input_output_aliases={9: 0},
        compiler_params=_cparams(("parallel", "parallel", "arbitrary")), name="mlstm_scan",
    )(z, z, z, z, gc, c0, n0[:, :, None, :], m0b, norm_a.reshape(1, hb), y_prev)
    return y, c1, n1[:, :, 0, :], m1[:, :, 0, 0]


def _mlstm_kernel_aliased(*refs, **kw):
    return _mlstm_kernel(*refs[:9], *refs[10:], **kw)


def _retention_scan(z, cos, sin, y_prev, row0, bsz, t_len, s0, norm_b):
    tb, L, nt, n_chunks = _scan_geometry(t_len)
    rb = row0 // tb
    hb = HEAD_DIM
    nh = SCAN_HEADS
    ng = H_B // nh
    base = 4 * (H_A // nh)
    in_specs = [_col_spec(tb, nt, rb, lambda g: base + g, nh), _col_spec(tb, nt, rb, lambda g: base + ng + g, nh),
                _col_spec(tb, nt, rb, lambda g: base + 2 * ng + g, nh),
                _col_spec(tb, nt, rb, lambda g: base + 3 * ng + g, nh),
                pl.BlockSpec((tb, hb), lambda b, g, t: (t, 0)), pl.BlockSpec((tb, hb), lambda b, g, t: (t, 0)),
                _state_spec((hb, hb), nh), pl.BlockSpec((1, hb), lambda b, g, t: (0, 0)),
                pl.BlockSpec((nh, L, L), lambda b, g, t: (g, 0, 0)),
                pl.BlockSpec((nh, L, hb), lambda b, g, t: (g, 0, 0)),
                pl.BlockSpec(memory_space=pl.ANY)]
    out_specs = [_col_spec(tb, nt, rb, lambda g: H_A // nh + g, nh), _state_spec((hb, hb), nh)]
    out_shape = [jax.ShapeDtypeStruct(y_prev.shape, F32), jax.ShapeDtypeStruct((bsz, H_B, hb, hb), F32)]
    lg = jnp.log(1.0 - 2.0 ** (-5.0 - jnp.arange(H_B, dtype=F32)))
    idx = jnp.arange(L, dtype=F32)
    rel = idx[:, None] - idx[None, :]
    causal = rel >= 0
    dmat = jnp.where(causal, jnp.exp(jnp.where(causal, rel, 0.0) * lg[:, None, None]), 0.0)
    q_dec = jnp.exp((idx + 1.0) * lg[:, None])
    k_dec = jnp.exp((L - 1.0 - idx) * lg[:, None])
    c_dec = jnp.broadcast_to(jnp.exp(L * lg)[:, None], (H_B, L))
    dvec = jnp.pad(jnp.stack([q_dec, k_dec, c_dec], axis=-1), ((0, 0), (0, 0), (0, hb - 3)))
    kern = functools.partial(_retention_kernel_aliased, L=L, n_chunks=n_chunks, n_heads=nh)
    return pl.pallas_call(
        kern, grid=(bsz, ng, nt), in_specs=in_specs, out_specs=out_specs, out_shape=out_shape,
        scratch_shapes=[pltpu.VMEM((nh, hb, hb), F32)], input_output_aliases={10: 0},
        compiler_params=_cparams(("parallel", "parallel", "arbitrary")), name="retention_scan",
    )(z, z, z, z, cos, sin, s0, norm_b.reshape(1, hb), dmat, dvec, y_prev)


def _retention_kernel_aliased(*refs, **kw):
    return _retention_kernel(*refs[:10], *refs[11:], **kw)


def _gdn_scan(z, gc, y_prev, row0, bsz, t_len, conv_w, conv_hist, s0, norm_c):
    tb, L, nt, n_chunks = _scan_geometry(t_len, GDN_BLOCK)
    rb = row0 // tb
    hb = HEAD_DIM
    nh = GDN_HEADS
    ng = H_C // nh
    cw_spec = lambda off: pl.BlockSpec((CONV_W, nh * hb), lambda b, g, t: (0, off + g))
    hist_spec = lambda off: pl.BlockSpec((1, GDN_HIST_ROWS, nh * hb), lambda b, g, t: (b, 0, off + g))
    in_specs = [_col_spec(tb, nt, rb, lambda g: g, nh), _col_spec(tb, nt, rb, lambda g: ng + g, nh),
                _col_spec(tb, nt, rb, lambda g: 2 * ng + g, nh), _col_spec(tb, nt, rb, lambda g: 3 * ng + g, nh),
                pl.BlockSpec((tb, GATE_LANES), lambda b, g, t: (rb + b * nt + t, 0)),
                cw_spec(0), cw_spec(ng), cw_spec(2 * ng),
                hist_spec(0), hist_spec(ng), hist_spec(2 * ng),
                _state_spec((hb, hb), nh), pl.BlockSpec((1, hb), lambda b, g, t: (0, 0)),
                pl.BlockSpec(memory_space=pl.ANY)]
    out_specs = [_col_spec(tb, nt, rb, lambda g: g, nh), _state_spec((hb, hb), nh)]
    out_shape = [jax.ShapeDtypeStruct(y_prev.shape, F32), jax.ShapeDtypeStruct((bsz, H_C, hb, hb), F32)]
    kern = functools.partial(_gdn_kernel_aliased, L=L, n_chunks=n_chunks, n_heads=nh)
    ext = pltpu.VMEM((tb + GDN_HIST_ROWS, nh * hb), F32)
    blk = pltpu.VMEM((tb, nh * hb), F32)
    n_items = nh * n_chunks
    return pl.pallas_call(
        kern, grid=(bsz, ng, nt), in_specs=in_specs, out_specs=out_specs, out_shape=out_shape,
        scratch_shapes=[pltpu.VMEM((nh, hb, hb), F32), ext, ext, ext, blk, blk, blk,
                        pltpu.VMEM((n_items, hb + L, hb), F32), pltpu.VMEM((n_items, hb, hb), F32),
                        pltpu.VMEM((n_items, L, hb), F32)],
        input_output_aliases={13: 0},
        compiler_params=_cparams(("parallel", "parallel", "arbitrary")), name="gdn_scan",
    )(z, z, z, z, gc, conv_w, conv_w, conv_w, conv_hist, conv_hist, conv_hist, s0, norm_c.reshape(1, hb), y_prev)


def _gdn_kernel_aliased(*refs, **kw):
    return _gdn_kernel(*refs[:13], *refs[14:], **kw)


def _rope_tables(pos):
    half = HEAD_DIM // 2
    inv = ROPE_BASE ** (-jnp.arange(half, dtype=F32) / half)
    ang = pos.astype(F32)[:, None] * inv[None, :]
    cos, sin = jnp.cos(ang), jnp.sin(ang)
    return jnp.concatenate([cos, cos], axis=-1), jnp.concatenate([-sin, sin], axis=-1)


def _gate_rows(*rows):
    out = jnp.zeros((8, GATE_LANES), F32)
    for i, rvals in enumerate(rows):
        out = out.at[i, :rvals.shape[0]].set(rvals.astype(F32))
    return out


def _pad_cols(w, width):
    return jnp.pad(w, ((0, 0), (0, width - w.shape[1])))


def kernel(x_prompt, x_sample, state_mlstm_c, state_mlstm_n, state_mlstm_m, state_ret, state_gdn, state_gdn_conv, norm_mix, norm_ffn, norm_final, ab_w_in, ab_gate_bias, ab_w_out, ab_norm_a, ab_norm_b, c_w_in, c_conv_w, c_a_log, c_dt_bias, c_norm, c_w_out, moe_w_grp, moe_b_grp, moe_w_exp, moe_b_exp, moe_w_gate, moe_w_up, moe_w_down):
    bp, tp, d = x_prompt.shape
    bs, ts, _ = x_sample.shape
    n_p, n_s = bp * tp, bs * ts
    n = n_p + n_s
    x = (x_prompt.reshape(n_p, d), x_sample.reshape(n_s, d))
    hd = HEAD_DIM
    qkv_a = 4 * H_A * hd
    n_gate_a = 2 * H_A
    c_qkv = 3 * H_C * hd
    n_gate_c = 2 * H_C

    def moe_layer(y, xres, w_out, layer):
        w_router = _pad_cols(jnp.concatenate([moe_w_grp[layer], moe_w_exp[layer]], axis=1), LANES)
        b_router = _gate_rows(jnp.concatenate([moe_b_grp[layer], moe_b_exp[layer]]))
        x1, hmoe, rt, cnt = _out_proj_router(y, xres, w_out, norm_ffn[layer], w_router, b_router, n_p)
        gab, second_row0 = _moe(hmoe, rt, cnt[0], moe_w_gate, moe_w_up, moe_w_down, layer)
        return x1, gab, second_row0, rt

    w_in = ab_w_in[0]
    w_main = jnp.concatenate([w_in[:, :qkv_a], w_in[:, qkv_a + n_gate_a:]], axis=1)
    w_gate = _pad_cols(w_in[:, qkv_a:qkv_a + n_gate_a], GATE_LANES)
    z, gc = _in_proj(x, norm_mix[0], w_main, w_gate, _gate_rows(ab_gate_bias[0]), 0, n_p)
    y = jnp.zeros((n, d), F32)
    zeros = lambda *s: jnp.zeros(s, F32)
    cos_p, sin_p = _rope_tables(jnp.arange(tp))
    cos_s, sin_s = _rope_tables(PAST_LEN + jnp.arange(ts))
    y, c_p, n_pm, m_p = _mlstm_scan(z, gc, y, 0, bp, tp, zeros(bp, H_A, hd, hd), zeros(bp, H_A, hd),
                                    zeros(bp, H_A), ab_norm_a[0])
    y, c_s, n_sm, m_s = _mlstm_scan(z, gc, y, n_p, bs, ts, state_mlstm_c[0], state_mlstm_n[0],
                                    state_mlstm_m[0], ab_norm_a[0])
    y, r_p = _retention_scan(z, cos_p, sin_p, y, 0, bp, tp, zeros(bp, H_B, hd, hd), ab_norm_b[0])
    y, r_s = _retention_scan(z, cos_s, sin_s, y, n_p, bs, ts, state_ret[0], ab_norm_b[0])
    x1, gab, second_row0, rt = moe_layer(y, x, ab_w_out[0], 0)

    w_in = c_w_in[0]
    w_main = jnp.concatenate([w_in[:, :c_qkv], w_in[:, c_qkv + n_gate_c:]], axis=1)
    w_gate = _pad_cols(w_in[:, c_qkv:c_qkv + n_gate_c], GATE_LANES)
    zero8 = jnp.zeros((H_C,), F32)
    gparams = _gate_rows(jnp.concatenate([zero8, c_dt_bias[0]]), jnp.concatenate([zero8, c_a_log[0]]))
    x2, z, gc = _in_proj(x1, norm_mix[1], w_main, w_gate, gparams, 1, n_p, comb=(gab, second_row0, rt))
    hist_p = jnp.zeros((bp, GDN_HIST_ROWS, c_qkv), F32)
    hist_s = jnp.pad(state_gdn_conv[0], ((0, 0), (GDN_HIST_ROWS - (CONV_W - 1), 0), (0, 0)))
    y, g_p = _gdn_scan(z, gc, y, 0, bp, tp, c_conv_w[0], hist_p, zeros(bp, H_C, hd, hd), c_norm[0])
    y, g_s = _gdn_scan(z, gc, y, n_p, bs, ts, c_conv_w[0], hist_s, state_gdn[0], c_norm[0])
    keep = CONV_W - 1
    last_rows = lambda row0, b, t: (row0 + np.arange(b)[:, None] * t + np.arange(t - keep, t)[None, :]).reshape(-1)
    conv_p = jnp.take(z, last_rows(0, bp, tp), axis=0)[:, :c_qkv].reshape(bp, keep, c_qkv)
    conv_s = jnp.take(z, last_rows(n_p, bs, ts), axis=0)[:, :c_qkv].reshape(bs, keep, c_qkv)
    x3, gab, second_row0, rt = moe_layer(y, x2, c_w_out[0], 1)

    yf_p, yf_s = _final_norm(x3, gab, second_row0, rt, norm_final, n_p)
    y_prompt = yf_p.reshape(bp, tp, d)
    y_sample = yf_s.reshape(bs, ts, d)
    return (y_prompt, y_sample, c_p[None], n_pm[None], m_p[None], r_p[None], g_p[None], conv_p[None],
            c_s[None], n_sm[None], m_s[None], r_s[None], g_s[None], conv_s[None])
```
